```python
import jax
import jax.numpy as jnp
from jax import lax
import numpy as np

D_MODEL = 4096
BATCH = 1
SEQ = 8192
DEPTH = 2


GRID_W = 64
CTX_LEN = 256
NORM_EPS = 1e-6
NEG_INF = -1e30

MIX_W = D_MODEL
RW_HEAD = 64
RW_W = MIX_W // 2
RW_HEADS = RW_W // RW_HEAD
LORA_W = 96
LORA_A = 96
LORA_G = 256
GN_EPS = 64e-5
NA_HEAD = 128
NA_W = MIX_W // 4
NA_HEADS = NA_W // NA_HEAD
NA_KH = 8
NA_KW = 16
SW_HEAD = 64
SW_W = MIX_W - RW_W - NA_W
SW_HEADS = SW_W // SW_HEAD
SW_KV = SW_HEADS // 8
SW_WIN = 128
SW_BLOCK = 128
ROPE_BASE = 10000.0

RW_COLS = 3 * RW_W + LORA_G + 2 * LORA_W + 2 * LORA_A
NA_COLS = 3 * NA_W
SW_COLS = SW_W + 2 * SW_KV * SW_HEAD
N_IN = RW_COLS + NA_COLS + SW_COLS

N_EXPERTS = 64
N_GROUPS = 8
TOPK_GROUPS = 4
TOP_K = 8
D_EXPERT = 256
ROUTE_SCALE = 2.5

kernel_name = 'hybrid_rwkv7_natten_swa_moe_dit'


def rms_norm(x, g):
    xf = x.astype(jnp.float32)
    y = xf * lax.rsqrt(jnp.mean(xf * xf, axis=-1, keepdims=True) + NORM_EPS)
    return (y * g.astype(jnp.float32)).astype(x.dtype)


def centred_dwconv(u, taps):
    up = jnp.pad(u, ((0, 0), (1, 1), (0, 0)))
    return up[:, :-2] * taps[0] + up[:, 1:-1] * taps[1] + up[:, 2:] * taps[2]


def rope_2d(x, row, col):
    half = x.shape[-1] // 2
    inv = ROPE_BASE ** (-jnp.arange(0, half, 2, dtype=jnp.float32) / half)

    def rot(xp, pos):
        ang = pos.astype(jnp.float32)[:, None] * inv[None, :]
        cos = jnp.cos(ang)[None, :, None, :]
        sin = jnp.sin(ang)[None, :, None, :]
        x1, x2 = jnp.split(xp.astype(jnp.float32), 2, axis=-1)
        return jnp.concatenate([x1 * cos - x2 * sin, x1 * sin + x2 * cos], axis=-1)

    return jnp.concatenate([rot(x[..., :half], row), rot(x[..., half:], col)], axis=-1).astype(x.dtype)


def rwkv_scan(state0, r, w, k, v, kk, a, reverse):
    def step(S, inp):
        r_t, w_t, k_t, v_t, kk_t, a_t = inp
        sa = jnp.einsum('bhvk,bhk->bhv', S, -kk_t)
        S = (S * w_t[:, :, None, :] + sa[..., None] * (kk_t * a_t)[:, :, None, :]
             + v_t[..., None] * k_t[:, :, None, :])
        return S, jnp.einsum('bhvk,bhk->bhv', S, r_t)
    xs = tuple(jnp.moveaxis(t, 1, 0) for t in (r, w, k, v, kk, a))
    S, ys = lax.scan(step, state0, xs, reverse=reverse)
    return S, jnp.moveaxis(ys, 0, 1)


def head_group_norm(y, g, b):
    B, T = y.shape[:2]
    mu = jnp.mean(y, axis=-1, keepdims=True)
    var = jnp.mean(jnp.square(y - mu), axis=-1, keepdims=True)
    yn = (y - mu) * lax.rsqrt(var + GN_EPS)
    return yn.reshape(B, T, -1) * g + b


def rwkv_group(u, s0_f, s0_b, taps, w0, w2, a0, a2, g2, k_k, k_a, r_k, lnx_g, lnx_b, with_output):
    B, T, _ = u.shape
    dtype = u.dtype
    u = centred_dwconv(u, taps).astype(jnp.float32)
    o1 = 3 * RW_W + LORA_G
    r, k, v, gd, wdf, wdb, adf, adb = jnp.split(
        u, [RW_W, 2 * RW_W, 3 * RW_W, o1, o1 + LORA_W, o1 + 2 * LORA_W, o1 + 2 * LORA_W + LORA_A], axis=-1)
    heads = lambda t: t.reshape(B, T, RW_HEADS, RW_HEAD)
    kk = heads(k * k_k)
    kk = kk / jnp.maximum(jnp.sqrt(jnp.sum(kk * kk, axis=-1, keepdims=True)), 1e-12)
    rh, vh = heads(r), heads(v)
    y = jnp.zeros_like(rh)
    bonus = jnp.zeros((B, T, RW_HEADS, 1), jnp.float32)
    finals = []
    for d, (wd, ad, s0, rev) in enumerate(((wdf, adf, s0_f, False), (wdb, adb, s0_b, True))):
        w_log = -jax.nn.softplus(-(w0[d] + jnp.tanh(wd) @ w2[d])) - 0.5
        decay = jnp.exp(-jnp.exp(w_log))
        a = jax.nn.sigmoid(a0[d] + ad @ a2[d])
        kd = heads(k * (1.0 + (a - 1.0) * k_a))
        s_fin, yd = rwkv_scan(s0, rh, heads(decay), kd, vh, kk, heads(a), rev)
        y = y + yd
        bonus = bonus + jnp.sum(rh * kd * r_k, axis=-1, keepdims=True)
        finals.append(s_fin)
    if not with_output:
        return None, finals[0], finals[1]
    g = jax.nn.sigmoid(gd) @ g2
    out = (head_group_norm(y, lnx_g, lnx_b) + (bonus * vh).reshape(B, T, RW_W)) * g
    return out.astype(dtype), finals[0], finals[1]


def na_latent(q, k, v, kc, vc, rpb):
    B, S, H, Dh = q.shape
    rows = S // GRID_W
    kh = min(NA_KH, rows)
    scale = Dh ** -0.5
    grid = lambda t: t.reshape(B, rows, GRID_W, H, Dh)
    qg, kg, vg = grid(q), grid(k), grid(v)
    r = jnp.arange(rows)
    row_idx = jnp.clip(r - kh // 2, 0, rows - kh)[:, None] + jnp.arange(kh)[None, :]
    k_nb = kg[:, row_idx]
    v_nb = vg[:, row_idx]
    cols = jnp.arange(GRID_W)
    col_start = jnp.clip(cols - NA_KW // 2, 0, GRID_W - NA_KW)
    col_ok = (cols[None, :] >= col_start[:, None]) & (cols[None, :] < col_start[:, None] + NA_KW)
    dr = row_idx - r[:, None] + NA_KH - 1
    dc = jnp.clip(cols[None, :] - cols[:, None] + NA_KW - 1, 0, 2 * NA_KW - 2)
    bias = rpb.astype(jnp.float32)[:, dr][..., dc].transpose(1, 0, 3, 2, 4)
    s_nb = jnp.einsum('brqhd,brjkhd->brhqjk', qg, k_nb, preferred_element_type=jnp.float32) * scale + bias
    s_nb = jnp.where(col_ok[:, None, :], s_nb, NEG_INF)
    s_ctx = jnp.einsum('brqhd,blhd->brhql', qg, kc, preferred_element_type=jnp.float32) * scale
    n_nb = kh * GRID_W
    s = jnp.concatenate([s_nb.reshape(B, rows, H, GRID_W, n_nb), s_ctx], axis=-1)
    p = jax.nn.softmax(s, axis=-1).astype(v.dtype)
    p_nb = p[..., :n_nb].reshape(B, rows, H, GRID_W, kh, GRID_W)
    out = (jnp.einsum('brhqjk,brjkhd->brqhd', p_nb, v_nb)
           + jnp.einsum('brhql,blhd->brqhd', p[..., n_nb:], vc))
    return out.reshape(B, S, H * Dh)


def ctx_attn(q, k, v):
    B, L, H, Dh = q.shape
    s = jnp.einsum('blhd,bmhd->bhlm', q, k, preferred_element_type=jnp.float32) * Dh ** -0.5
    p = jax.nn.softmax(s, axis=-1).astype(v.dtype)
    return jnp.einsum('bhlm,bmhd->blhd', p, v).reshape(B, L, H * Dh)


def split_swa(u):
    B, T, _ = u.shape
    q, k, v = jnp.split(u, [SW_W, SW_W + SW_KV * SW_HEAD], axis=-1)
    return (q.reshape(B, T, SW_HEADS, SW_HEAD), k.reshape(B, T, SW_KV, SW_HEAD),
            v.reshape(B, T, SW_KV, SW_HEAD))


def swa_latent(q, k, v, kc, vc, sink):
    B, S, HQ, Dh = q.shape
    G = HQ // SW_KV
    nb = S // SW_BLOCK
    L = kc.shape[1]
    scale = Dh ** -0.5
    qb = q.reshape(B, nb, SW_BLOCK, SW_KV, G, Dh)

    def band(t):
        tp = jnp.pad(t, ((0, 0), (SW_BLOCK, SW_BLOCK), (0, 0), (0, 0))).reshape(B, nb + 2, SW_BLOCK, SW_KV, Dh)
        return jnp.concatenate([tp[:, :-2], tp[:, 1:-1], tp[:, 2:]], axis=2)

    kw, vw = band(k), band(v)
    qpos = jnp.arange(S).reshape(nb, SW_BLOCK)
    kpos = (jnp.arange(nb)[:, None] - 1) * SW_BLOCK + jnp.arange(3 * SW_BLOCK)[None, :]
    ok = ((kpos[:, None, :] >= 0) & (kpos[:, None, :] < S)
          & (jnp.abs(kpos[:, None, :] - qpos[:, :, None]) <= SW_WIN))
    s_loc = jnp.einsum('bnqkgd,bnjkd->bnkgqj', qb, kw, preferred_element_type=jnp.float32) * scale
    s_loc = jnp.where(ok[None, :, None, None], s_loc, NEG_INF)
    s_ctx = jnp.einsum('bnqkgd,blkd->bnkgql', qb, kc, preferred_element_type=jnp.float32) * scale
    sk = jnp.broadcast_to(sink.astype(jnp.float32).reshape(1, 1, SW_KV, G, 1, 1), s_loc.shape[:-1] + (1,))
    p = jax.nn.softmax(jnp.concatenate([s_loc, s_ctx, sk], axis=-1), axis=-1).astype(v.dtype)
    n_loc = 3 * SW_BLOCK
    out = (jnp.einsum('bnkgqj,bnjkd->bnqkgd', p[..., :n_loc], vw)
           + jnp.einsum('bnkgql,blkd->bnqkgd', p[..., n_loc:n_loc + L], vc))
    return out.reshape(B, S, HQ * Dh)


def swa_ctx(q, k, v, sink):
    B, L, HQ, Dh = q.shape
    G = HQ // SW_KV
    qg = q.reshape(B, L, SW_KV, G, Dh)
    s = jnp.einsum('blkgd,bmkd->bkglm', qg, k, preferred_element_type=jnp.float32) * Dh ** -0.5
    sk = jnp.broadcast_to(sink.astype(jnp.float32).reshape(1, SW_KV, G, 1, 1), (B, SW_KV, G, L, 1))
    p = jax.nn.softmax(jnp.concatenate([s, sk], axis=-1), axis=-1)[..., :L].astype(v.dtype)
    return jnp.einsum('bkglm,bmkd->blkgd', p, v).reshape(B, L, HQ * Dh)


def moe(h, w_router, r_bias, we_gate, we_up, we_down, ws_gate, ws_up, ws_down):
    B, T, D = h.shape
    hf = h.reshape(B * T, D)
    scores = jax.nn.sigmoid(jnp.dot(hf, w_router, preferred_element_type=jnp.float32))
    biased = scores + r_bias.astype(jnp.float32)
    epg = N_EXPERTS // N_GROUPS
    grp_score = lax.top_k(biased.reshape(-1, N_GROUPS, epg), 2)[0].sum(-1)
    _, grp_idx = lax.top_k(grp_score, TOPK_GROUPS)
    grp_keep = jax.nn.one_hot(grp_idx, N_GROUPS, dtype=jnp.float32).sum(-2) > 0
    keep = jnp.repeat(grp_keep, epg, axis=-1)
    _, top_idx = lax.top_k(jnp.where(keep, biased, NEG_INF), TOP_K)
    top_w = jnp.take_along_axis(scores, top_idx, axis=-1)
    top_w = ROUTE_SCALE * top_w / jnp.sum(top_w, axis=-1, keepdims=True)
    gates = jnp.einsum('tk,tke->te', top_w, jax.nn.one_hot(top_idx, N_EXPERTS, dtype=jnp.float32)).astype(h.dtype)
    out = jnp.dot(jax.nn.silu(hf @ ws_gate) * (hf @ ws_up), ws_down)
    for gi in range(N_GROUPS):
        sl = slice(gi * epg, (gi + 1) * epg)
        hg = jnp.einsum('td,edf->tef', hf, we_gate[sl])
        hu = jnp.einsum('td,edf->tef', hf, we_up[sl])
        act = jax.nn.silu(hg) * hu * gates[:, sl, None]
        out = out + jnp.einsum('tef,efd->td', act, we_down[sl])
    return out.reshape(B, T, D)


def setup_inputs(seed: int = 0) -> dict:
    key = jax.random.key(seed)
    ks = iter(jax.random.split(key, 40))
    nrm = lambda shape, s: jax.random.normal(next(ks), shape, jnp.float32) * s
    D = D_MODEL
    inp = {}
    inp['x'] = nrm((BATCH, SEQ, D), 1.0)
    inp['c'] = nrm((BATCH, D), 1.0)
    inp['ctx'] = nrm((BATCH, CTX_LEN, D), 1.0)
    inp['c_ctx'] = nrm((D,), 1.0)
    inp['w_ada'] = nrm((DEPTH, D, 6 * D), 0.5 * D ** -0.5)
    inp['b_ada'] = nrm((DEPTH, 6 * D), 0.02)
    inp['norm1_g'] = 1.0 + nrm((DEPTH, D), 0.05)
    inp['norm2_g'] = 1.0 + nrm((DEPTH, D), 0.05)
    inp['w_in'] = nrm((DEPTH, D, N_IN), D ** -0.5)
    inp['rw_conv'] = jnp.array([0.2, 1.0, 0.2], jnp.float32)[None, :, None] + nrm((DEPTH, 3, RW_COLS), 0.1)
    inp['rw_w0'] = nrm((DEPTH, 2, RW_W), 0.5) - 0.5
    inp['rw_w2'] = nrm((DEPTH, 2, LORA_W, RW_W), 0.1 * LORA_W ** -0.5)
    inp['rw_a0'] = nrm((DEPTH, 2, RW_W), 0.5)
    inp['rw_a2'] = nrm((DEPTH, 2, LORA_A, RW_W), 0.5 * LORA_A ** -0.5)
    inp['rw_g2'] = nrm((DEPTH, LORA_G, RW_W), LORA_G ** -0.5)
    inp['rw_kk'] = 0.85 + nrm((DEPTH, RW_W), 0.05)
    inp['rw_ka'] = 1.0 + nrm((DEPTH, RW_W), 0.05)
    inp['rw_rk'] = nrm((DEPTH, RW_HEADS, RW_HEAD), 0.1)
    inp['rw_lnx_g'] = 1.0 + nrm((DEPTH, RW_W), 0.05)
    inp['rw_lnx_b'] = nrm((DEPTH, RW_W), 0.02)
    inp['na_rpb'] = nrm((DEPTH, NA_HEADS, 2 * NA_KH - 1, 2 * NA_KW - 1), 0.1)
    inp['sw_sink'] = nrm((DEPTH, SW_HEADS), 0.5)
    inp['w_out'] = nrm((DEPTH, MIX_W, D), MIX_W ** -0.5)
    inp['w_router'] = nrm((DEPTH, D, N_EXPERTS), D ** -0.5)
    inp['router_bias'] = nrm((DEPTH, N_EXPERTS), 0.01)
    inp['we_gate'] = nrm((DEPTH, N_EXPERTS, D, D_EXPERT), D ** -0.5)
    inp['we_up'] = nrm((DEPTH, N_EXPERTS, D, D_EXPERT), D ** -0.5)
    inp['we_down'] = nrm((DEPTH, N_EXPERTS, D_EXPERT, D), D_EXPERT ** -0.5)
    inp['ws_gate'] = nrm((DEPTH, D, D_EXPERT), D ** -0.5)
    inp['ws_up'] = nrm((DEPTH, D, D_EXPERT), D ** -0.5)
    inp['ws_down'] = nrm((DEPTH, D_EXPERT, D), D_EXPERT ** -0.5)
    inp['final_g'] = 1.0 + nrm((D,), 0.05)
    return inp


def reference(x, c, ctx, c_ctx, w_ada, b_ada, norm1_g, norm2_g, w_in, rw_conv, rw_w0, rw_w2, rw_a0, rw_a2,
              rw_g2, rw_kk, rw_ka, rw_rk, rw_lnx_g, rw_lnx_b, na_rpb, sw_sink, w_out, w_router, router_bias,
              we_gate, we_up, we_down, ws_gate, ws_up, ws_down, final_g):
    B, S, D = x.shape
    t = jnp.arange(S)
    row = t // GRID_W
    col = t % GRID_W
    silu_c = jax.nn.silu(c)
    silu_cc = jax.nn.silu(c_ctx)
    xl, xc = x, ctx
    for i in range(DEPTH):
        ctx_needed = i < DEPTH - 1
        mod_l = (jnp.dot(silu_c, w_ada[i]) + b_ada[i])[:, None, :]
        mod_c = jnp.dot(silu_cc, w_ada[i]) + b_ada[i]
        sh1l, sc1l, ga1l, sh2l, sc2l, ga2l = jnp.split(mod_l, 6, axis=-1)
        sh1c, sc1c, ga1c, sh2c, sc2c, ga2c = jnp.split(mod_c, 6, axis=-1)
        hl = rms_norm(xl, norm1_g[i]) * (1 + sc1l) + sh1l
        hc = rms_norm(xc, norm1_g[i]) * (1 + sc1c) + sh1c
        al, bl, cl = jnp.split(hl @ w_in[i], [RW_COLS, RW_COLS + NA_COLS], axis=-1)
        ac, bc, cc = jnp.split(hc @ w_in[i], [RW_COLS, RW_COLS + NA_COLS], axis=-1)
        rw_prm = (rw_conv[i], rw_w0[i], rw_w2[i], rw_a0[i], rw_a2[i], rw_g2[i], rw_kk[i], rw_ka[i],
                  rw_rk[i], rw_lnx_g[i], rw_lnx_b[i])
        z = jnp.zeros((B, RW_HEADS, RW_HEAD, RW_HEAD), jnp.float32)
        rwc, s_f, s_b = rwkv_group(ac, z, z, *rw_prm, ctx_needed)
        rwl, _, _ = rwkv_group(al, s_f, s_b, *rw_prm, True)
        qbl, kbl, vbl = [u.reshape(B, S, NA_HEADS, NA_HEAD) for u in jnp.split(bl, 3, axis=-1)]
        qbc, kbc, vbc = [u.reshape(B, ctx.shape[1], NA_HEADS, NA_HEAD) for u in jnp.split(bc, 3, axis=-1)]
        nal = na_latent(qbl, kbl, vbl, kbc, vbc, na_rpb[i])
        qcl, kcl, vcl = split_swa(cl)
        qcc, kcc, vcc = split_swa(cc)
        swl = swa_latent(rope_2d(qcl, row, col), rope_2d(kcl, row, col), vcl, kcc, vcc, sw_sink[i])
        xl = xl + ga1l * (jnp.concatenate([rwl, nal, swl], axis=-1) @ w_out[i])
        moe_prm = (w_router[i], router_bias[i], we_gate[i], we_up[i], we_down[i], ws_gate[i], ws_up[i], ws_down[i])
        xl = xl + ga2l * moe(rms_norm(xl, norm2_g[i]) * (1 + sc2l) + sh2l, *moe_prm)
        if ctx_needed:
            nac = ctx_attn(qbc, kbc, vbc)
            swc = swa_ctx(qcc, kcc, vcc, sw_sink[i])
            xc = xc + ga1c * (jnp.concatenate([rwc, nac, swc], axis=-1) @ w_out[i])
            xc = xc + ga2c * moe(rms_norm(xc, norm2_g[i]) * (1 + sc2c) + sh2c, *moe_prm)
    return rms_norm(xl, final_g)
```

```python
import functools

import jax
import jax.numpy as jnp
from jax import lax
from jax.experimental import pallas as pl
from jax.experimental.pallas import tpu as pltpu

F32 = jnp.float32
BF16 = jnp.bfloat16

D_MODEL = 4096
DEPTH = 2
GRID_W = 64
CTX_LEN = 256
NORM_EPS = 1e-6
NEG_INF = -1e30

RW_HEAD = 64
RW_W = D_MODEL // 2
RW_HEADS = RW_W // RW_HEAD
LORA_W = 96
LORA_A = 96
LORA_G = 256
GN_EPS = 64e-5
RW_COLS = 3 * RW_W + LORA_G + 2 * LORA_W + 2 * LORA_A
LORA_OFF = 3 * RW_W + LORA_G
LORA_ALL = 2 * LORA_W + 2 * LORA_A

NA_HEAD = 128
NA_W = D_MODEL // 4
NA_HEADS = NA_W // NA_HEAD
NA_KH = 8
NA_KW = 16
NA_COLS = 3 * NA_W

SW_HEAD = 64
SW_W = D_MODEL - RW_W - NA_W
SW_HEADS = SW_W // SW_HEAD
SW_KV = SW_HEADS // 8
SW_GROUP = SW_HEADS // SW_KV
SW_WIN = 128
SW_BLOCK = 128
ROPE_BASE = 10000.0
SW_COLS = SW_W + 2 * SW_KV * SW_HEAD
ATT_COLS = NA_COLS + SW_COLS

N_EXPERTS = 64
N_GROUPS = 8
EXPERTS_PER_GROUP = N_EXPERTS // N_GROUPS
TOPK_GROUPS = 4
TOP_K = 8
D_EXPERT = 256
ROUTE_SCALE = 2.5

LANES = 128
SUBLANES = 8
RW_COLS_PAD = 6912
CHUNK = 64
VMEM_LIMIT = 56 * 1024 * 1024


def _cparams(n_axes, vmem=VMEM_LIMIT):
    return pltpu.CompilerParams(dimension_semantics=("arbitrary",) * n_axes, vmem_limit_bytes=vmem)


def _dot(a, b):
    return jnp.dot(a, b, preferred_element_type=F32)


def _dot_nt(a, b):
    return lax.dot_general(a, b, (((1,), (1,)), ((), ())), preferred_element_type=F32)


def _split2(x):
    hi = x.astype(BF16)
    lo = (x - hi.astype(F32)).astype(BF16)
    return hi, lo


def _split3(x):
    hi = x.astype(BF16)
    r1 = x - hi.astype(F32)
    mid = r1.astype(BF16)
    lo = (r1 - mid.astype(F32)).astype(BF16)
    return hi, mid, lo


def _sigmoid(x):
    return 1.0 / (1.0 + jnp.exp(-x))


def _ada_body(s_ref, w_ref, b_ref, o_ref):
    c = s_ref[...]
    s = (c * _sigmoid(c)).astype(BF16)
    o_ref[0] = _dot(s, w_ref[0].astype(BF16)) + b_ref[0]


def ada_mod(cond8, w_ada, b_ada):
    bn = 512
    n = 6 * D_MODEL
    return pl.pallas_call(
        _ada_body,
        grid=(DEPTH, n // bn),
        in_specs=[
            pl.BlockSpec((SUBLANES, D_MODEL), lambda l, j: (0, 0)),
            pl.BlockSpec((1, D_MODEL, bn), lambda l, j: (l, 0, j)),
            pl.BlockSpec((1, 1, bn), lambda l, j: (l, 0, j)),
        ],
        out_specs=pl.BlockSpec((1, SUBLANES, bn), lambda l, j: (l, 0, j)),
        out_shape=jax.ShapeDtypeStruct((DEPTH, SUBLANES, n), F32),
        compiler_params=_cparams(2),
        name="ada_mod",
    )(cond8, w_ada, b_ada.reshape(DEPTH, 1, n))


def _norm_mod_body(x_ref, g_ref, sc_ref, sh_ref, o_ref):
    x = x_ref[...]
    ms = jnp.mean(x * x, axis=-1, keepdims=True)
    y = x * lax.rsqrt(ms + NORM_EPS) * g_ref[...]
    o_ref[...] = (y * (1.0 + sc_ref[...]) + sh_ref[...]).astype(o_ref.dtype)


def norm_mod(x, g, sc, sh, out_dtype=BF16):
    t = x.shape[0]
    tm = 256
    row = pl.BlockSpec((1, D_MODEL), lambda i: (0, 0))
    return pl.pallas_call(
        _norm_mod_body,
        grid=(t // tm,),
        in_specs=[pl.BlockSpec((tm, D_MODEL), lambda i: (i, 0)), row, row, row],
        out_specs=pl.BlockSpec((tm, D_MODEL), lambda i: (i, 0)),
        out_shape=jax.ShapeDtypeStruct((t, D_MODEL), out_dtype),
        compiler_params=_cparams(1),
        name="norm_mod",
    )(x, g, sc, sh)


def _final_norm_body(x_ref, g_ref, o_ref):
    x = x_ref[...]
    ms = jnp.mean(x * x, axis=-1, keepdims=True)
    o_ref[...] = x * lax.rsqrt(ms + NORM_EPS) * g_ref[...]


def final_norm(x, g):
    t = x.shape[0]
    tm = 256
    return pl.pallas_call(
        _final_norm_body,
        grid=(t // tm,),
        in_specs=[pl.BlockSpec((tm, D_MODEL), lambda i: (i, 0)), pl.BlockSpec((1, D_MODEL), lambda i: (0, 0))],
        out_specs=pl.BlockSpec((tm, D_MODEL), lambda i: (i, 0)),
        out_shape=jax.ShapeDtypeStruct((t, D_MODEL), F32),
        compiler_params=_cparams(1),
        name="final_norm",
    )(x, g)


def _mm_body(*refs, n_in, residual):
    a_refs = refs[:n_in]
    w_refs = refs[n_in:2 * n_in]
    rest = refs[2 * n_in:]
    acc = _dot(a_refs[0][...], w_refs[0][...])
    for a_ref, w_ref in zip(a_refs[1:], w_refs[1:]):
        acc = acc + _dot(a_ref[...], w_ref[...])
    if residual:
        x_ref, ga_ref, o_ref = rest
        o_ref[...] = x_ref[...] + ga_ref[...] * acc
    else:
        (o_ref,) = rest
        o_ref[...] = acc.astype(o_ref.dtype)


def matmul(a_list, w_list, bn, out_dtype, residual=None, name="matmul"):
    m = a_list[0].shape[0]
    n = w_list[0].shape[1]
    bm = 1024 if m % 1024 == 0 else 256
    in_specs = [pl.BlockSpec((bm, a.shape[1]), lambda i, j: (i, 0)) for a in a_list]
    in_specs += [pl.BlockSpec((w.shape[0], bn), lambda i, j: (0, j)) for w in w_list]
    args = list(a_list) + list(w_list)
    if residual is not None:
        in_specs += [pl.BlockSpec((bm, bn), lambda i, j: (i, j)), pl.BlockSpec((1, bn), lambda i, j: (0, j))]
        args += list(residual)
    return pl.pallas_call(
        functools.partial(_mm_body, n_in=len(a_list), residual=residual is not None),
        grid=(m // bm, n // bn),
        in_specs=in_specs,
        out_specs=pl.BlockSpec((bm, bn), lambda i, j: (i, j)),
        out_shape=jax.ShapeDtypeStruct((m, n), out_dtype),
        compiler_params=_cparams(2),
        name=name,
    )(*args)


RW_TM = 128


def _seg_sum_bcast(x, e_ref, et_ref):
    hi, lo = _split2(x)
    s = _dot(hi, e_ref[...]) + _dot(lo, e_ref[...])
    shi, slo = _split2(s)
    return _dot(shi, et_ref[...]) + _dot(slo, et_ref[...])


def _rw_prep_body(u_ref, up_ref, un_ref, taps_ref, w0_ref, a0_ref, w2_ref, a2_ref, g2_ref, kks_ref, ka_ref,
                  rk_ref, e_ref, et_ref, tri_ref, sel_ref,
                  v_ref, g_ref, bv_ref, rt_ref, kkt_ref, kbar_ref, bbar_ref, gam_ref):
    i = pl.program_id(0)
    nt = pl.num_programs(0)
    tm = RW_TM
    rows = lax.broadcasted_iota(jnp.int32, (tm, 1), 0)
    has_prev = (i > 0).astype(F32)
    has_next = (i < nt - 1).astype(F32)

    def conv(c0, c1):
        u = u_ref[:, c0:c1]
        prev_row = up_ref[SUBLANES - 1:SUBLANES, c0:c1] * has_prev
        next_row = un_ref[0:1, c0:c1] * has_next
        u_prev = jnp.where(rows == 0, prev_row, pltpu.roll(u, 1, 0))
        u_next = jnp.where(rows == tm - 1, next_row, pltpu.roll(u, tm - 1, 0))
        return (u_prev * taps_ref[0:1, c0:c1] + u * taps_ref[1:2, c0:c1] + u_next * taps_ref[2:3, c0:c1])

    r = conv(0, RW_W)
    k = conv(RW_W, 2 * RW_W)
    v = conv(2 * RW_W, 3 * RW_W)
    gd = conv(3 * RW_W, LORA_OFF)
    lo_in = conv(LORA_OFF, LORA_OFF + LORA_ALL)

    v_ref[...] = v.astype(v_ref.dtype)
    g_ref[...] = _dot(_sigmoid(gd).astype(BF16), g2_ref[...])

    kk = k * kks_ref[...]
    ssq = _seg_sum_bcast(kk * kk, e_ref, et_ref)
    kkn = kk / jnp.maximum(jnp.sqrt(ssq), 1e-12)

    tanh_lo = jnp.tanh(lo_in).astype(BF16)
    raw_lo = lo_in.astype(BF16)
    bonus = jnp.zeros((tm, RW_W), F32)
    for d in range(2):
        w_pre = w0_ref[d:d + 1, :] + _dot(tanh_lo, w2_ref[d])
        z = -w_pre
        softplus = jnp.maximum(z, 0.0) + jnp.log(1.0 + jnp.exp(-jnp.abs(z)))
        w_log = -softplus - 0.5
        lw = -jnp.exp(w_log)
        a = _sigmoid(a0_ref[d:d + 1, :] + _dot(raw_lo, a2_ref[d]))
        kd = k * (1.0 + (a - 1.0) * ka_ref[...])
        b = kkn * a
        bonus = bonus + r * kd * rk_ref[...]
        l1, l2, l3 = _split3(lw)
        tri = tri_ref[d]
        sel = sel_ref[...]
        cum = _dot(tri, l1) + _dot(tri, l2) + _dot(tri, l3)
        tot = _dot(sel, l1) + _dot(sel, l2) + _dot(sel, l3)
        e_inv = jnp.exp(-cum)
        rt_ref[d] = (r * jnp.exp(cum)).astype(BF16)
        kkt_ref[d] = (kkn * jnp.exp(cum - lw)).astype(BF16)
        kbar_ref[d] = (kd * e_inv).astype(BF16)
        bbar_ref[d] = (b * e_inv).astype(BF16)
        gam_ref[d] = jnp.exp(tot)
    bv_ref[...] = _seg_sum_bcast(bonus, e_ref, et_ref) * v


def _rw_consts():
    tm = RW_TM
    t = jnp.arange(tm)
    same = (t[:, None] // CHUNK) == (t[None, :] // CHUNK)
    tri_f = same & (t[None, :] <= t[:, None])
    tri_b = same & (t[None, :] >= t[:, None])
    tri = jnp.stack([tri_f, tri_b]).astype(BF16)
    sel = ((t[None, :] // CHUNK) == jnp.arange(SUBLANES)[:, None]).astype(BF16)
    lane_head = jnp.arange(RW_W) // RW_HEAD
    e = (lane_head[:, None] == jnp.arange(LANES)[None, :]).astype(BF16)
    return tri, sel, e, e.T


def rw_prep(u, prm):
    t = u.shape[0]
    tm = RW_TM
    nt = t // tm
    hb = tm // SUBLANES
    tri, sel, e, et = _rw_consts()
    full = lambda shp: pl.BlockSpec(shp, lambda i: (0,) * len(shp))
    in_specs = [
        pl.BlockSpec((tm, RW_COLS_PAD), lambda i: (i, 0)),
        pl.BlockSpec((SUBLANES, RW_COLS_PAD), lambda i: (jnp.maximum(i * hb - 1, 0), 0)),
        pl.BlockSpec((SUBLANES, RW_COLS_PAD), lambda i: (jnp.minimum((i + 1) * hb, nt * hb - 1), 0)),
        full((SUBLANES, RW_COLS_PAD)),
        full((2, RW_W)), full((2, RW_W)),
        full((2, LORA_ALL, RW_W)), full((2, LORA_ALL, RW_W)),
        full((LORA_G, RW_W)),
        full((1, RW_W)), full((1, RW_W)), full((1, RW_W)),
        full((RW_W, LANES)), full((LANES, RW_W)),
        full((2, tm, tm)), full((SUBLANES, tm)),
    ]
    tok = lambda dt: jax.ShapeDtypeStruct((t, RW_W), dt)
    tok2 = lambda dt: jax.ShapeDtypeStruct((2, t, RW_W), dt)
    spec1 = pl.BlockSpec((tm, RW_W), lambda i: (i, 0))
    spec2 = pl.BlockSpec((2, tm, RW_W), lambda i: (0, i, 0))
    out_shape = [tok(BF16), tok(F32), tok(F32)] + [tok2(BF16)] * 4 + [
        jax.ShapeDtypeStruct((2, nt * SUBLANES, RW_W), F32)]
    out_specs = [spec1, spec1, spec1] + [spec2] * 4 + [pl.BlockSpec((2, SUBLANES, RW_W), lambda i: (0, i, 0))]
    return pl.pallas_call(
        _rw_prep_body,
        grid=(nt,),
        in_specs=in_specs,
        out_specs=out_specs,
        out_shape=out_shape,
        compiler_params=_cparams(1),
        name="rw_prep",
    )(u, u, u, prm["taps"], prm["w0"], prm["a0"], prm["w2"], prm["a2"], prm["g2"], prm["kks"], prm["ka"],
      prm["rk"], e, et, tri, sel)


SCAN_HB = 8
SCAN_CB = 4


def _rw_scan_body(kb_ref, qt_ref, vt_ref, gam_ref, s0_ref, yt_ref, sfin_ref, s_scr, *, reverse):
    c = pl.program_id(1)
    nc = pl.num_programs(1)

    @pl.when(c == 0)
    def _():
        s_scr[...] = s0_ref[...]

    ri = lax.broadcasted_iota(jnp.int32, (CHUNK, CHUNK), 0)
    ci = lax.broadcasted_iota(jnp.int32, (CHUNK, CHUNK), 1)
    strict = (ri > ci) if reverse else (ri < ci)
    incl = (ri >= ci) if reverse else (ri <= ci)
    eye = (ri == ci).astype(F32)
    level_masks = []
    b = 1
    while b < CHUNK:
        level_masks.append(((ri // (2 * b)) == (ci // (2 * b))) & ((ri // b) != (ci // b)))
        b *= 2

    def chunk_step(jj, carry):
        j = (SCAN_CB - 1 - jj) if reverse else jj
        for h in range(SCAN_HB):
            s = s_scr[h]
            kb = kb_ref[h, j]
            qt = qt_ref[h, j]
            kkt = qt[:CHUNK]
            rt = qt[CHUNK:]
            vt = vt_ref[h, j]
            ak = _dot(kb, kkt)
            ar = _dot(kb, rt)
            lkt = jnp.where(strict, ak[:CHUNK], 0.0).astype(BF16)
            n = jnp.where(strict, ak[CHUNK:], 0.0)
            arkt = jnp.where(incl, ar[:CHUNK], 0.0).astype(BF16)
            arbt = jnp.where(incl, ar[CHUNK:], 0.0).astype(BF16)
            x = eye
            for mask in level_masks:
                xb = x.astype(BF16)
                nm = jnp.where(mask, n, 0.0).astype(BF16)
                x = x - _dot(xb, _dot(nm, xb).astype(BF16))
            sb = s.astype(BF16)
            w = _dot(sb, kkt) + _dot(vt, lkt)
            ut = _dot(w.astype(BF16), x.astype(BF16))
            utb = ut.astype(BF16)
            yt_ref[h, j] = _dot(sb, rt) + _dot(vt, arkt) - _dot(utb, arbt)
            gam = gam_ref[h, j]
            kbh = (kb.astype(F32) * gam).astype(BF16)
            s_scr[h] = s * gam + _dot(vt, kbh[:CHUNK]) - _dot(utb, kbh[CHUNK:])
        return carry

    lax.fori_loop(0, SCAN_CB, chunk_step, 0)

    @pl.when(c == nc - 1)
    def _():
        sfin_ref[...] = s_scr[...]


def rw_scan(kb, qt, vt, gam, s0, reverse):
    nh, nc = kb.shape[0], kb.shape[1]
    ncb = nc // SCAN_CB
    tix = (lambda g, c: (g, ncb - 1 - c, 0, 0)) if reverse else (lambda g, c: (g, c, 0, 0))
    big = pl.BlockSpec((SCAN_HB, SCAN_CB, 2 * CHUNK, CHUNK), tix)
    sq = pl.BlockSpec((SCAN_HB, SCAN_CB, CHUNK, CHUNK), tix)
    st = pl.BlockSpec((SCAN_HB, CHUNK, CHUNK), lambda g, c: (g, 0, 0))
    return pl.pallas_call(
        functools.partial(_rw_scan_body, reverse=reverse),
        grid=(nh // SCAN_HB, ncb),
        in_specs=[big, big, sq, pl.BlockSpec((SCAN_HB, SCAN_CB, 1, CHUNK), tix), st],
        out_specs=[sq, st],
        out_shape=[jax.ShapeDtypeStruct((nh, nc, CHUNK, CHUNK), F32),
                   jax.ShapeDtypeStruct((nh, CHUNK, CHUNK), F32)],
        scratch_shapes=[pltpu.VMEM((SCAN_HB, CHUNK, CHUNK), F32)],
        compiler_params=_cparams(2),
        name="rw_scan_rev" if reverse else "rw_scan_fwd",
    )(kb, qt, vt, gam, s0)


def _rw_post_body(yf_ref, yb_ref, bv_ref, g_ref, lg_ref, lb_ref, e_ref, et_ref, o_ref):
    y = yf_ref[...] + yb_ref[...]
    mu = _seg_sum_bcast(y, e_ref, et_ref) * (1.0 / RW_HEAD)
    dlt = y - mu
    var = _seg_sum_bcast(dlt * dlt, e_ref, et_ref) * (1.0 / RW_HEAD)
    yn = dlt * lax.rsqrt(var + GN_EPS)
    o_ref[...] = ((yn * lg_ref[...] + lb_ref[...] + bv_ref[...]) * g_ref[...]).astype(o_ref.dtype)


def rw_post(yf, yb, bv, g, lnx_g, lnx_b):
    t = yf.shape[0]
    tm = 256
    _, _, e, et = _rw_consts()
    tokspec = pl.BlockSpec((tm, RW_W), lambda i: (i, 0))
    row = pl.BlockSpec((1, RW_W), lambda i: (0, 0))
    return pl.pallas_call(
        _rw_post_body,
        grid=(t // tm,),
        in_specs=[tokspec, tokspec, tokspec, tokspec, row, row,
                  pl.BlockSpec((RW_W, LANES), lambda i: (0, 0)), pl.BlockSpec((LANES, RW_W), lambda i: (0, 0))],
        out_specs=tokspec,
        out_shape=jax.ShapeDtypeStruct((t, RW_W), BF16),
        compiler_params=_cparams(1),
        name="rw_post",
    )(yf, yb, bv, g, lnx_g, lnx_b, e, et)


def _to_heads_time_major(x):
    t = x.shape[-2]
    x = x.reshape(t // CHUNK, CHUNK, RW_HEADS, RW_HEAD)
    return x.transpose(2, 0, 1, 3)


def _to_heads_channel_major(x):
    t = x.shape[-2]
    x = x.reshape(t // CHUNK, CHUNK, RW_HEADS, RW_HEAD)
    return x.transpose(2, 0, 3, 1)


def _from_heads_channel_major(y):
    nh, nc = y.shape[0], y.shape[1]
    return y.transpose(1, 3, 0, 2).reshape(nc * CHUNK, nh * RW_HEAD)


def rwkv_group(u, prm, s0_f, s0_b, with_output):
    v, g, bv, rt, kkt, kbar, bbar, gam = rw_prep(u, prm)
    vt = _to_heads_channel_major(v)
    chunks_per_tile = RW_TM // CHUNK
    outs = []
    finals = []
    for d, s0 in ((0, s0_f), (1, s0_b)):
        kb = jnp.concatenate([_to_heads_time_major(kbar[d]), _to_heads_time_major(bbar[d])], axis=2)
        qt = jnp.concatenate([_to_heads_channel_major(kkt[d]), _to_heads_channel_major(rt[d])], axis=2)
        nc = kb.shape[1]
        gm = gam[d].reshape(-1, SUBLANES, RW_HEADS, RW_HEAD)[:, :chunks_per_tile].reshape(nc, RW_HEADS, RW_HEAD)
        gm = gm.transpose(1, 0, 2)[:, :, None, :]
        yt, sfin = rw_scan(kb, qt, vt, gm, s0, reverse=(d == 1))
        outs.append(yt)
        finals.append(sfin)
    if not with_output:
        return None, finals[0], finals[1]
    out = rw_post(_from_heads_channel_major(outs[0]), _from_heads_channel_major(outs[1]), bv, g,
                  prm["lnx_g"], prm["lnx_b"])
    return out, finals[0], finals[1]


NA_RQ = 4


def _na_body(q_ref, k_ref, v_ref, kc_ref, vc_ref, bt_ref, o_ref, *, n_rows):
    rb = pl.program_id(1)
    scale = NA_HEAD ** -0.5
    kc = kc_ref[...]
    vc = vc_ref[...]
    for qi in range(NA_RQ):
        r = rb * NA_RQ + qi
        start = jnp.clip(r - NA_KH // 2, 0, n_rows - NA_KH)
        dr0 = start - r + NA_KH - 1
        tok0 = pl.multiple_of(start * GRID_W, GRID_W)
        q = q_ref[qi * GRID_W:(qi + 1) * GRID_W, :]
        kw = k_ref[pl.ds(tok0, NA_KH * GRID_W), :]
        vw = v_ref[pl.ds(tok0, NA_KH * GRID_W), :]
        s = _dot_nt(q, kw) * scale
        bias = jnp.concatenate([bt_ref[0, dr0 + 2 * m] for m in range(NA_KH // 2)], axis=1)
        s = s + bias
        sc = _dot_nt(q, kc) * scale
        mx = jnp.maximum(jnp.max(s, axis=-1, keepdims=True), jnp.max(sc, axis=-1, keepdims=True))
        p = jnp.exp(s - mx)
        pc = jnp.exp(sc - mx)
        den = jnp.sum(p, axis=-1, keepdims=True) + jnp.sum(pc, axis=-1, keepdims=True)
        o = (_dot(p.astype(BF16), vw) + _dot(pc.astype(BF16), vc)) / den
        o_ref[qi * GRID_W:(qi + 1) * GRID_W, :] = o.astype(o_ref.dtype)


def _na_bias_table(rpb):
    cols = jnp.arange(GRID_W)
    col_start = jnp.clip(cols - NA_KW // 2, 0, GRID_W - NA_KW)
    col_ok = (cols[None, :] >= col_start[:, None]) & (cols[None, :] < col_start[:, None] + NA_KW)
    dc = jnp.clip(cols[None, :] - cols[:, None] + NA_KW - 1, 0, 2 * NA_KW - 2)
    bt = rpb.astype(F32)[:, :, dc]
    bt = jnp.where(col_ok[None, None], bt, NEG_INF)
    return jnp.concatenate([bt[:, :-1], bt[:, 1:]], axis=-1)


def na_latent(att_l, att_c, rpb):
    s_len = att_l.shape[0]
    n_rows = s_len // GRID_W
    qb = NA_W // NA_HEAD
    bt = _na_bias_table(rpb)
    return pl.pallas_call(
        functools.partial(_na_body, n_rows=n_rows),
        grid=(NA_HEADS, n_rows // NA_RQ),
        in_specs=[
            pl.BlockSpec((NA_RQ * GRID_W, NA_HEAD), lambda h, r: (r, h)),
            pl.BlockSpec((s_len, NA_HEAD), lambda h, r: (0, qb + h)),
            pl.BlockSpec((s_len, NA_HEAD), lambda h, r: (0, 2 * qb + h)),
            pl.BlockSpec((CTX_LEN, NA_HEAD), lambda h, r: (0, qb + h)),
            pl.BlockSpec((CTX_LEN, NA_HEAD), lambda h, r: (0, 2 * qb + h)),
            pl.BlockSpec((1, 2 * NA_KH - 2, GRID_W, 2 * GRID_W), lambda h, r: (h, 0, 0, 0)),
        ],
        out_specs=pl.BlockSpec((NA_RQ * GRID_W, NA_HEAD), lambda h, r: (r, h)),
        out_shape=jax.ShapeDtypeStruct((s_len, NA_W), BF16),
        compiler_params=_cparams(2),
        name="na_latent",
    )(att_l, att_l, att_l, att_c, att_c, bt)


def _na_ctx_body(q_ref, k_ref, v_ref, o_ref):
    s = _dot_nt(q_ref[...], k_ref[...]) * (NA_HEAD ** -0.5)
    mx = jnp.max(s, axis=-1, keepdims=True)
    p = jnp.exp(s - mx)
    den = jnp.sum(p, axis=-1, keepdims=True)
    o_ref[...] = (_dot(p.astype(BF16), v_ref[...]) / den).astype(o_ref.dtype)


def na_ctx(att_c):
    qb = NA_W // NA_HEAD
    blk = lambda off: pl.BlockSpec((CTX_LEN, NA_HEAD), lambda h: (0, off + h))
    return pl.pallas_call(
        _na_ctx_body,
        grid=(NA_HEADS,),
        in_specs=[blk(0), blk(qb), blk(2 * qb)],
        out_specs=blk(0),
        out_shape=jax.ShapeDtypeStruct((CTX_LEN, NA_W), BF16),
        compiler_params=_cparams(1),
        name="na_ctx",
    )(att_c, att_c, att_c)


def _rope_body(x_ref, c_ref, s1_ref, s2_ref, o_ref):
    x = x_ref[...].astype(F32)
    quarter = SW_HEAD // 4
    x_up = pltpu.roll(x, LANES - quarter, 1)
    x_dn = pltpu.roll(x, quarter, 1)
    o_ref[...] = (x * c_ref[...] + x_up * s1_ref[...] + x_dn * s2_ref[...]).astype(o_ref.dtype)


def _rope_tables(s_len):
    t = jnp.arange(s_len)
    row = (t // GRID_W).astype(F32)
    col = (t % GRID_W).astype(F32)
    half = SW_HEAD // 2
    inv = ROPE_BASE ** (-jnp.arange(0, half, 2, dtype=F32) / half)
    ang_r = row[:, None] * inv[None, :]
    ang_c = col[:, None] * inv[None, :]
    cos = jnp.concatenate([jnp.cos(ang_r)] * 2 + [jnp.cos(ang_c)] * 2, axis=-1)
    sin_r, sin_c = jnp.sin(ang_r), jnp.sin(ang_c)
    zero = jnp.zeros_like(sin_r)
    s_up = jnp.concatenate([-sin_r, zero, -sin_c, zero], axis=-1)
    s_dn = jnp.concatenate([zero, sin_r, zero, sin_c], axis=-1)
    tile2 = lambda a: jnp.concatenate([a, a], axis=-1)
    return tile2(cos), tile2(s_up), tile2(s_dn)


def rope_qk(att_l):
    s_len = att_l.shape[0]
    tm = 512
    nblk = (SW_W + SW_KV * SW_HEAD) // LANES
    off = NA_COLS // LANES
    cos, s_up, s_dn = _rope_tables(s_len)
    tab = pl.BlockSpec((tm, LANES), lambda i, j: (i, 0))
    return pl.pallas_call(
        _rope_body,
        grid=(s_len // tm, nblk),
        in_specs=[pl.BlockSpec((tm, LANES), lambda i, j: (i, off + j)), tab, tab, tab],
        out_specs=pl.BlockSpec((tm, LANES), lambda i, j: (i, j)),
        out_shape=jax.ShapeDtypeStruct((s_len, nblk * LANES), BF16),
        compiler_params=_cparams(2),
        name="rope_qk",
    )(att_l, cos, s_up, s_dn)


def _sw_softmax_out(q8, s_parts, v_parts, sink_col):
    mx = sink_col
    for s in s_parts:
        mx = jnp.maximum(mx, jnp.max(s, axis=-1, keepdims=True))
    den = jnp.exp(sink_col - mx)
    o = None
    for s, v in zip(s_parts, v_parts):
        p = jnp.exp(s - mx)
        den = den + jnp.sum(p, axis=-1, keepdims=True)
        pv = _dot(p.astype(BF16), v)
        o = pv if o is None else o + pv
    return o / den


def _sink_column(sink_ref, kh, rows_per_head):
    n = SW_GROUP * rows_per_head
    grp = lax.broadcasted_iota(jnp.int32, (n, 1), 0) // rows_per_head
    col = jnp.zeros((n, 1), F32)
    for g in range(SW_GROUP):
        col = jnp.where(grp == g, sink_ref[kh * SW_GROUP + g], col)
    return col


def _sw_body(sink_ref, q_ref, k_ref, v_ref, kc_ref, vc_ref, o_ref, *, s_len):
    n = pl.program_id(0)
    scale = SW_HEAD ** -0.5
    win = 3 * SW_BLOCK
    start = pl.multiple_of(jnp.clip((n - 1) * SW_BLOCK, 0, s_len - win), SW_BLOCK)
    k2 = k_ref[pl.ds(start, win), :]
    v2 = v_ref[pl.ds(start, win), :]
    qpos = n * SW_BLOCK + lax.broadcasted_iota(jnp.int32, (SW_BLOCK, win), 0)
    kpos = start + lax.broadcasted_iota(jnp.int32, (SW_BLOCK, win), 1)
    ok = jnp.abs(kpos - qpos) <= SW_WIN
    for kh in range(SW_KV):
        lanes = slice(kh * SW_HEAD, (kh + 1) * SW_HEAD)
        q8 = jnp.concatenate(
            [q_ref[:, (kh * SW_GROUP + g) * SW_HEAD:(kh * SW_GROUP + g + 1) * SW_HEAD] for g in range(SW_GROUP)],
            axis=0)
        s_loc = _dot_nt(q8, k2[:, lanes]) * scale
        s_loc = jnp.where(ok[None], s_loc.reshape(SW_GROUP, SW_BLOCK, win), NEG_INF).reshape(
            SW_GROUP * SW_BLOCK, win)
        s_ctx = _dot_nt(q8, kc_ref[:, lanes]) * scale
        o = _sw_softmax_out(q8, [s_loc, s_ctx], [v2[:, lanes], vc_ref[:, lanes]],
                            _sink_column(sink_ref, kh, SW_BLOCK))
        for g in range(SW_GROUP):
            c0 = (kh * SW_GROUP + g) * SW_HEAD
            o_ref[:, c0:c0 + SW_HEAD] = o[g * SW_BLOCK:(g + 1) * SW_BLOCK].astype(o_ref.dtype)


def swa_latent(qk_rot, att_l, att_c, sink):
    s_len = qk_rot.shape[0]
    kcol = SW_W // LANES
    ck = (NA_COLS + SW_W) // LANES
    return pl.pallas_call(
        functools.partial(_sw_body, s_len=s_len),
        grid=(s_len // SW_BLOCK,),
        in_specs=[
            pl.BlockSpec(memory_space=pltpu.SMEM),
            pl.BlockSpec((SW_BLOCK, SW_W), lambda n: (n, 0)),
            pl.BlockSpec((s_len, LANES), lambda n: (0, kcol)),
            pl.BlockSpec((s_len, LANES), lambda n: (0, ck + 1)),
            pl.BlockSpec((CTX_LEN, LANES), lambda n: (0, ck)),
            pl.BlockSpec((CTX_LEN, LANES), lambda n: (0, ck + 1)),
        ],
        out_specs=pl.BlockSpec((SW_BLOCK, SW_W), lambda n: (n, 0)),
        out_shape=jax.ShapeDtypeStruct((s_len, SW_W), BF16),
        compiler_params=_cparams(1),
        name="swa_latent",
    )(sink, qk_rot, qk_rot, att_l, att_c, att_c)


def _sw_ctx_body(sink_ref, q_ref, k_ref, v_ref, o_ref):
    scale = SW_HEAD ** -0.5
    for kh in range(SW_KV):
        lanes = slice(kh * SW_HEAD, (kh + 1) * SW_HEAD)
        q8 = jnp.concatenate(
            [q_ref[:, (kh * SW_GROUP + g) * SW_HEAD:(kh * SW_GROUP + g + 1) * SW_HEAD] for g in range(SW_GROUP)],
            axis=0)
        s = _dot_nt(q8, k_ref[:, lanes]) * scale
        o = _sw_softmax_out(q8, [s], [v_ref[:, lanes]], _sink_column(sink_ref, kh, CTX_LEN))
        for g in range(SW_GROUP):
            c0 = (kh * SW_GROUP + g) * SW_HEAD
            o_ref[:, c0:c0 + SW_HEAD] = o[g * CTX_LEN:(g + 1) * CTX_LEN].astype(o_ref.dtype)


def swa_ctx(att_c, sink):
    qo = NA_COLS // SW_W
    ck = (NA_COLS + SW_W) // LANES
    return pl.pallas_call(
        _sw_ctx_body,
        grid=(1,),
        in_specs=[
            pl.BlockSpec(memory_space=pltpu.SMEM),
            pl.BlockSpec((CTX_LEN, SW_W), lambda n: (0, qo)),
            pl.BlockSpec((CTX_LEN, LANES), lambda n: (0, ck)),
            pl.BlockSpec((CTX_LEN, LANES), lambda n: (0, ck + 1)),
        ],
        out_specs=pl.BlockSpec((CTX_LEN, SW_W), lambda n: (0, 0)),
        out_shape=jax.ShapeDtypeStruct((CTX_LEN, SW_W), BF16),
        compiler_params=_cparams(1),
        name="swa_ctx",
    )(sink, att_c, att_c, att_c)


def _router_body(x_ref, g_ref, sc_ref, sh_ref, wr_ref, rb_ref, h_ref, gate_ref):
    x = x_ref[...]
    tm = x.shape[0]
    ms = jnp.mean(x * x, axis=-1, keepdims=True)
    h = (x * lax.rsqrt(ms + NORM_EPS) * g_ref[...]) * (1.0 + sc_ref[...]) + sh_ref[...]
    h_ref[...] = h.astype(h_ref.dtype)

    hh, hl = _split2(h)
    wh, wl = _split2(wr_ref[...])
    logits = _dot(hh, wh) + _dot(hl, wh) + _dot(hh, wl)
    lane = lax.broadcasted_iota(jnp.int32, (tm, LANES), 1)
    lane_f = lane.astype(F32)
    valid = lane < N_EXPERTS
    neg = -jnp.inf
    big = float(2 * LANES)
    scores = _sigmoid(logits)
    biased = jnp.where(valid, scores + rb_ref[...], neg)
    grp = lane // EXPERTS_PER_GROUP

    def first_argmax(vals):
        m = jnp.max(vals, axis=-1, keepdims=True)
        idx = jnp.min(jnp.where(vals == m, lane_f, big), axis=-1, keepdims=True)
        return m, idx

    gscore = jnp.full((tm, LANES), neg, F32)
    for gi in range(N_GROUPS):
        vg = jnp.where(grp == gi, biased, neg)
        m1, i1 = first_argmax(vg)
        m2 = jnp.max(jnp.where(lane_f == i1, neg, vg), axis=-1, keepdims=True)
        gscore = jnp.where(lane == gi * EXPERTS_PER_GROUP, m1 + m2, gscore)
    keep = jnp.zeros((tm, LANES), jnp.bool_)
    for _ in range(TOPK_GROUPS):
        _, gi = first_argmax(gscore)
        keep = keep | (grp.astype(F32) == jnp.floor(gi * (1.0 / EXPERTS_PER_GROUP)))
        gscore = jnp.where(lane_f == gi, neg, gscore)
    cur = jnp.where(valid, jnp.where(keep, biased, NEG_INF), neg)
    sel = jnp.zeros((tm, LANES), jnp.bool_)
    for _ in range(TOP_K):
        _, ei = first_argmax(cur)
        hit = lane_f == ei
        sel = sel | hit
        cur = jnp.where(hit, neg, cur)
    w = jnp.where(sel, scores, 0.0)
    gates = ROUTE_SCALE * w / jnp.sum(w, axis=-1, keepdims=True)
    gate_ref[...] = jnp.where(lane == N_EXPERTS, 1.0, gates)


def router(x, g, sc, sh, w_router_pad, r_bias_pad):
    t = x.shape[0]
    tm = 256
    row = pl.BlockSpec((1, D_MODEL), lambda i: (0, 0))
    return pl.pallas_call(
        _router_body,
        grid=(t // tm,),
        in_specs=[pl.BlockSpec((tm, D_MODEL), lambda i: (i, 0)), row, row, row,
                  pl.BlockSpec((D_MODEL, LANES), lambda i: (0, 0)), pl.BlockSpec((1, LANES), lambda i: (0, 0))],
        out_specs=[pl.BlockSpec((tm, D_MODEL), lambda i: (i, 0)), pl.BlockSpec((tm, LANES), lambda i: (i, 0))],
        out_shape=[jax.ShapeDtypeStruct((t, D_MODEL), BF16), jax.ShapeDtypeStruct((t, LANES), F32)],
        compiler_params=_cparams(1),
        name="router",
    )(x, g, sc, sh, w_router_pad, r_bias_pad)


MOE_DOWN_COLS = 1024


def _moe_body(h_ref, gate_ref, wg_ref, wu_ref, wd_ref, x_ref, ga_ref, o_ref):
    e = pl.program_id(1)
    ne = pl.num_programs(1)

    @pl.when(e == 0)
    def _():
        o_ref[...] = jnp.zeros_like(o_ref)

    h = h_ref[...]
    hg = _dot(h, wg_ref[0])
    hu = _dot(h, wu_ref[0])
    lane = lax.broadcasted_iota(jnp.int32, gate_ref.shape, 1)
    gate = jnp.sum(jnp.where(lane == e, gate_ref[...], 0.0), axis=-1, keepdims=True)
    act = ((hg * _sigmoid(hg)) * hu * gate).astype(BF16)
    for c0 in range(0, D_MODEL, MOE_DOWN_COLS):
        cols = slice(c0, c0 + MOE_DOWN_COLS)
        o_ref[:, cols] += _dot(act, wd_ref[0, :, cols])

    @pl.when(e == ne - 1)
    def _():
        o_ref[...] = x_ref[...] + ga_ref[...] * o_ref[...]


def moe_experts(h, gates, wg, wu, wd, x, ga):
    t = h.shape[0]
    bm = 512 if t % 512 == 0 else 256
    ne = wg.shape[0]
    tok = pl.BlockSpec((bm, D_MODEL), lambda i, e: (i, 0))
    return pl.pallas_call(
        _moe_body,
        grid=(t // bm, ne),
        in_specs=[tok, pl.BlockSpec((bm, LANES), lambda i, e: (i, 0)),
                  pl.BlockSpec((1, D_MODEL, D_EXPERT), lambda i, e: (e, 0, 0)),
                  pl.BlockSpec((1, D_MODEL, D_EXPERT), lambda i, e: (e, 0, 0)),
                  pl.BlockSpec((1, D_EXPERT, D_MODEL), lambda i, e: (e, 0, 0)),
                  pl.BlockSpec((bm, D_MODEL), lambda i, e: (i, 0), pipeline_mode=pl.Buffered(1)),
                  pl.BlockSpec((1, D_MODEL), lambda i, e: (0, 0))],
        out_specs=tok,
        out_shape=jax.ShapeDtypeStruct((t, D_MODEL), F32),
        compiler_params=_cparams(2),
        name="moe_experts",
    )(h, gates, wg, wu, wd, x, ga)


def _pad_cols(a, n):
    return jnp.pad(a, [(0, 0)] * (a.ndim - 1) + [(0, n - a.shape[-1])])


def _rw_params(i, rw_conv, rw_w0, rw_w2, rw_a0, rw_a2, rw_g2, rw_kk, rw_ka, rw_rk, rw_lnx_g, rw_lnx_b):
    w2 = jnp.zeros((2, LORA_ALL, RW_W), F32)
    a2 = jnp.zeros((2, LORA_ALL, RW_W), F32)
    for d in range(2):
        w2 = w2.at[d, d * LORA_W:(d + 1) * LORA_W].set(rw_w2[i, d])
        a2 = a2.at[d, 2 * LORA_W + d * LORA_A:2 * LORA_W + (d + 1) * LORA_A].set(rw_a2[i, d])
    taps = jnp.pad(rw_conv[i], ((0, SUBLANES - 3), (0, RW_COLS_PAD - RW_COLS)))
    return dict(
        taps=taps, w0=rw_w0[i], a0=rw_a0[i], w2=w2.astype(BF16), a2=a2.astype(BF16), g2=rw_g2[i].astype(BF16),
        kks=rw_kk[i][None], ka=rw_ka[i][None], rk=rw_rk[i].reshape(1, RW_W),
        lnx_g=rw_lnx_g[i][None], lnx_b=rw_lnx_b[i][None])


def kernel(x, c, ctx, c_ctx, w_ada, b_ada, norm1_g, norm2_g, w_in, rw_conv, rw_w0, rw_w2, rw_a0, rw_a2, rw_g2,
           rw_kk, rw_ka, rw_rk, rw_lnx_g, rw_lnx_b, na_rpb, sw_sink, w_out, w_router, router_bias, we_gate,
           we_up, we_down, ws_gate, ws_up, ws_down, final_g):
    assert x.shape[0] == 1 and x.shape[2] == D_MODEL and ctx.shape[1] == CTX_LEN
    xl = x[0]
    xc = ctx[0]
    cond8 = jnp.zeros((SUBLANES, D_MODEL), F32).at[0].set(c[0]).at[1].set(c_ctx)
    mods = ada_mod(cond8, w_ada, b_ada)

    for i in range(DEPTH):
        ctx_needed = i < DEPTH - 1
        mod_l = [m[None] for m in mods[i, 0].reshape(6, D_MODEL)]
        mod_c = [m[None] for m in mods[i, 1].reshape(6, D_MODEL)]
        w_rw = _pad_cols(w_in[i][:, :RW_COLS], RW_COLS_PAD).astype(BF16)
        w_att = w_in[i][:, RW_COLS:].astype(BF16)
        w_o = w_out[i].astype(BF16)
        w_o_parts = [w_o[:RW_W], w_o[RW_W:RW_W + NA_W], w_o[RW_W + NA_W:]]
        rw_prm = _rw_params(i, rw_conv, rw_w0, rw_w2, rw_a0, rw_a2, rw_g2, rw_kk, rw_ka, rw_rk, rw_lnx_g,
                            rw_lnx_b)
        n1 = norm1_g[i][None]
        n2 = norm2_g[i][None]

        hl = norm_mod(xl, n1, mod_l[1], mod_l[0])
        hc = norm_mod(xc, n1, mod_c[1], mod_c[0])
        url = matmul([hl], [w_rw], 768, F32, name="in_proj_rw")
        urc = matmul([hc], [w_rw], 768, F32, name="in_proj_rw_ctx")
        ual = matmul([hl], [w_att], 256, BF16, name="in_proj_att")
        uac = matmul([hc], [w_att], 256, BF16, name="in_proj_att_ctx")

        z = jnp.zeros((RW_HEADS, RW_HEAD, RW_HEAD), F32)
        rwc, s_f, s_b = rwkv_group(urc, rw_prm, z, z, ctx_needed)
        rwl, _, _ = rwkv_group(url, rw_prm, s_f, s_b, True)
        nal = na_latent(ual, uac, na_rpb[i])
        swl = swa_latent(rope_qk(ual), ual, uac, sw_sink[i])
        xl = matmul([rwl, nal, swl], w_o_parts, 512, F32, residual=(xl, mod_l[2]), name="out_proj")

        wr = _pad_cols(w_router[i], LANES)
        rb = _pad_cols(router_bias[i][None], LANES)
        wg = jnp.concatenate([we_gate[i], ws_gate[i][None]], axis=0).astype(BF16)
        wu = jnp.concatenate([we_up[i], ws_up[i][None]], axis=0).astype(BF16)
        wd = jnp.concatenate([we_down[i], ws_down[i][None]], axis=0).astype(BF16)
        h2, gates = router(xl, n2, mod_l[4], mod_l[3], wr, rb)
        xl = moe_experts(h2, gates, wg, wu, wd, xl, mod_l[5])
        if ctx_needed:
            nac = na_ctx(uac)
            swc = swa_ctx(uac, sw_sink[i])
            xc = matmul([rwc, nac, swc], w_o_parts, 512, F32, residual=(xc, mod_c[2]), name="out_proj_ctx")
            h2c, gates_c = router(xc, n2, mod_c[4], mod_c[3], wr, rb)
            xc = moe_experts(h2c, gates_c, wg, wu, wd, xc, mod_c[5])
    return final_norm(xl, final_g[None])[None]
```

```python
import functools

import jax
import jax.numpy as jnp
from jax import lax
from jax.experimental import pallas as pl
from jax.experimental.pallas import tpu as pltpu

F32 = jnp.float32
BF16 = jnp.bfloat16

D_MODEL = 4096
DEPTH = 2
GRID_W = 64
CTX_LEN = 256
NORM_EPS = 1e-6
NEG_INF = -1e30

RW_HEAD = 64
RW_W = D_MODEL // 2
RW_HEADS = RW_W // RW_HEAD
LORA_W = 96
LORA_A = 96
LORA_G = 256
GN_EPS = 64e-5
RW_COLS = 3 * RW_W + LORA_G + 2 * LORA_W + 2 * LORA_A
LORA_OFF = 3 * RW_W + LORA_G
LORA_ALL = 2 * LORA_W + 2 * LORA_A

NA_HEAD = 128
NA_W = D_MODEL // 4
NA_HEADS = NA_W // NA_HEAD
NA_KH = 8
NA_KW = 16
NA_COLS = 3 * NA_W

SW_HEAD = 64
SW_W = D_MODEL - RW_W - NA_W
SW_HEADS = SW_W // SW_HEAD
SW_KV = SW_HEADS // 8
SW_GROUP = SW_HEADS // SW_KV
SW_WIN = 128
SW_BLOCK = 128
ROPE_BASE = 10000.0
SW_COLS = SW_W + 2 * SW_KV * SW_HEAD
ATT_COLS = NA_COLS + SW_COLS

N_EXPERTS = 64
N_GROUPS = 8
EXPERTS_PER_GROUP = N_EXPERTS // N_GROUPS
TOPK_GROUPS = 4
TOP_K = 8
D_EXPERT = 256
ROUTE_SCALE = 2.5

LANES = 128
SUBLANES = 8
RW_COLS_PAD = 6912
CHUNK = 64
VMEM_LIMIT = 56 * 1024 * 1024


def _cparams(n_axes, vmem=VMEM_LIMIT):
    return pltpu.CompilerParams(dimension_semantics=("arbitrary",) * n_axes, vmem_limit_bytes=vmem)


def _dot(a, b):
    return jnp.dot(a, b, preferred_element_type=F32)


def _dot_nt(a, b):
    return lax.dot_general(a, b, (((1,), (1,)), ((), ())), preferred_element_type=F32)


def _split2(x):
    hi = x.astype(BF16)
    lo = (x - hi.astype(F32)).astype(BF16)
    return hi, lo


def _split3(x):
    hi = x.astype(BF16)
    r1 = x - hi.astype(F32)
    mid = r1.astype(BF16)
    lo = (r1 - mid.astype(F32)).astype(BF16)
    return hi, mid, lo


def _sigmoid(x):
    return 1.0 / (1.0 + jnp.exp(-x))


def _ada_body(s_ref, w_ref, b_ref, o_ref):
    c = s_ref[...]
    s = (c * _sigmoid(c)).astype(BF16)
    o_ref[0] = _dot(s, w_ref[0].astype(BF16)) + b_ref[0]


def ada_mod(cond8, w_ada, b_ada):
    bn = 512
    n = 6 * D_MODEL
    return pl.pallas_call(
        _ada_body,
        grid=(DEPTH, n // bn),
        in_specs=[
            pl.BlockSpec((SUBLANES, D_MODEL), lambda l, j: (0, 0)),
            pl.BlockSpec((1, D_MODEL, bn), lambda l, j: (l, 0, j)),
            pl.BlockSpec((1, 1, bn), lambda l, j: (l, 0, j)),
        ],
        out_specs=pl.BlockSpec((1, SUBLANES, bn), lambda l, j: (l, 0, j)),
        out_shape=jax.ShapeDtypeStruct((DEPTH, SUBLANES, n), F32),
        compiler_params=_cparams(2),
        name="ada_mod",
    )(cond8, w_ada, b_ada.reshape(DEPTH, 1, n))


def _norm_mod_body(x_ref, g_ref, sc_ref, sh_ref, o_ref):
    x = x_ref[...]
    ms = jnp.mean(x * x, axis=-1, keepdims=True)
    y = x * lax.rsqrt(ms + NORM_EPS) * g_ref[...]
    o_ref[...] = (y * (1.0 + sc_ref[...]) + sh_ref[...]).astype(o_ref.dtype)


def norm_mod(x, g, sc, sh, out_dtype=BF16):
    t = x.shape[0]
    tm = 256
    row = pl.BlockSpec((1, D_MODEL), lambda i: (0, 0))
    return pl.pallas_call(
        _norm_mod_body,
        grid=(t // tm,),
        in_specs=[pl.BlockSpec((tm, D_MODEL), lambda i: (i, 0)), row, row, row],
        out_specs=pl.BlockSpec((tm, D_MODEL), lambda i: (i, 0)),
        out_shape=jax.ShapeDtypeStruct((t, D_MODEL), out_dtype),
        compiler_params=_cparams(1),
        name="norm_mod",
    )(x, g, sc, sh)


def _final_norm_body(x_ref, g_ref, o_ref):
    x = x_ref[...]
    ms = jnp.mean(x * x, axis=-1, keepdims=True)
    o_ref[...] = x * lax.rsqrt(ms + NORM_EPS) * g_ref[...]


def final_norm(x, g):
    t = x.shape[0]
    tm = 256
    return pl.pallas_call(
        _final_norm_body,
        grid=(t // tm,),
        in_specs=[pl.BlockSpec((tm, D_MODEL), lambda i: (i, 0)), pl.BlockSpec((1, D_MODEL), lambda i: (0, 0))],
        out_specs=pl.BlockSpec((tm, D_MODEL), lambda i: (i, 0)),
        out_shape=jax.ShapeDtypeStruct((t, D_MODEL), F32),
        compiler_params=_cparams(1),
        name="final_norm",
    )(x, g)


def _mm_body(*refs, n_in, residual):
    a_refs = refs[:n_in]
    w_refs = refs[n_in:2 * n_in]
    rest = refs[2 * n_in:]
    acc = _dot(a_refs[0][...], w_refs[0][...])
    for a_ref, w_ref in zip(a_refs[1:], w_refs[1:]):
        acc = acc + _dot(a_ref[...], w_ref[...])
    if residual:
        x_ref, ga_ref, o_ref = rest
        o_ref[...] = x_ref[...] + ga_ref[...] * acc
    else:
        (o_ref,) = rest
        o_ref[...] = acc.astype(o_ref.dtype)


def matmul(a_list, w_list, bn, out_dtype, residual=None, name="matmul"):
    m = a_list[0].shape[0]
    n = w_list[0].shape[1]
    bm = 1024 if m % 1024 == 0 else 256
    in_specs = [pl.BlockSpec((bm, a.shape[1]), lambda i, j: (i, 0)) for a in a_list]
    in_specs += [pl.BlockSpec((w.shape[0], bn), lambda i, j: (0, j)) for w in w_list]
    args = list(a_list) + list(w_list)
    if residual is not None:
        in_specs += [pl.BlockSpec((bm, bn), lambda i, j: (i, j)), pl.BlockSpec((1, bn), lambda i, j: (0, j))]
        args += list(residual)
    return pl.pallas_call(
        functools.partial(_mm_body, n_in=len(a_list), residual=residual is not None),
        grid=(m // bm, n // bn),
        in_specs=in_specs,
        out_specs=pl.BlockSpec((bm, bn), lambda i, j: (i, j)),
        out_shape=jax.ShapeDtypeStruct((m, n), out_dtype),
        compiler_params=_cparams(2),
        name=name,
    )(*args)


RW_TM = 128


def _seg_sum_bcast(x, e_ref, et_ref):
    hi, lo = _split2(x)
    s = _dot(hi, e_ref[...]) + _dot(lo, e_ref[...])
    shi, slo = _split2(s)
    return _dot(shi, et_ref[...]) + _dot(slo, et_ref[...])


def _rw_prep_body(u_ref, up_ref, un_ref, taps_ref, w0_ref, a0_ref, w2_ref, a2_ref, g2_ref, kks_ref, ka_ref,
                  rk_ref, e_ref, et_ref, tri_ref, sel_ref,
                  v_ref, g_ref, bv_ref, rt_ref, kkt_ref, kbar_ref, bbar_ref, gam_ref):
    i = pl.program_id(0)
    nt = pl.num_programs(0)
    tm = RW_TM
    rows = lax.broadcasted_iota(jnp.int32, (tm, 1), 0)
    has_prev = (i > 0).astype(F32)
    has_next = (i < nt - 1).astype(F32)

    def conv(c0, c1):
        u = u_ref[:, c0:c1]
        prev_row = up_ref[SUBLANES - 1:SUBLANES, c0:c1] * has_prev
        next_row = un_ref[0:1, c0:c1] * has_next
        u_prev = jnp.where(rows == 0, prev_row, pltpu.roll(u, 1, 0))
        u_next = jnp.where(rows == tm - 1, next_row, pltpu.roll(u, tm - 1, 0))
        return (u_prev * taps_ref[0:1, c0:c1] + u * taps_ref[1:2, c0:c1] + u_next * taps_ref[2:3, c0:c1])

    r = conv(0, RW_W)
    k = conv(RW_W, 2 * RW_W)
    v = conv(2 * RW_W, 3 * RW_W)
    gd = conv(3 * RW_W, LORA_OFF)
    lo_in = conv(LORA_OFF, LORA_OFF + LORA_ALL)

    v_ref[...] = v.astype(v_ref.dtype)
    g_ref[...] = _dot(_sigmoid(gd).astype(BF16), g2_ref[...])

    kk = k * kks_ref[...]
    ssq = _seg_sum_bcast(kk * kk, e_ref, et_ref)
    kkn = kk / jnp.maximum(jnp.sqrt(ssq), 1e-12)

    tanh_lo = jnp.tanh(lo_in).astype(BF16)
    raw_lo = lo_in.astype(BF16)
    bonus = jnp.zeros((tm, RW_W), F32)
    for d in range(2):
        w_pre = w0_ref[d:d + 1, :] + _dot(tanh_lo, w2_ref[d])
        z = -w_pre
        softplus = jnp.maximum(z, 0.0) + jnp.log(1.0 + jnp.exp(-jnp.abs(z)))
        w_log = -softplus - 0.5
        lw = -jnp.exp(w_log)
        a = _sigmoid(a0_ref[d:d + 1, :] + _dot(raw_lo, a2_ref[d]))
        kd = k * (1.0 + (a - 1.0) * ka_ref[...])
        b = kkn * a
        bonus = bonus + r * kd * rk_ref[...]
        l1, l2, l3 = _split3(lw)
        tri = tri_ref[d]
        sel = sel_ref[...]
        cum = _dot(tri, l1) + _dot(tri, l2) + _dot(tri, l3)
        tot = _dot(sel, l1) + _dot(sel, l2) + _dot(sel, l3)
        e_inv = jnp.exp(-cum)
        rt_ref[d] = (r * jnp.exp(cum)).astype(BF16)
        kkt_ref[d] = (kkn * jnp.exp(cum - lw)).astype(BF16)
        kbar_ref[d] = (kd * e_inv).T.astype(BF16)
        bbar_ref[d] = (b * e_inv).T.astype(BF16)
        gam_ref[d] = jnp.exp(tot)
    bv_ref[...] = _seg_sum_bcast(bonus, e_ref, et_ref) * v


def _rw_consts():
    tm = RW_TM
    t = jnp.arange(tm)
    same = (t[:, None] // CHUNK) == (t[None, :] // CHUNK)
    tri_f = same & (t[None, :] <= t[:, None])
    tri_b = same & (t[None, :] >= t[:, None])
    tri = jnp.stack([tri_f, tri_b]).astype(BF16)
    sel = ((t[None, :] // CHUNK) == jnp.arange(SUBLANES)[:, None]).astype(BF16)
    lane_head = jnp.arange(RW_W) // RW_HEAD
    e = (lane_head[:, None] == jnp.arange(LANES)[None, :]).astype(BF16)
    return tri, sel, e, e.T


def rw_prep(u, prm):
    t = u.shape[0]
    tm = RW_TM
    nt = t // tm
    hb = tm // SUBLANES
    tri, sel, e, et = _rw_consts()
    full = lambda shp: pl.BlockSpec(shp, lambda i: (0,) * len(shp))
    in_specs = [
        pl.BlockSpec((tm, RW_COLS_PAD), lambda i: (i, 0)),
        pl.BlockSpec((SUBLANES, RW_COLS_PAD), lambda i: (jnp.maximum(i * hb - 1, 0), 0)),
        pl.BlockSpec((SUBLANES, RW_COLS_PAD), lambda i: (jnp.minimum((i + 1) * hb, nt * hb - 1), 0)),
        full((SUBLANES, RW_COLS_PAD)),
        full((2, RW_W)), full((2, RW_W)),
        full((2, LORA_ALL, RW_W)), full((2, LORA_ALL, RW_W)),
        full((LORA_G, RW_W)),
        full((1, RW_W)), full((1, RW_W)), full((1, RW_W)),
        full((RW_W, LANES)), full((LANES, RW_W)),
        full((2, tm, tm)), full((SUBLANES, tm)),
    ]
    tok = lambda dt: jax.ShapeDtypeStruct((t, RW_W), dt)
    tok2 = lambda dt: jax.ShapeDtypeStruct((2, t, RW_W), dt)
    spec1 = pl.BlockSpec((tm, RW_W), lambda i: (i, 0))
    spec2 = pl.BlockSpec((2, tm, RW_W), lambda i: (0, i, 0))
    chan2 = jax.ShapeDtypeStruct((2, RW_W, t), BF16)
    spec2t = pl.BlockSpec((2, RW_W, tm), lambda i: (0, 0, i))
    out_shape = [tok(BF16), tok(F32), tok(F32), tok2(BF16), tok2(BF16), chan2, chan2,
                 jax.ShapeDtypeStruct((2, nt * SUBLANES, RW_W), F32)]
    out_specs = [spec1, spec1, spec1, spec2, spec2, spec2t, spec2t,
                 pl.BlockSpec((2, SUBLANES, RW_W), lambda i: (0, i, 0))]
    return pl.pallas_call(
        _rw_prep_body,
        grid=(nt,),
        in_specs=in_specs,
        out_specs=out_specs,
        out_shape=out_shape,
        compiler_params=_cparams(1),
        name="rw_prep",
    )(u, u, u, prm["taps"], prm["w0"], prm["a0"], prm["w2"], prm["a2"], prm["g2"], prm["kks"], prm["ka"],
      prm["rk"], e, et, tri, sel)


RW_PAIRS = RW_HEADS // 2
SCAN_G = 8
SC_T = RW_TM


def _bmm(a, b):
    return lax.dot_general(a, b, (((2,), (1,)), ((0,), (0,))), preferred_element_type=F32)


def _rw_scan_body(rt_ref, kkt_ref, v_ref, kbt_ref, bbt_ref, gam_ref, h0_ref, y_ref, hfin_ref, h_scr, *, reverse):
    c = pl.program_id(1)
    nc = pl.num_programs(1)

    @pl.when(c == 0)
    def _():
        h_scr[...] = h0_ref[...]

    gn = SCAN_G
    n = SC_T
    row = lax.broadcasted_iota(jnp.int32, (n, n), 0)
    col = lax.broadcasted_iota(jnp.int32, (n, n), 1)
    same = (row // CHUNK) == (col // CHUNK)
    earlier = (col > row) if reverse else (col < row)
    strict = same & earlier
    incl = same & (earlier | (col == row))
    eye = (row == col).astype(F32)
    level_masks = []
    b = 1
    while b < CHUNK:
        level_masks.append(((row // (2 * b)) == (col // (2 * b))) & ((row // b) != (col // b)))
        b *= 2
    head_a = (col < RW_HEAD).astype(BF16)
    head_b = (col >= RW_HEAD).astype(BF16)

    lanes = lambda g: slice(g * n, (g + 1) * n)
    rt_p = [rt_ref[0, :, lanes(g)] for g in range(gn)]
    kk_p = [kkt_ref[0, :, lanes(g)] for g in range(gn)]
    v_p = [v_ref[:, lanes(g)] for g in range(gn)]
    kbt = kbt_ref[0].reshape(gn, n, n)
    bbt = bbt_ref[0].reshape(gn, n, n)

    lhs = jnp.stack([jnp.concatenate([rt_p[g] * head_a, kk_p[g] * head_a, rt_p[g] * head_b, kk_p[g] * head_b],
                                     axis=0) for g in range(gn)])
    ak = _bmm(lhs, kbt)
    ab = _bmm(lhs, bbt)
    per_head = lambda x, i: jnp.concatenate([x[:, i * n:(i + 1) * n], x[:, (i + 2) * n:(i + 3) * n]], axis=0)
    ark = jnp.where(incl, per_head(ak, 0), 0.0)
    lk = jnp.where(strict, per_head(ak, 1), 0.0)
    arb = jnp.where(incl, per_head(ab, 0), 0.0)
    lb = jnp.where(strict, per_head(ab, 1), 0.0)

    x = eye - jnp.where(level_masks[0], lb, 0.0)
    for mask in level_masks[1:]:
        xb = x.astype(BF16)
        off = jnp.where(mask, lb, 0.0).astype(BF16)
        x = x - _bmm(xb, _bmm(off, xb).astype(BF16))
    tb = x.astype(BF16)

    kkm = jnp.stack([kk_p[g] * head_a for g in range(gn)] + [kk_p[g] * head_b for g in range(gn)])
    vm = jnp.stack([v_p[g] * head_a for g in range(gn)] + [v_p[g] * head_b for g in range(gn)])
    wk = _bmm(tb, kkm)
    u0 = _bmm(tb, _bmm(lk.astype(BF16), vm).astype(BF16))
    arbb = arb.astype(BF16)
    qw = _bmm(arbb, wk.astype(BF16))
    y0 = _bmm(jnp.concatenate([ark.astype(BF16), -arbb], axis=2),
              jnp.concatenate([vm, u0.astype(BF16)], axis=1))
    pair = lambda t: t[:gn] + t[gn:]
    wk_p = pair(wk).astype(BF16)
    u0_p = pair(u0).astype(BF16)
    qhat = (jnp.stack(rt_p).astype(F32) - pair(qw)).astype(BF16)
    y0_p = pair(y0)
    vu = jnp.concatenate([jnp.stack(v_p), u0_p], axis=1)

    h = h_scr[...]
    chunks = range(n // CHUNK)
    for j in (reversed(chunks) if reverse else chunks):
        in_chunk = ((col // CHUNK) == j).astype(BF16)
        kj = kbt * in_chunk
        bj = bbt * in_chunk
        hloc = jnp.where(same, _bmm(jnp.concatenate([kj, -bj], axis=2), vu), 0.0)
        wb = jnp.where(same, _bmm(bj, wk_p), 0.0).astype(BF16)
        gamma = jnp.stack([jnp.broadcast_to(gam_ref[0, j:j + 1, lanes(g)], (n, n)).T for g in range(gn)])
        hb = h.astype(BF16)
        rows = slice(j * CHUNK, (j + 1) * CHUNK)
        yj = _bmm(qhat[:, rows], hb) + y0_p[:, rows]
        for g in range(gn):
            y_ref[rows, lanes(g)] = yj[g]
        h = gamma * (h - _bmm(wb, hb) + hloc)
    h_scr[...] = h

    @pl.when(c == nc - 1)
    def _():
        hfin_ref[...] = h


def rw_scan(rt, kkt, v, kbar_t, bbar_t, gam, h0, d):
    t = v.shape[0]
    nsc = t // SC_T
    reverse = d == 1
    gw = SCAN_G * LANES
    tix = (lambda c: nsc - 1 - c) if reverse else (lambda c: c)
    tm_spec = pl.BlockSpec((1, SC_T, gw), lambda p, c: (d, tix(c), p))
    cm_spec = pl.BlockSpec((1, gw, SC_T), lambda p, c: (d, p, tix(c)))
    st_spec = pl.BlockSpec((SCAN_G, LANES, LANES), lambda p, c: (p, 0, 0))
    return pl.pallas_call(
        functools.partial(_rw_scan_body, reverse=reverse),
        grid=(RW_PAIRS // SCAN_G, nsc),
        in_specs=[tm_spec, tm_spec, pl.BlockSpec((SC_T, gw), lambda p, c: (tix(c), p)), cm_spec, cm_spec,
                  pl.BlockSpec((1, SUBLANES, gw), lambda p, c: (d, tix(c), p)), st_spec],
        out_specs=[pl.BlockSpec((SC_T, gw), lambda p, c: (tix(c), p)), st_spec],
        out_shape=[jax.ShapeDtypeStruct((t, RW_W), F32),
                   jax.ShapeDtypeStruct((RW_PAIRS, LANES, LANES), F32)],
        scratch_shapes=[pltpu.VMEM((SCAN_G, LANES, LANES), F32)],
        compiler_params=_cparams(2),
        name="rw_scan_rev" if reverse else "rw_scan_fwd",
    )(rt, kkt, v, kbar_t, bbar_t, gam, h0)


def _rw_post_body(yf_ref, yb_ref, bv_ref, g_ref, lg_ref, lb_ref, e_ref, et_ref, o_ref):
    y = yf_ref[...] + yb_ref[...]
    mu = _seg_sum_bcast(y, e_ref, et_ref) * (1.0 / RW_HEAD)
    dlt = y - mu
    var = _seg_sum_bcast(dlt * dlt, e_ref, et_ref) * (1.0 / RW_HEAD)
    yn = dlt * lax.rsqrt(var + GN_EPS)
    o_ref[...] = ((yn * lg_ref[...] + lb_ref[...] + bv_ref[...]) * g_ref[...]).astype(o_ref.dtype)


def rw_post(yf, yb, bv, g, lnx_g, lnx_b):
    t = yf.shape[0]
    tm = 256
    _, _, e, et = _rw_consts()
    tokspec = pl.BlockSpec((tm, RW_W), lambda i: (i, 0))
    row = pl.BlockSpec((1, RW_W), lambda i: (0, 0))
    return pl.pallas_call(
        _rw_post_body,
        grid=(t // tm,),
        in_specs=[tokspec, tokspec, tokspec, tokspec, row, row,
                  pl.BlockSpec((RW_W, LANES), lambda i: (0, 0)), pl.BlockSpec((LANES, RW_W), lambda i: (0, 0))],
        out_specs=tokspec,
        out_shape=jax.ShapeDtypeStruct((t, RW_W), BF16),
        compiler_params=_cparams(1),
        name="rw_post",
    )(yf, yb, bv, g, lnx_g, lnx_b, e, et)


def rwkv_group(u, prm, h0_f, h0_b, with_output):
    v, g, bv, rt, kkt, kbar_t, bbar_t, gam = rw_prep(u, prm)
    outs = []
    finals = []
    for d, h0 in ((0, h0_f), (1, h0_b)):
        y, hfin = rw_scan(rt, kkt, v, kbar_t, bbar_t, gam, h0, d)
        outs.append(y)
        finals.append(hfin)
    if not with_output:
        return None, finals[0], finals[1]
    out = rw_post(outs[0], outs[1], bv, g, prm["lnx_g"], prm["lnx_b"])
    return out, finals[0], finals[1]


NA_RQ = 4


def _na_body(q_ref, k_ref, v_ref, kc_ref, vc_ref, bt_ref, o_ref, *, n_rows):
    rb = pl.program_id(1)
    scale = NA_HEAD ** -0.5
    kc = kc_ref[...]
    vc = vc_ref[...]
    for qi in range(NA_RQ):
        r = rb * NA_RQ + qi
        start = jnp.clip(r - NA_KH // 2, 0, n_rows - NA_KH)
        dr0 = start - r + NA_KH - 1
        tok0 = pl.multiple_of(start * GRID_W, GRID_W)
        q = q_ref[qi * GRID_W:(qi + 1) * GRID_W, :]
        kw = k_ref[pl.ds(tok0, NA_KH * GRID_W), :]
        vw = v_ref[pl.ds(tok0, NA_KH * GRID_W), :]
        s = _dot_nt(q, kw) * scale
        bias = jnp.concatenate([bt_ref[0, dr0 + 2 * m] for m in range(NA_KH // 2)], axis=1)
        s = s + bias
        sc = _dot_nt(q, kc) * scale
        mx = jnp.maximum(jnp.max(s, axis=-1, keepdims=True), jnp.max(sc, axis=-1, keepdims=True))
        p = jnp.exp(s - mx)
        pc = jnp.exp(sc - mx)
        den = jnp.sum(p, axis=-1, keepdims=True) + jnp.sum(pc, axis=-1, keepdims=True)
        o = (_dot(p.astype(BF16), vw) + _dot(pc.astype(BF16), vc)) / den
        o_ref[qi * GRID_W:(qi + 1) * GRID_W, :] = o.astype(o_ref.dtype)


def _na_bias_table(rpb):
    cols = jnp.arange(GRID_W)
    col_start = jnp.clip(cols - NA_KW // 2, 0, GRID_W - NA_KW)
    col_ok = (cols[None, :] >= col_start[:, None]) & (cols[None, :] < col_start[:, None] + NA_KW)
    dc = jnp.clip(cols[None, :] - cols[:, None] + NA_KW - 1, 0, 2 * NA_KW - 2)
    bt = rpb.astype(F32)[:, :, dc]
    bt = jnp.where(col_ok[None, None], bt, NEG_INF)
    return jnp.concatenate([bt[:, :-1], bt[:, 1:]], axis=-1)


def na_latent(att_l, att_c, rpb):
    s_len = att_l.shape[0]
    n_rows = s_len // GRID_W
    qb = NA_W // NA_HEAD
    bt = _na_bias_table(rpb)
    return pl.pallas_call(
        functools.partial(_na_body, n_rows=n_rows),
        grid=(NA_HEADS, n_rows // NA_RQ),
        in_specs=[
            pl.BlockSpec((NA_RQ * GRID_W, NA_HEAD), lambda h, r: (r, h)),
            pl.BlockSpec((s_len, NA_HEAD), lambda h, r: (0, qb + h)),
            pl.BlockSpec((s_len, NA_HEAD), lambda h, r: (0, 2 * qb + h)),
            pl.BlockSpec((CTX_LEN, NA_HEAD), lambda h, r: (0, qb + h)),
            pl.BlockSpec((CTX_LEN, NA_HEAD), lambda h, r: (0, 2 * qb + h)),
            pl.BlockSpec((1, 2 * NA_KH - 2, GRID_W, 2 * GRID_W), lambda h, r: (h, 0, 0, 0)),
        ],
        out_specs=pl.BlockSpec((NA_RQ * GRID_W, NA_HEAD), lambda h, r: (r, h)),
        out_shape=jax.ShapeDtypeStruct((s_len, NA_W), BF16),
        compiler_params=_cparams(2),
        name="na_latent",
    )(att_l, att_l, att_l, att_c, att_c, bt)


def _na_ctx_body(q_ref, k_ref, v_ref, o_ref):
    s = _dot_nt(q_ref[...], k_ref[...]) * (NA_HEAD ** -0.5)
    mx = jnp.max(s, axis=-1, keepdims=True)
    p = jnp.exp(s - mx)
    den = jnp.sum(p, axis=-1, keepdims=True)
    o_ref[...] = (_dot(p.astype(BF16), v_ref[...]) / den).astype(o_ref.dtype)


def na_ctx(att_c):
    qb = NA_W // NA_HEAD
    blk = lambda off: pl.BlockSpec((CTX_LEN, NA_HEAD), lambda h: (0, off + h))
    return pl.pallas_call(
        _na_ctx_body,
        grid=(NA_HEADS,),
        in_specs=[blk(0), blk(qb), blk(2 * qb)],
        out_specs=blk(0),
        out_shape=jax.ShapeDtypeStruct((CTX_LEN, NA_W), BF16),
        compiler_params=_cparams(1),
        name="na_ctx",
    )(att_c, att_c, att_c)


def _rope_body(x_ref, c_ref, s1_ref, s2_ref, o_ref):
    x = x_ref[...].astype(F32)
    quarter = SW_HEAD // 4
    x_up = pltpu.roll(x, LANES - quarter, 1)
    x_dn = pltpu.roll(x, quarter, 1)
    o_ref[...] = (x * c_ref[...] + x_up * s1_ref[...] + x_dn * s2_ref[...]).astype(o_ref.dtype)


def _rope_tables(s_len):
    t = jnp.arange(s_len)
    row = (t // GRID_W).astype(F32)
    col = (t % GRID_W).astype(F32)
    half = SW_HEAD // 2
    inv = ROPE_BASE ** (-jnp.arange(0, half, 2, dtype=F32) / half)
    ang_r = row[:, None] * inv[None, :]
    ang_c = col[:, None] * inv[None, :]
    cos = jnp.concatenate([jnp.cos(ang_r)] * 2 + [jnp.cos(ang_c)] * 2, axis=-1)
    sin_r, sin_c = jnp.sin(ang_r), jnp.sin(ang_c)
    zero = jnp.zeros_like(sin_r)
    s_up = jnp.concatenate([-sin_r, zero, -sin_c, zero], axis=-1)
    s_dn = jnp.concatenate([zero, sin_r, zero, sin_c], axis=-1)
    tile2 = lambda a: jnp.concatenate([a, a], axis=-1)
    return tile2(cos), tile2(s_up), tile2(s_dn)


def rope_qk(att_l):
    s_len = att_l.shape[0]
    tm = 512
    nblk = (SW_W + SW_KV * SW_HEAD) // LANES
    off = NA_COLS // LANES
    cos, s_up, s_dn = _rope_tables(s_len)
    tab = pl.BlockSpec((tm, LANES), lambda i, j: (i, 0))
    return pl.pallas_call(
        _rope_body,
        grid=(s_len // tm, nblk),
        in_specs=[pl.BlockSpec((tm, LANES), lambda i, j: (i, off + j)), tab, tab, tab],
        out_specs=pl.BlockSpec((tm, LANES), lambda i, j: (i, j)),
        out_shape=jax.ShapeDtypeStruct((s_len, nblk * LANES), BF16),
        compiler_params=_cparams(2),
        name="rope_qk",
    )(att_l, cos, s_up, s_dn)


def _sw_softmax_out(q8, s_parts, v_parts, sink_col):
    mx = sink_col
    for s in s_parts:
        mx = jnp.maximum(mx, jnp.max(s, axis=-1, keepdims=True))
    den = jnp.exp(sink_col - mx)
    o = None
    for s, v in zip(s_parts, v_parts):
        p = jnp.exp(s - mx)
        den = den + jnp.sum(p, axis=-1, keepdims=True)
        pv = _dot(p.astype(BF16), v)
        o = pv if o is None else o + pv
    return o / den


def _sink_column(sink_ref, kh, rows_per_head):
    n = SW_GROUP * rows_per_head
    grp = lax.broadcasted_iota(jnp.int32, (n, 1), 0) // rows_per_head
    col = jnp.zeros((n, 1), F32)
    for g in range(SW_GROUP):
        col = jnp.where(grp == g, sink_ref[kh * SW_GROUP + g], col)
    return col


def _sw_body(sink_ref, q_ref, k_ref, v_ref, kc_ref, vc_ref, o_ref, *, s_len):
    n = pl.program_id(0)
    scale = SW_HEAD ** -0.5
    win = 3 * SW_BLOCK
    start = pl.multiple_of(jnp.clip((n - 1) * SW_BLOCK, 0, s_len - win), SW_BLOCK)
    k2 = k_ref[pl.ds(start, win), :]
    v2 = v_ref[pl.ds(start, win), :]
    qpos = n * SW_BLOCK + lax.broadcasted_iota(jnp.int32, (SW_BLOCK, win), 0)
    kpos = start + lax.broadcasted_iota(jnp.int32, (SW_BLOCK, win), 1)
    ok = jnp.abs(kpos - qpos) <= SW_WIN
    for kh in range(SW_KV):
        lanes = slice(kh * SW_HEAD, (kh + 1) * SW_HEAD)
        q8 = jnp.concatenate(
            [q_ref[:, (kh * SW_GROUP + g) * SW_HEAD:(kh * SW_GROUP + g + 1) * SW_HEAD] for g in range(SW_GROUP)],
            axis=0)
        s_loc = _dot_nt(q8, k2[:, lanes]) * scale
        s_loc = jnp.where(ok[None], s_loc.reshape(SW_GROUP, SW_BLOCK, win), NEG_INF).reshape(
            SW_GROUP * SW_BLOCK, win)
        s_ctx = _dot_nt(q8, kc_ref[:, lanes]) * scale
        o = _sw_softmax_out(q8, [s_loc, s_ctx], [v2[:, lanes], vc_ref[:, lanes]],
                            _sink_column(sink_ref, kh, SW_BLOCK))
        for g in range(SW_GROUP):
            c0 = (kh * SW_GROUP + g) * SW_HEAD
            o_ref[:, c0:c0 + SW_HEAD] = o[g * SW_BLOCK:(g + 1) * SW_BLOCK].astype(o_ref.dtype)


def swa_latent(qk_rot, att_l, att_c, sink):
    s_len = qk_rot.shape[0]
    kcol = SW_W // LANES
    ck = (NA_COLS + SW_W) // LANES
    return pl.pallas_call(
        functools.partial(_sw_body, s_len=s_len),
        grid=(s_len // SW_BLOCK,),
        in_specs=[
            pl.BlockSpec(memory_space=pltpu.SMEM),
            pl.BlockSpec((SW_BLOCK, SW_W), lambda n: (n, 0)),
            pl.BlockSpec((s_len, LANES), lambda n: (0, kcol)),
            pl.BlockSpec((s_len, LANES), lambda n: (0, ck + 1)),
            pl.BlockSpec((CTX_LEN, LANES), lambda n: (0, ck)),
            pl.BlockSpec((CTX_LEN, LANES), lambda n: (0, ck + 1)),
        ],
        out_specs=pl.BlockSpec((SW_BLOCK, SW_W), lambda n: (n, 0)),
        out_shape=jax.ShapeDtypeStruct((s_len, SW_W), BF16),
        compiler_params=_cparams(1),
        name="swa_latent",
    )(sink, qk_rot, qk_rot, att_l, att_c, att_c)


def _sw_ctx_body(sink_ref, q_ref, k_ref, v_ref, o_ref):
    scale = SW_HEAD ** -0.5
    for kh in range(SW_KV):
        lanes = slice(kh * SW_HEAD, (kh + 1) * SW_HEAD)
        q8 = jnp.concatenate(
            [q_ref[:, (kh * SW_GROUP + g) * SW_HEAD:(kh * SW_GROUP + g + 1) * SW_HEAD] for g in range(SW_GROUP)],
            axis=0)
        s = _dot_nt(q8, k_ref[:, lanes]) * scale
        o = _sw_softmax_out(q8, [s], [v_ref[:, lanes]], _sink_column(sink_ref, kh, CTX_LEN))
        for g in range(SW_GROUP):
            c0 = (kh * SW_GROUP + g) * SW_HEAD
            o_ref[:, c0:c0 + SW_HEAD] = o[g * CTX_LEN:(g + 1) * CTX_LEN].astype(o_ref.dtype)


def swa_ctx(att_c, sink):
    qo = NA_COLS // SW_W
    ck = (NA_COLS + SW_W) // LANES
    return pl.pallas_call(
        _sw_ctx_body,
        grid=(1,),
        in_specs=[
            pl.BlockSpec(memory_space=pltpu.SMEM),
            pl.BlockSpec((CTX_LEN, SW_W), lambda n: (0, qo)),
            pl.BlockSpec((CTX_LEN, LANES), lambda n: (0, ck)),
            pl.BlockSpec((CTX_LEN, LANES), lambda n: (0, ck + 1)),
        ],
        out_specs=pl.BlockSpec((CTX_LEN, SW_W), lambda n: (0, 0)),
        out_shape=jax.ShapeDtypeStruct((CTX_LEN, SW_W), BF16),
        compiler_params=_cparams(1),
        name="swa_ctx",
    )(sink, att_c, att_c, att_c)


def _router_body(x_ref, g_ref, sc_ref, sh_ref, wr_ref, rb_ref, h_ref, gate_ref):
    x = x_ref[...]
    tm = x.shape[0]
    ms = jnp.mean(x * x, axis=-1, keepdims=True)
    h = (x * lax.rsqrt(ms + NORM_EPS) * g_ref[...]) * (1.0 + sc_ref[...]) + sh_ref[...]
    h_ref[...] = h.astype(h_ref.dtype)

    hh, hl = _split2(h)
    wh, wl = _split2(wr_ref[...])
    logits = _dot(hh, wh) + _dot(hl, wh) + _dot(hh, wl)
    lane = lax.broadcasted_iota(jnp.int32, (tm, LANES), 1)
    lane_f = lane.astype(F32)
    valid = lane < N_EXPERTS
    neg = -jnp.inf
    big = float(2 * LANES)
    scores = _sigmoid(logits)
    biased = jnp.where(valid, scores + rb_ref[...], neg)
    grp = lane // EXPERTS_PER_GROUP

    def first_argmax(vals):
        m = jnp.max(vals, axis=-1, keepdims=True)
        idx = jnp.min(jnp.where(vals == m, lane_f, big), axis=-1, keepdims=True)
        return m, idx

    gscore = jnp.full((tm, LANES), neg, F32)
    for gi in range(N_GROUPS):
        vg = jnp.where(grp == gi, biased, neg)
        m1, i1 = first_argmax(vg)
        m2 = jnp.max(jnp.where(lane_f == i1, neg, vg), axis=-1, keepdims=True)
        gscore = jnp.where(lane == gi * EXPERTS_PER_GROUP, m1 + m2, gscore)
    keep = jnp.zeros((tm, LANES), jnp.bool_)
    for _ in range(TOPK_GROUPS):
        _, gi = first_argmax(gscore)
        keep = keep | (grp.astype(F32) == jnp.floor(gi * (1.0 / EXPERTS_PER_GROUP)))
        gscore = jnp.where(lane_f == gi, neg, gscore)
    cur = jnp.where(valid, jnp.where(keep, biased, NEG_INF), neg)
    sel = jnp.zeros((tm, LANES), jnp.bool_)
    for _ in range(TOP_K):
        _, ei = first_argmax(cur)
        hit = lane_f == ei
        sel = sel | hit
        cur = jnp.where(hit, neg, cur)
    w = jnp.where(sel, scores, 0.0)
    gates = ROUTE_SCALE * w / jnp.sum(w, axis=-1, keepdims=True)
    gate_ref[...] = jnp.where(lane == N_EXPERTS, 1.0, gates)


def router(x, g, sc, sh, w_router_pad, r_bias_pad):
    t = x.shape[0]
    tm = 256
    row = pl.BlockSpec((1, D_MODEL), lambda i: (0, 0))
    return pl.pallas_call(
        _router_body,
        grid=(t // tm,),
        in_specs=[pl.BlockSpec((tm, D_MODEL), lambda i: (i, 0)), row, row, row,
                  pl.BlockSpec((D_MODEL, LANES), lambda i: (0, 0)), pl.BlockSpec((1, LANES), lambda i: (0, 0))],
        out_specs=[pl.BlockSpec((tm, D_MODEL), lambda i: (i, 0)), pl.BlockSpec((tm, LANES), lambda i: (i, 0))],
        out_shape=[jax.ShapeDtypeStruct((t, D_MODEL), BF16), jax.ShapeDtypeStruct((t, LANES), F32)],
        compiler_params=_cparams(1),
        name="router",
    )(x, g, sc, sh, w_router_pad, r_bias_pad)


MOE_DOWN_COLS = 1024


def _moe_body(h_ref, gate_ref, wg_ref, wu_ref, wd_ref, x_ref, ga_ref, o_ref):
    e = pl.program_id(1)
    ne = pl.num_programs(1)

    @pl.when(e == 0)
    def _():
        o_ref[...] = jnp.zeros_like(o_ref)

    h = h_ref[...]
    hg = _dot(h, wg_ref[0])
    hu = _dot(h, wu_ref[0])
    lane = lax.broadcasted_iota(jnp.int32, gate_ref.shape, 1)
    gate = jnp.sum(jnp.where(lane == e, gate_ref[...], 0.0), axis=-1, keepdims=True)
    act = ((hg * _sigmoid(hg)) * hu * gate).astype(BF16)
    for c0 in range(0, D_MODEL, MOE_DOWN_COLS):
        cols = slice(c0, c0 + MOE_DOWN_COLS)
        o_ref[:, cols] += _dot(act, wd_ref[0, :, cols])

    @pl.when(e == ne - 1)
    def _():
        o_ref[...] = x_ref[...] + ga_ref[...] * o_ref[...]


def moe_experts(h, gates, wg, wu, wd, x, ga):
    t = h.shape[0]
    bm = 512 if t % 512 == 0 else 256
    ne = wg.shape[0]
    tok = pl.BlockSpec((bm, D_MODEL), lambda i, e: (i, 0))
    return pl.pallas_call(
        _moe_body,
        grid=(t // bm, ne),
        in_specs=[tok, pl.BlockSpec((bm, LANES), lambda i, e: (i, 0)),
                  pl.BlockSpec((1, D_MODEL, D_EXPERT), lambda i, e: (e, 0, 0)),
                  pl.BlockSpec((1, D_MODEL, D_EXPERT), lambda i, e: (e, 0, 0)),
                  pl.BlockSpec((1, D_EXPERT, D_MODEL), lambda i, e: (e, 0, 0)),
                  pl.BlockSpec((bm, D_MODEL), lambda i, e: (i, 0), pipeline_mode=pl.Buffered(1)),
                  pl.BlockSpec((1, D_MODEL), lambda i, e: (0, 0))],
        out_specs=tok,
        out_shape=jax.ShapeDtypeStruct((t, D_MODEL), F32),
        compiler_params=_cparams(2),
        name="moe_experts",
    )(h, gates, wg, wu, wd, x, ga)


def _pad_cols(a, n):
    return jnp.pad(a, [(0, 0)] * (a.ndim - 1) + [(0, n - a.shape[-1])])


def _rw_params(i, rw_conv, rw_w0, rw_w2, rw_a0, rw_a2, rw_g2, rw_kk, rw_ka, rw_rk, rw_lnx_g, rw_lnx_b):
    w2 = jnp.zeros((2, LORA_ALL, RW_W), F32)
    a2 = jnp.zeros((2, LORA_ALL, RW_W), F32)
    for d in range(2):
        w2 = w2.at[d, d * LORA_W:(d + 1) * LORA_W].set(rw_w2[i, d])
        a2 = a2.at[d, 2 * LORA_W + d * LORA_A:2 * LORA_W + (d + 1) * LORA_A].set(rw_a2[i, d])
    taps = jnp.pad(rw_conv[i], ((0, SUBLANES - 3), (0, RW_COLS_PAD - RW_COLS)))
    return dict(
        taps=taps, w0=rw_w0[i], a0=rw_a0[i], w2=w2.astype(BF16), a2=a2.astype(BF16), g2=rw_g2[i].astype(BF16),
        kks=rw_kk[i][None], ka=rw_ka[i][None], rk=rw_rk[i].reshape(1, RW_W),
        lnx_g=rw_lnx_g[i][None], lnx_b=rw_lnx_b[i][None])


def kernel(x, c, ctx, c_ctx, w_ada, b_ada, norm1_g, norm2_g, w_in, rw_conv, rw_w0, rw_w2, rw_a0, rw_a2, rw_g2,
           rw_kk, rw_ka, rw_rk, rw_lnx_g, rw_lnx_b, na_rpb, sw_sink, w_out, w_router, router_bias, we_gate,
           we_up, we_down, ws_gate, ws_up, ws_down, final_g):
    assert x.shape[0] == 1 and x.shape[2] == D_MODEL and ctx.shape[1] == CTX_LEN
    xl = x[0]
    xc = ctx[0]
    cond8 = jnp.zeros((SUBLANES, D_MODEL), F32).at[0].set(c[0]).at[1].set(c_ctx)
    mods = ada_mod(cond8, w_ada, b_ada)

    for i in range(DEPTH):
        ctx_needed = i < DEPTH - 1
        mod_l = [m[None] for m in mods[i, 0].reshape(6, D_MODEL)]
        mod_c = [m[None] for m in mods[i, 1].reshape(6, D_MODEL)]
        w_rw = _pad_cols(w_in[i][:, :RW_COLS], RW_COLS_PAD).astype(BF16)
        w_att = w_in[i][:, RW_COLS:].astype(BF16)
        w_o = w_out[i].astype(BF16)
        w_o_parts = [w_o[:RW_W], w_o[RW_W:RW_W + NA_W], w_o[RW_W + NA_W:]]
        rw_prm = _rw_params(i, rw_conv, rw_w0, rw_w2, rw_a0, rw_a2, rw_g2, rw_kk, rw_ka, rw_rk, rw_lnx_g,
                            rw_lnx_b)
        n1 = norm1_g[i][None]
        n2 = norm2_g[i][None]

        hl = norm_mod(xl, n1, mod_l[1], mod_l[0])
        hc = norm_mod(xc, n1, mod_c[1], mod_c[0])
        url = matmul([hl], [w_rw], 768, F32, name="in_proj_rw")
        urc = matmul([hc], [w_rw], 768, F32, name="in_proj_rw_ctx")
        ual = matmul([hl], [w_att], 256, BF16, name="in_proj_att")
        uac = matmul([hc], [w_att], 256, BF16, name="in_proj_att_ctx")

        z = jnp.zeros((RW_PAIRS, LANES, LANES), F32)
        rwc, s_f, s_b = rwkv_group(urc, rw_prm, z, z, ctx_needed)
        rwl, _, _ = rwkv_group(url, rw_prm, s_f, s_b, True)
        nal = na_latent(ual, uac, na_rpb[i])
        swl = swa_latent(rope_qk(ual), ual, uac, sw_sink[i])
        xl = matmul([rwl, nal, swl], w_o_parts, 512, F32, residual=(xl, mod_l[2]), name="out_proj")

        wr = _pad_cols(w_router[i], LANES)
        rb = _pad_cols(router_bias[i][None], LANES)
        wg = jnp.concatenate([we_gate[i], ws_gate[i][None]], axis=0).astype(BF16)
        wu = jnp.concatenate([we_up[i], ws_up[i][None]], axis=0).astype(BF16)
        wd = jnp.concatenate([we_down[i], ws_down[i][None]], axis=0).astype(BF16)
        h2, gates = router(xl, n2, mod_l[4], mod_l[3], wr, rb)
        xl = moe_experts(h2, gates, wg, wu, wd, xl, mod_l[5])
        if ctx_needed:
            nac = na_ctx(uac)
            swc = swa_ctx(uac, sw_sink[i])
            xc = matmul([rwc, nac, swc], w_o_parts, 512, F32, residual=(xc, mod_c[2]), name="out_proj_ctx")
            h2c, gates_c = router(xc, n2, mod_c[4], mod_c[3], wr, rb)
            xc = moe_experts(h2c, gates_c, wg, wu, wd, xc, mod_c[5])
    return final_norm(xl, final_g[None])[None]
```

```python
import functools

import jax
import jax.numpy as jnp
from jax import lax
from jax.experimental import pallas as pl
from jax.experimental.pallas import tpu as pltpu

F32 = jnp.float32
BF16 = jnp.bfloat16

D_MODEL = 4096
DEPTH = 2
GRID_W = 64
CTX_LEN = 256
NORM_EPS = 1e-6
NEG_INF = -1e30

RW_HEAD = 64
RW_W = D_MODEL // 2
RW_HEADS = RW_W // RW_HEAD
LORA_W = 96
LORA_A = 96
LORA_G = 256
GN_EPS = 64e-5
RW_COLS = 3 * RW_W + LORA_G + 2 * LORA_W + 2 * LORA_A
LORA_OFF = 3 * RW_W + LORA_G
LORA_ALL = 2 * LORA_W + 2 * LORA_A

NA_HEAD = 128
NA_W = D_MODEL // 4
NA_HEADS = NA_W // NA_HEAD
NA_KH = 8
NA_KW = 16
NA_COLS = 3 * NA_W

SW_HEAD = 64
SW_W = D_MODEL - RW_W - NA_W
SW_HEADS = SW_W // SW_HEAD
SW_KV = SW_HEADS // 8
SW_GROUP = SW_HEADS // SW_KV
SW_WIN = 128
SW_BLOCK = 128
ROPE_BASE = 10000.0
SW_COLS = SW_W + 2 * SW_KV * SW_HEAD
ATT_COLS = NA_COLS + SW_COLS

N_EXPERTS = 64
N_GROUPS = 8
EXPERTS_PER_GROUP = N_EXPERTS // N_GROUPS
TOPK_GROUPS = 4
TOP_K = 8
D_EXPERT = 256
ROUTE_SCALE = 2.5

LANES = 128
SUBLANES = 8
RW_COLS_PAD = 6912
CHUNK = 64
VMEM_LIMIT = 56 * 1024 * 1024


def _cparams(n_axes, vmem=VMEM_LIMIT):
    return pltpu.CompilerParams(dimension_semantics=("arbitrary",) * n_axes, vmem_limit_bytes=vmem)


def _dot(a, b):
    return jnp.dot(a, b, preferred_element_type=F32)


def _dot_nt(a, b):
    return lax.dot_general(a, b, (((1,), (1,)), ((), ())), preferred_element_type=F32)


def _split2(x):
    hi = x.astype(BF16)
    lo = (x - hi.astype(F32)).astype(BF16)
    return hi, lo


def _split3(x):
    hi = x.astype(BF16)
    r1 = x - hi.astype(F32)
    mid = r1.astype(BF16)
    lo = (r1 - mid.astype(F32)).astype(BF16)
    return hi, mid, lo


def _sigmoid(x):
    return 1.0 / (1.0 + jnp.exp(-x))


def _ada_body(s_ref, w_ref, b_ref, o_ref):
    c = s_ref[...]
    s = (c * _sigmoid(c)).astype(BF16)
    o_ref[0] = _dot(s, w_ref[0].astype(BF16)) + b_ref[0]


def ada_mod(cond8, w_ada, b_ada):
    bn = 512
    n = 6 * D_MODEL
    return pl.pallas_call(
        _ada_body,
        grid=(DEPTH, n // bn),
        in_specs=[
            pl.BlockSpec((SUBLANES, D_MODEL), lambda l, j: (0, 0)),
            pl.BlockSpec((1, D_MODEL, bn), lambda l, j: (l, 0, j)),
            pl.BlockSpec((1, 1, bn), lambda l, j: (l, 0, j)),
        ],
        out_specs=pl.BlockSpec((1, SUBLANES, bn), lambda l, j: (l, 0, j)),
        out_shape=jax.ShapeDtypeStruct((DEPTH, SUBLANES, n), F32),
        compiler_params=_cparams(2),
        name="ada_mod",
    )(cond8, w_ada, b_ada.reshape(DEPTH, 1, n))


def _norm_mod_body(x_ref, g_ref, sc_ref, sh_ref, o_ref):
    x = x_ref[...]
    ms = jnp.mean(x * x, axis=-1, keepdims=True)
    y = x * lax.rsqrt(ms + NORM_EPS) * g_ref[...]
    o_ref[...] = (y * (1.0 + sc_ref[...]) + sh_ref[...]).astype(o_ref.dtype)


def norm_mod(x, g, sc, sh, out_dtype=BF16):
    t = x.shape[0]
    tm = 256
    row = pl.BlockSpec((1, D_MODEL), lambda i: (0, 0))
    return pl.pallas_call(
        _norm_mod_body,
        grid=(t // tm,),
        in_specs=[pl.BlockSpec((tm, D_MODEL), lambda i: (i, 0)), row, row, row],
        out_specs=pl.BlockSpec((tm, D_MODEL), lambda i: (i, 0)),
        out_shape=jax.ShapeDtypeStruct((t, D_MODEL), out_dtype),
        compiler_params=_cparams(1),
        name="norm_mod",
    )(x, g, sc, sh)


def _final_norm_body(x_ref, g_ref, o_ref):
    x = x_ref[...]
    ms = jnp.mean(x * x, axis=-1, keepdims=True)
    o_ref[...] = x * lax.rsqrt(ms + NORM_EPS) * g_ref[...]


def final_norm(x, g):
    t = x.shape[0]
    tm = 256
    return pl.pallas_call(
        _final_norm_body,
        grid=(t // tm,),
        in_specs=[pl.BlockSpec((tm, D_MODEL), lambda i: (i, 0)), pl.BlockSpec((1, D_MODEL), lambda i: (0, 0))],
        out_specs=pl.BlockSpec((tm, D_MODEL), lambda i: (i, 0)),
        out_shape=jax.ShapeDtypeStruct((t, D_MODEL), F32),
        compiler_params=_cparams(1),
        name="final_norm",
    )(x, g)


def _mm_body(*refs, n_in, residual):
    a_refs = refs[:n_in]
    w_refs = refs[n_in:2 * n_in]
    rest = refs[2 * n_in:]
    acc = _dot(a_refs[0][...], w_refs[0][...])
    for a_ref, w_ref in zip(a_refs[1:], w_refs[1:]):
        acc = acc + _dot(a_ref[...], w_ref[...])
    if residual:
        x_ref, ga_ref, o_ref = rest
        o_ref[...] = x_ref[...] + ga_ref[...] * acc
    else:
        (o_ref,) = rest
        o_ref[...] = acc.astype(o_ref.dtype)


def matmul(a_list, w_list, bn, out_dtype, residual=None, name="matmul"):
    m = a_list[0].shape[0]
    n = w_list[0].shape[1]
    bm = 1024 if m % 1024 == 0 else 256
    in_specs = [pl.BlockSpec((bm, a.shape[1]), lambda i, j: (i, 0)) for a in a_list]
    in_specs += [pl.BlockSpec((w.shape[0], bn), lambda i, j: (0, j)) for w in w_list]
    args = list(a_list) + list(w_list)
    if residual is not None:
        in_specs += [pl.BlockSpec((bm, bn), lambda i, j: (i, j)), pl.BlockSpec((1, bn), lambda i, j: (0, j))]
        args += list(residual)
    return pl.pallas_call(
        functools.partial(_mm_body, n_in=len(a_list), residual=residual is not None),
        grid=(m // bm, n // bn),
        in_specs=in_specs,
        out_specs=pl.BlockSpec((bm, bn), lambda i, j: (i, j)),
        out_shape=jax.ShapeDtypeStruct((m, n), out_dtype),
        compiler_params=_cparams(2),
        name=name,
    )(*args)


RW_TM = 128


def _seg_sum_bcast(x, e_ref, et_ref):
    hi, lo = _split2(x)
    s = _dot(hi, e_ref[...]) + _dot(lo, e_ref[...])
    shi, slo = _split2(s)
    return _dot(shi, et_ref[...]) + _dot(slo, et_ref[...])


def _rw_prep_body(u_ref, up_ref, un_ref, taps_ref, w0_ref, a0_ref, w2_ref, a2_ref, g2_ref, kks_ref, ka_ref,
                  rk_ref, e_ref, et_ref, tri_ref, sel_ref,
                  v_ref, g_ref, bv_ref, rt_ref, kkt_ref, kbar_ref, bbar_ref, gam_ref):
    i = pl.program_id(0)
    nt = pl.num_programs(0)
    tm = RW_TM
    rows = lax.broadcasted_iota(jnp.int32, (tm, 1), 0)
    has_prev = (i > 0).astype(F32)
    has_next = (i < nt - 1).astype(F32)

    def conv(c0, c1):
        u = u_ref[:, c0:c1]
        prev_row = up_ref[SUBLANES - 1:SUBLANES, c0:c1] * has_prev
        next_row = un_ref[0:1, c0:c1] * has_next
        u_prev = jnp.where(rows == 0, prev_row, pltpu.roll(u, 1, 0))
        u_next = jnp.where(rows == tm - 1, next_row, pltpu.roll(u, tm - 1, 0))
        return (u_prev * taps_ref[0:1, c0:c1] + u * taps_ref[1:2, c0:c1] + u_next * taps_ref[2:3, c0:c1])

    r = conv(0, RW_W)
    k = conv(RW_W, 2 * RW_W)
    v = conv(2 * RW_W, 3 * RW_W)
    gd = conv(3 * RW_W, LORA_OFF)
    lo_in = conv(LORA_OFF, LORA_OFF + LORA_ALL)

    v_ref[...] = v.astype(v_ref.dtype)
    g_ref[...] = _dot(_sigmoid(gd).astype(BF16), g2_ref[...])

    kk = k * kks_ref[...]
    ssq = _seg_sum_bcast(kk * kk, e_ref, et_ref)
    kkn = kk / jnp.maximum(jnp.sqrt(ssq), 1e-12)

    tanh_lo = jnp.tanh(lo_in).astype(BF16)
    raw_lo = lo_in.astype(BF16)
    bonus = jnp.zeros((tm, RW_W), F32)
    for d in range(2):
        w_pre = w0_ref[d:d + 1, :] + _dot(tanh_lo, w2_ref[d])
        z = -w_pre
        softplus = jnp.maximum(z, 0.0) + jnp.log(1.0 + jnp.exp(-jnp.abs(z)))
        w_log = -softplus - 0.5
        lw = -jnp.exp(w_log)
        a = _sigmoid(a0_ref[d:d + 1, :] + _dot(raw_lo, a2_ref[d]))
        kd = k * (1.0 + (a - 1.0) * ka_ref[...])
        b = kkn * a
        bonus = bonus + r * kd * rk_ref[...]
        l1, l2, l3 = _split3(lw)
        tri = tri_ref[d]
        sel = sel_ref[...]
        cum = _dot(tri, l1) + _dot(tri, l2) + _dot(tri, l3)
        tot = _dot(sel, l1) + _dot(sel, l2) + _dot(sel, l3)
        e_inv = jnp.exp(-cum)
        rt_ref[d] = (r * jnp.exp(cum)).astype(BF16)
        kkt_ref[d] = (kkn * jnp.exp(cum - lw)).astype(BF16)
        kbar_ref[d] = (kd * e_inv).T.astype(BF16)
        bbar_ref[d] = (b * e_inv).T.astype(BF16)
        gam_ref[d] = jnp.exp(tot)
    bv_ref[...] = _seg_sum_bcast(bonus, e_ref, et_ref) * v


def _rw_consts():
    tm = RW_TM
    t = jnp.arange(tm)
    same = (t[:, None] // CHUNK) == (t[None, :] // CHUNK)
    tri_f = same & (t[None, :] <= t[:, None])
    tri_b = same & (t[None, :] >= t[:, None])
    tri = jnp.stack([tri_f, tri_b]).astype(BF16)
    sel = ((t[None, :] // CHUNK) == jnp.arange(SUBLANES)[:, None]).astype(BF16)
    lane_head = jnp.arange(RW_W) // RW_HEAD
    e = (lane_head[:, None] == jnp.arange(LANES)[None, :]).astype(BF16)
    return tri, sel, e, e.T


def rw_prep(u, prm):
    t = u.shape[0]
    tm = RW_TM
    nt = t // tm
    hb = tm // SUBLANES
    tri, sel, e, et = _rw_consts()
    full = lambda shp: pl.BlockSpec(shp, lambda i: (0,) * len(shp))
    in_specs = [
        pl.BlockSpec((tm, RW_COLS_PAD), lambda i: (i, 0)),
        pl.BlockSpec((SUBLANES, RW_COLS_PAD), lambda i: (jnp.maximum(i * hb - 1, 0), 0)),
        pl.BlockSpec((SUBLANES, RW_COLS_PAD), lambda i: (jnp.minimum((i + 1) * hb, nt * hb - 1), 0)),
        full((SUBLANES, RW_COLS_PAD)),
        full((2, RW_W)), full((2, RW_W)),
        full((2, LORA_ALL, RW_W)), full((2, LORA_ALL, RW_W)),
        full((LORA_G, RW_W)),
        full((1, RW_W)), full((1, RW_W)), full((1, RW_W)),
        full((RW_W, LANES)), full((LANES, RW_W)),
        full((2, tm, tm)), full((SUBLANES, tm)),
    ]
    tok = lambda dt: jax.ShapeDtypeStruct((t, RW_W), dt)
    tok2 = lambda dt: jax.ShapeDtypeStruct((2, t, RW_W), dt)
    spec1 = pl.BlockSpec((tm, RW_W), lambda i: (i, 0))
    spec2 = pl.BlockSpec((2, tm, RW_W), lambda i: (0, i, 0))
    chan2 = jax.ShapeDtypeStruct((2, RW_W, t), BF16)
    spec2t = pl.BlockSpec((2, RW_W, tm), lambda i: (0, 0, i))
    out_shape = [tok(BF16), tok(F32), tok(F32), tok2(BF16), tok2(BF16), chan2, chan2,
                 jax.ShapeDtypeStruct((2, nt * SUBLANES, RW_W), F32)]
    out_specs = [spec1, spec1, spec1, spec2, spec2, spec2t, spec2t,
                 pl.BlockSpec((2, SUBLANES, RW_W), lambda i: (0, i, 0))]
    return pl.pallas_call(
        _rw_prep_body,
        grid=(nt,),
        in_specs=in_specs,
        out_specs=out_specs,
        out_shape=out_shape,
        compiler_params=_cparams(1),
        name="rw_prep",
    )(u, u, u, prm["taps"], prm["w0"], prm["a0"], prm["w2"], prm["a2"], prm["g2"], prm["kks"], prm["ka"],
      prm["rk"], e, et, tri, sel)


RW_PAIRS = RW_HEADS // 2
SCAN_G = 8
SC_T = RW_TM


def _bmm(a, b):
    return lax.dot_general(a, b, (((2,), (1,)), ((0,), (0,))), preferred_element_type=F32)


def _rw_scan_body(rt_ref, kkt_ref, v_ref, kbt_ref, bbt_ref, gam_ref, h0_ref, y_ref, hfin_ref, h_scr, *, reverse):
    c = pl.program_id(1)
    nc = pl.num_programs(1)

    @pl.when(c == 0)
    def _():
        h_scr[...] = h0_ref[...]

    gn = SCAN_G
    n = SC_T
    row = lax.broadcasted_iota(jnp.int32, (n, n), 0)
    col = lax.broadcasted_iota(jnp.int32, (n, n), 1)
    same = (row // CHUNK) == (col // CHUNK)
    earlier = (col > row) if reverse else (col < row)
    strict = same & earlier
    incl = same & (earlier | (col == row))
    eye = (row == col).astype(F32)
    level_masks = []
    b = 1
    while b < CHUNK:
        level_masks.append(((row // (2 * b)) == (col // (2 * b))) & ((row // b) != (col // b)))
        b *= 2
    head_a = (col < RW_HEAD).astype(BF16)
    head_b = (col >= RW_HEAD).astype(BF16)

    lanes = lambda g: slice(g * n, (g + 1) * n)
    rt_p = [rt_ref[0, :, lanes(g)] for g in range(gn)]
    kk_p = [kkt_ref[0, :, lanes(g)] for g in range(gn)]
    v_p = [v_ref[:, lanes(g)] for g in range(gn)]
    kbt = kbt_ref[0].reshape(gn, n, n)
    bbt = bbt_ref[0].reshape(gn, n, n)

    lhs = jnp.stack([jnp.concatenate([rt_p[g] * head_a, kk_p[g] * head_a, rt_p[g] * head_b, kk_p[g] * head_b],
                                     axis=0) for g in range(gn)])
    ak = _bmm(lhs, kbt)
    ab = _bmm(lhs, bbt)
    per_head = lambda x, i: jnp.concatenate([x[:, i * n:(i + 1) * n], x[:, (i + 2) * n:(i + 3) * n]], axis=0)
    ark = jnp.where(incl, per_head(ak, 0), 0.0)
    lk = jnp.where(strict, per_head(ak, 1), 0.0)
    arb = jnp.where(incl, per_head(ab, 0), 0.0)
    lb = jnp.where(strict, per_head(ab, 1), 0.0)

    x = eye - jnp.where(level_masks[0], lb, 0.0)
    for mask in level_masks[1:]:
        xb = x.astype(BF16)
        off = jnp.where(mask, lb, 0.0).astype(BF16)
        x = x - _bmm(xb, _bmm(off, xb).astype(BF16))
    tb = x.astype(BF16)

    kkm = jnp.stack([kk_p[g] * head_a for g in range(gn)] + [kk_p[g] * head_b for g in range(gn)])
    vm = jnp.stack([v_p[g] * head_a for g in range(gn)] + [v_p[g] * head_b for g in range(gn)])
    wk = _bmm(tb, kkm)
    u0 = _bmm(tb, _bmm(lk.astype(BF16), vm).astype(BF16))
    arbb = arb.astype(BF16)
    qw = _bmm(arbb, wk.astype(BF16))
    y0 = _bmm(jnp.concatenate([ark.astype(BF16), -arbb], axis=2),
              jnp.concatenate([vm, u0.astype(BF16)], axis=1))
    pair = lambda t: t[:gn] + t[gn:]
    wk_p = pair(wk).astype(BF16)
    u0_p = pair(u0).astype(BF16)
    qhat = (jnp.stack(rt_p).astype(F32) - pair(qw)).astype(BF16)
    y0_p = pair(y0)
    vu = jnp.concatenate([jnp.stack(v_p), u0_p], axis=1)

    h = h_scr[...]
    chunks = range(n // CHUNK)
    for j in (reversed(chunks) if reverse else chunks):
        in_chunk = ((col // CHUNK) == j).astype(BF16)
        kj = kbt * in_chunk
        bj = bbt * in_chunk
        hloc = jnp.where(same, _bmm(jnp.concatenate([kj, -bj], axis=2), vu), 0.0)
        wb = jnp.where(same, _bmm(bj, wk_p), 0.0).astype(BF16)
        gamma = jnp.stack([jnp.broadcast_to(gam_ref[0, j:j + 1, lanes(g)], (n, n)).T for g in range(gn)])
        hb = h.astype(BF16)
        rows = slice(j * CHUNK, (j + 1) * CHUNK)
        yj = _bmm(qhat[:, rows], hb) + y0_p[:, rows]
        for g in range(gn):
            y_ref[rows, lanes(g)] = yj[g]
        h = gamma * (h - _bmm(wb, hb) + hloc)
    h_scr[...] = h

    @pl.when(c == nc - 1)
    def _():
        hfin_ref[...] = h


def rw_scan(rt, kkt, v, kbar_t, bbar_t, gam, h0, d):
    t = v.shape[0]
    nsc = t // SC_T
    reverse = d == 1
    gw = SCAN_G * LANES
    tix = (lambda c: nsc - 1 - c) if reverse else (lambda c: c)
    tm_spec = pl.BlockSpec((1, SC_T, gw), lambda p, c: (d, tix(c), p))
    cm_spec = pl.BlockSpec((1, gw, SC_T), lambda p, c: (d, p, tix(c)))
    st_spec = pl.BlockSpec((SCAN_G, LANES, LANES), lambda p, c: (p, 0, 0))
    return pl.pallas_call(
        functools.partial(_rw_scan_body, reverse=reverse),
        grid=(RW_PAIRS // SCAN_G, nsc),
        in_specs=[tm_spec, tm_spec, pl.BlockSpec((SC_T, gw), lambda p, c: (tix(c), p)), cm_spec, cm_spec,
                  pl.BlockSpec((1, SUBLANES, gw), lambda p, c: (d, tix(c), p)), st_spec],
        out_specs=[pl.BlockSpec((SC_T, gw), lambda p, c: (tix(c), p)), st_spec],
        out_shape=[jax.ShapeDtypeStruct((t, RW_W), F32),
                   jax.ShapeDtypeStruct((RW_PAIRS, LANES, LANES), F32)],
        scratch_shapes=[pltpu.VMEM((SCAN_G, LANES, LANES), F32)],
        compiler_params=_cparams(2),
        name="rw_scan_rev" if reverse else "rw_scan_fwd",
    )(rt, kkt, v, kbar_t, bbar_t, gam, h0)


def _rw_post_body(yf_ref, yb_ref, bv_ref, g_ref, lg_ref, lb_ref, e_ref, et_ref, o_ref):
    y = yf_ref[...] + yb_ref[...]
    mu = _seg_sum_bcast(y, e_ref, et_ref) * (1.0 / RW_HEAD)
    dlt = y - mu
    var = _seg_sum_bcast(dlt * dlt, e_ref, et_ref) * (1.0 / RW_HEAD)
    yn = dlt * lax.rsqrt(var + GN_EPS)
    o_ref[...] = ((yn * lg_ref[...] + lb_ref[...] + bv_ref[...]) * g_ref[...]).astype(o_ref.dtype)


def rw_post(yf, yb, bv, g, lnx_g, lnx_b):
    t = yf.shape[0]
    tm = 256
    _, _, e, et = _rw_consts()
    tokspec = pl.BlockSpec((tm, RW_W), lambda i: (i, 0))
    row = pl.BlockSpec((1, RW_W), lambda i: (0, 0))
    return pl.pallas_call(
        _rw_post_body,
        grid=(t // tm,),
        in_specs=[tokspec, tokspec, tokspec, tokspec, row, row,
                  pl.BlockSpec((RW_W, LANES), lambda i: (0, 0)), pl.BlockSpec((LANES, RW_W), lambda i: (0, 0))],
        out_specs=tokspec,
        out_shape=jax.ShapeDtypeStruct((t, RW_W), BF16),
        compiler_params=_cparams(1),
        name="rw_post",
    )(yf, yb, bv, g, lnx_g, lnx_b, e, et)


def rwkv_group(u, prm, h0_f, h0_b, with_output):
    v, g, bv, rt, kkt, kbar_t, bbar_t, gam = rw_prep(u, prm)
    outs = []
    finals = []
    for d, h0 in ((0, h0_f), (1, h0_b)):
        y, hfin = rw_scan(rt, kkt, v, kbar_t, bbar_t, gam, h0, d)
        outs.append(y)
        finals.append(hfin)
    if not with_output:
        return None, finals[0], finals[1]
    out = rw_post(outs[0], outs[1], bv, g, prm["lnx_g"], prm["lnx_b"])
    return out, finals[0], finals[1]


NA_RQ = 4


def _na_body(q_ref, k_ref, v_ref, kc_ref, vc_ref, bt_ref, o_ref, *, n_rows):
    rb = pl.program_id(1)
    scale = NA_HEAD ** -0.5
    kc = kc_ref[...]
    vc = vc_ref[...]
    for qi in range(NA_RQ):
        r = rb * NA_RQ + qi
        start = jnp.clip(r - NA_KH // 2, 0, n_rows - NA_KH)
        dr0 = start - r + NA_KH - 1
        tok0 = pl.multiple_of(start * GRID_W, GRID_W)
        q = q_ref[qi * GRID_W:(qi + 1) * GRID_W, :]
        kw = k_ref[pl.ds(tok0, NA_KH * GRID_W), :]
        vw = v_ref[pl.ds(tok0, NA_KH * GRID_W), :]
        s = _dot_nt(q, kw) * scale
        bias = jnp.concatenate([bt_ref[0, dr0 + 2 * m] for m in range(NA_KH // 2)], axis=1)
        s = s + bias
        sc = _dot_nt(q, kc) * scale
        mx = jnp.maximum(jnp.max(s, axis=-1, keepdims=True), jnp.max(sc, axis=-1, keepdims=True))
        p = jnp.exp(s - mx)
        pc = jnp.exp(sc - mx)
        den = jnp.sum(p, axis=-1, keepdims=True) + jnp.sum(pc, axis=-1, keepdims=True)
        o = (_dot(p.astype(BF16), vw) + _dot(pc.astype(BF16), vc)) / den
        o_ref[qi * GRID_W:(qi + 1) * GRID_W, :] = o.astype(o_ref.dtype)


def _na_bias_table(rpb):
    cols = jnp.arange(GRID_W)
    col_start = jnp.clip(cols - NA_KW // 2, 0, GRID_W - NA_KW)
    col_ok = (cols[None, :] >= col_start[:, None]) & (cols[None, :] < col_start[:, None] + NA_KW)
    dc = jnp.clip(cols[None, :] - cols[:, None] + NA_KW - 1, 0, 2 * NA_KW - 2)
    bt = rpb.astype(F32)[:, :, dc]
    bt = jnp.where(col_ok[None, None], bt, NEG_INF)
    return jnp.concatenate([bt[:, :-1], bt[:, 1:]], axis=-1)


def na_latent(att_l, att_c, rpb):
    s_len = att_l.shape[0]
    n_rows = s_len // GRID_W
    qb = NA_W // NA_HEAD
    bt = _na_bias_table(rpb)
    return pl.pallas_call(
        functools.partial(_na_body, n_rows=n_rows),
        grid=(NA_HEADS, n_rows // NA_RQ),
        in_specs=[
            pl.BlockSpec((NA_RQ * GRID_W, NA_HEAD), lambda h, r: (r, h)),
            pl.BlockSpec((s_len, NA_HEAD), lambda h, r: (0, qb + h)),
            pl.BlockSpec((s_len, NA_HEAD), lambda h, r: (0, 2 * qb + h)),
            pl.BlockSpec((CTX_LEN, NA_HEAD), lambda h, r: (0, qb + h)),
            pl.BlockSpec((CTX_LEN, NA_HEAD), lambda h, r: (0, 2 * qb + h)),
            pl.BlockSpec((1, 2 * NA_KH - 2, GRID_W, 2 * GRID_W), lambda h, r: (h, 0, 0, 0)),
        ],
        out_specs=pl.BlockSpec((NA_RQ * GRID_W, NA_HEAD), lambda h, r: (r, h)),
        out_shape=jax.ShapeDtypeStruct((s_len, NA_W), BF16),
        compiler_params=_cparams(2),
        name="na_latent",
    )(att_l, att_l, att_l, att_c, att_c, bt)


def _na_ctx_body(q_ref, k_ref, v_ref, o_ref):
    s = _dot_nt(q_ref[...], k_ref[...]) * (NA_HEAD ** -0.5)
    mx = jnp.max(s, axis=-1, keepdims=True)
    p = jnp.exp(s - mx)
    den = jnp.sum(p, axis=-1, keepdims=True)
    o_ref[...] = (_dot(p.astype(BF16), v_ref[...]) / den).astype(o_ref.dtype)


def na_ctx(att_c):
    qb = NA_W // NA_HEAD
    blk = lambda off: pl.BlockSpec((CTX_LEN, NA_HEAD), lambda h: (0, off + h))
    return pl.pallas_call(
        _na_ctx_body,
        grid=(NA_HEADS,),
        in_specs=[blk(0), blk(qb), blk(2 * qb)],
        out_specs=blk(0),
        out_shape=jax.ShapeDtypeStruct((CTX_LEN, NA_W), BF16),
        compiler_params=_cparams(1),
        name="na_ctx",
    )(att_c, att_c, att_c)


def _rope_body(x_ref, c_ref, s1_ref, s2_ref, o_ref):
    x = x_ref[...].astype(F32)
    quarter = SW_HEAD // 4
    x_up = pltpu.roll(x, LANES - quarter, 1)
    x_dn = pltpu.roll(x, quarter, 1)
    o_ref[...] = (x * c_ref[...] + x_up * s1_ref[...] + x_dn * s2_ref[...]).astype(o_ref.dtype)


def _rope_tables(s_len):
    t = jnp.arange(s_len)
    row = (t // GRID_W).astype(F32)
    col = (t % GRID_W).astype(F32)
    half = SW_HEAD // 2
    inv = ROPE_BASE ** (-jnp.arange(0, half, 2, dtype=F32) / half)
    ang_r = row[:, None] * inv[None, :]
    ang_c = col[:, None] * inv[None, :]
    cos = jnp.concatenate([jnp.cos(ang_r)] * 2 + [jnp.cos(ang_c)] * 2, axis=-1)
    sin_r, sin_c = jnp.sin(ang_r), jnp.sin(ang_c)
    zero = jnp.zeros_like(sin_r)
    s_up = jnp.concatenate([-sin_r, zero, -sin_c, zero], axis=-1)
    s_dn = jnp.concatenate([zero, sin_r, zero, sin_c], axis=-1)
    tile2 = lambda a: jnp.concatenate([a, a], axis=-1)
    return tile2(cos), tile2(s_up), tile2(s_dn)


def rope_qk(att_l):
    s_len = att_l.shape[0]
    tm = 512
    nblk = (SW_W + SW_KV * SW_HEAD) // LANES
    off = NA_COLS // LANES
    cos, s_up, s_dn = _rope_tables(s_len)
    tab = pl.BlockSpec((tm, LANES), lambda i, j: (i, 0))
    return pl.pallas_call(
        _rope_body,
        grid=(s_len // tm, nblk),
        in_specs=[pl.BlockSpec((tm, LANES), lambda i, j: (i, off + j)), tab, tab, tab],
        out_specs=pl.BlockSpec((tm, LANES), lambda i, j: (i, j)),
        out_shape=jax.ShapeDtypeStruct((s_len, nblk * LANES), BF16),
        compiler_params=_cparams(2),
        name="rope_qk",
    )(att_l, cos, s_up, s_dn)


def _sw_softmax_out(q8, s_parts, v_parts, sink_col):
    mx = sink_col
    for s in s_parts:
        mx = jnp.maximum(mx, jnp.max(s, axis=-1, keepdims=True))
    den = jnp.exp(sink_col - mx)
    o = None
    for s, v in zip(s_parts, v_parts):
        p = jnp.exp(s - mx)
        den = den + jnp.sum(p, axis=-1, keepdims=True)
        pv = _dot(p.astype(BF16), v)
        o = pv if o is None else o + pv
    return o / den


def _sink_column(sink_ref, kh, rows_per_head):
    n = SW_GROUP * rows_per_head
    grp = lax.broadcasted_iota(jnp.int32, (n, 1), 0) // rows_per_head
    col = jnp.zeros((n, 1), F32)
    for g in range(SW_GROUP):
        col = jnp.where(grp == g, sink_ref[kh * SW_GROUP + g], col)
    return col


def _sw_body(sink_ref, q_ref, k_ref, v_ref, kc_ref, vc_ref, o_ref, *, s_len):
    n = pl.program_id(0)
    scale = SW_HEAD ** -0.5
    win = 3 * SW_BLOCK
    start = pl.multiple_of(jnp.clip((n - 1) * SW_BLOCK, 0, s_len - win), SW_BLOCK)
    k2 = k_ref[pl.ds(start, win), :]
    v2 = v_ref[pl.ds(start, win), :]
    qpos = n * SW_BLOCK + lax.broadcasted_iota(jnp.int32, (SW_BLOCK, win), 0)
    kpos = start + lax.broadcasted_iota(jnp.int32, (SW_BLOCK, win), 1)
    ok = jnp.abs(kpos - qpos) <= SW_WIN
    for kh in range(SW_KV):
        lanes = slice(kh * SW_HEAD, (kh + 1) * SW_HEAD)
        q8 = jnp.concatenate(
            [q_ref[:, (kh * SW_GROUP + g) * SW_HEAD:(kh * SW_GROUP + g + 1) * SW_HEAD] for g in range(SW_GROUP)],
            axis=0)
        s_loc = _dot_nt(q8, k2[:, lanes]) * scale
        s_loc = jnp.where(ok[None], s_loc.reshape(SW_GROUP, SW_BLOCK, win), NEG_INF).reshape(
            SW_GROUP * SW_BLOCK, win)
        s_ctx = _dot_nt(q8, kc_ref[:, lanes]) * scale
        o = _sw_softmax_out(q8, [s_loc, s_ctx], [v2[:, lanes], vc_ref[:, lanes]],
                            _sink_column(sink_ref, kh, SW_BLOCK))
        for g in range(SW_GROUP):
            c0 = (kh * SW_GROUP + g) * SW_HEAD
            o_ref[:, c0:c0 + SW_HEAD] = o[g * SW_BLOCK:(g + 1) * SW_BLOCK].astype(o_ref.dtype)


def swa_latent(qk_rot, att_l, att_c, sink):
    s_len = qk_rot.shape[0]
    kcol = SW_W // LANES
    ck = (NA_COLS + SW_W) // LANES
    return pl.pallas_call(
        functools.partial(_sw_body, s_len=s_len),
        grid=(s_len // SW_BLOCK,),
        in_specs=[
            pl.BlockSpec(memory_space=pltpu.SMEM),
            pl.BlockSpec((SW_BLOCK, SW_W), lambda n: (n, 0)),
            pl.BlockSpec((s_len, LANES), lambda n: (0, kcol)),
            pl.BlockSpec((s_len, LANES), lambda n: (0, ck + 1)),
            pl.BlockSpec((CTX_LEN, LANES), lambda n: (0, ck)),
            pl.BlockSpec((CTX_LEN, LANES), lambda n: (0, ck + 1)),
        ],
        out_specs=pl.BlockSpec((SW_BLOCK, SW_W), lambda n: (n, 0)),
        out_shape=jax.ShapeDtypeStruct((s_len, SW_W), BF16),
        compiler_params=_cparams(1),
        name="swa_latent",
    )(sink, qk_rot, qk_rot, att_l, att_c, att_c)


def _sw_ctx_body(sink_ref, q_ref, k_ref, v_ref, o_ref):
    scale = SW_HEAD ** -0.5
    for kh in range(SW_KV):
        lanes = slice(kh * SW_HEAD, (kh + 1) * SW_HEAD)
        q8 = jnp.concatenate(
            [q_ref[:, (kh * SW_GROUP + g) * SW_HEAD:(kh * SW_GROUP + g + 1) * SW_HEAD] for g in range(SW_GROUP)],
            axis=0)
        s = _dot_nt(q8, k_ref[:, lanes]) * scale
        o = _sw_softmax_out(q8, [s], [v_ref[:, lanes]], _sink_column(sink_ref, kh, CTX_LEN))
        for g in range(SW_GROUP):
            c0 = (kh * SW_GROUP + g) * SW_HEAD
            o_ref[:, c0:c0 + SW_HEAD] = o[g * CTX_LEN:(g + 1) * CTX_LEN].astype(o_ref.dtype)


def swa_ctx(att_c, sink):
    qo = NA_COLS // SW_W
    ck = (NA_COLS + SW_W) // LANES
    return pl.pallas_call(
        _sw_ctx_body,
        grid=(1,),
        in_specs=[
            pl.BlockSpec(memory_space=pltpu.SMEM),
            pl.BlockSpec((CTX_LEN, SW_W), lambda n: (0, qo)),
            pl.BlockSpec((CTX_LEN, LANES), lambda n: (0, ck)),
            pl.BlockSpec((CTX_LEN, LANES), lambda n: (0, ck + 1)),
        ],
        out_specs=pl.BlockSpec((CTX_LEN, SW_W), lambda n: (0, 0)),
        out_shape=jax.ShapeDtypeStruct((CTX_LEN, SW_W), BF16),
        compiler_params=_cparams(1),
        name="swa_ctx",
    )(sink, att_c, att_c, att_c)


def _router_body(x_ref, g_ref, sc_ref, sh_ref, wr_ref, rb_ref, h_ref, pick_ref, hp_ref):
    x = x_ref[...]
    tm = x.shape[0]
    ms = jnp.mean(x * x, axis=-1, keepdims=True)
    h = (x * lax.rsqrt(ms + NORM_EPS) * g_ref[...]) * (1.0 + sc_ref[...]) + sh_ref[...]
    h_ref[...] = h.astype(h_ref.dtype)

    hh, hl = _split2(h)
    wh, wl = _split2(wr_ref[...])
    logits = _dot(hh, wh) + _dot(hl, wh) + _dot(hh, wl)
    lane = lax.broadcasted_iota(jnp.int32, (tm, LANES), 1)
    lane_f = lane.astype(F32)
    valid = lane < N_EXPERTS
    neg = -jnp.inf
    big = float(2 * LANES)
    scores = _sigmoid(logits)
    biased = jnp.where(valid, scores + rb_ref[...], neg)
    grp = lane // EXPERTS_PER_GROUP

    def first_argmax(vals):
        m = jnp.max(vals, axis=-1, keepdims=True)
        idx = jnp.min(jnp.where(vals == m, lane_f, big), axis=-1, keepdims=True)
        return m, idx

    gscore = jnp.full((tm, LANES), neg, F32)
    for gi in range(N_GROUPS):
        vg = jnp.where(grp == gi, biased, neg)
        m1, i1 = first_argmax(vg)
        m2 = jnp.max(jnp.where(lane_f == i1, neg, vg), axis=-1, keepdims=True)
        gscore = jnp.where(lane == gi * EXPERTS_PER_GROUP, m1 + m2, gscore)
    keep = jnp.zeros((tm, LANES), jnp.bool_)
    for _ in range(TOPK_GROUPS):
        _, gi = first_argmax(gscore)
        keep = keep | (grp.astype(F32) == jnp.floor(gi * (1.0 / EXPERTS_PER_GROUP)))
        gscore = jnp.where(lane_f == gi, neg, gscore)
    cur = jnp.where(valid, jnp.where(keep, biased, NEG_INF), neg)
    sel = jnp.zeros((tm, LANES), jnp.bool_)
    picks = jnp.zeros((tm, LANES), F32)
    for k in range(TOP_K):
        _, ei = first_argmax(cur)
        hit = lane_f == ei
        sel = sel | hit
        cur = jnp.where(hit, neg, cur)
        picks = jnp.where(lane == k, ei, picks)
        picks = jnp.where(lane == TOP_K + k, jnp.sum(jnp.where(hit, scores, 0.0), axis=-1, keepdims=True), picks)
    w = jnp.where(sel, scores, 0.0)
    norm = ROUTE_SCALE / jnp.sum(w, axis=-1, keepdims=True)
    pick_ref[...] = jnp.where((lane >= TOP_K) & (lane < 2 * TOP_K), picks * norm, picks)

    bits = pltpu.bitcast(h.astype(BF16).astype(F32), jnp.uint32)
    half = D_MODEL // 2
    hp_ref[...] = (bits[:, half:] & jnp.uint32(0xFFFF0000)) | (bits[:, :half] >> 16)


def router(x, g, sc, sh, w_router_pad, r_bias_pad):
    t = x.shape[0]
    tm = 256
    row = pl.BlockSpec((1, D_MODEL), lambda i: (0, 0))
    return pl.pallas_call(
        _router_body,
        grid=(t // tm,),
        in_specs=[pl.BlockSpec((tm, D_MODEL), lambda i: (i, 0)), row, row, row,
                  pl.BlockSpec((D_MODEL, LANES), lambda i: (0, 0)), pl.BlockSpec((1, LANES), lambda i: (0, 0))],
        out_specs=[pl.BlockSpec((tm, D_MODEL), lambda i: (i, 0)), pl.BlockSpec((tm, LANES), lambda i: (i, 0)),
                   pl.BlockSpec((tm, D_MODEL // 2), lambda i: (i, 0))],
        out_shape=[jax.ShapeDtypeStruct((t, D_MODEL), BF16), jax.ShapeDtypeStruct((t, LANES), F32),
                   jax.ShapeDtypeStruct((t, D_MODEL // 2), jnp.uint32)],
        compiler_params=_cparams(1),
        name="router",
    )(x, g, sc, sh, w_router_pad, r_bias_pad)


MOE_TM = 512
PACK_W = D_MODEL // 2
MOE_DOWN_COLS = 1024


def _unpack_bf16_pairs(words):
    lo = pltpu.bitcast(words << 16, F32)
    hi = pltpu.bitcast(words & jnp.uint32(0xFFFF0000), F32)
    return lo, hi


def _pack_bf16_pairs(lo, hi):
    lo_bits = pltpu.bitcast(lo.astype(BF16).astype(F32), jnp.uint32) >> 16
    hi_bits = pltpu.bitcast(hi.astype(BF16).astype(F32), jnp.uint32) & jnp.uint32(0xFFFF0000)
    return hi_bits | lo_bits


def _routed_body(te_ref, cnt_ref, nv_ref, src_ref, dst_ref, w_ref, hp_hbm, wg_ref, wu_ref, wd_ref, yk_hbm,
                 xbuf, ybuf, sems):
    i = pl.program_id(0)
    tm = MOE_TM

    def gather_copy(r):
        return pltpu.make_async_copy(hp_hbm.at[pl.ds(src_ref[0, 0, r], 1)], xbuf.at[pl.ds(r, 1)], sems.at[0])

    def scatter_copy(r):
        return pltpu.make_async_copy(ybuf.at[pl.ds(r, 1)], yk_hbm.at[pl.ds(dst_ref[0, 0, r], 1)], sems.at[1])

    @pl.when(i == 0)
    def _():
        xbuf[...] = jnp.zeros_like(xbuf)

    @pl.when(i < nv_ref[0])
    def _():
        n = cnt_ref[i]

        def start_gather(r, c):
            gather_copy(r).start()
            return c

        def wait_gather(r, c):
            gather_copy(r).wait()
            return c

        lax.fori_loop(0, n, start_gather, 0)
        lax.fori_loop(0, n, wait_gather, 0)

        x_lo, x_hi = _unpack_bf16_pairs(xbuf[...])
        x_lo = x_lo.astype(BF16)
        x_hi = x_hi.astype(BF16)
        hg = _dot(x_lo, wg_ref[0, :PACK_W].astype(BF16)) + _dot(x_hi, wg_ref[0, PACK_W:].astype(BF16))
        hu = _dot(x_lo, wu_ref[0, :PACK_W].astype(BF16)) + _dot(x_hi, wu_ref[0, PACK_W:].astype(BF16))
        ri = lax.broadcasted_iota(jnp.int32, (tm, tm), 0)
        ci = lax.broadcasted_iota(jnp.int32, (tm, tm), 1)
        w_col = jnp.sum(jnp.where(ri == ci, jnp.broadcast_to(w_ref[0], (tm, tm)), 0.0), axis=-1, keepdims=True)
        act = ((hg * _sigmoid(hg)) * hu * w_col).astype(BF16)
        for c0 in range(0, PACK_W, MOE_DOWN_COLS):
            y_lo = _dot(act, wd_ref[0, :, c0:c0 + MOE_DOWN_COLS].astype(BF16))
            y_hi = _dot(act, wd_ref[0, :, PACK_W + c0:PACK_W + c0 + MOE_DOWN_COLS].astype(BF16))
            ybuf[:, c0:c0 + MOE_DOWN_COLS] = _pack_bf16_pairs(y_lo, y_hi)

        def start_scatter(r, c):
            scatter_copy(r).start()
            return c

        def wait_scatter(r, c):
            scatter_copy(r).wait()
            return c

        lax.fori_loop(0, n, start_scatter, 0)
        lax.fori_loop(0, n, wait_scatter, 0)


def _dispatch_plan(picks, t):
    tm = MOE_TM
    n_pairs = TOP_K * t
    n_tiles = (n_pairs + N_EXPERTS * (tm - 1) + tm - 1) // tm
    e_flat = picks[:, :TOP_K].astype(jnp.int32).reshape(-1)
    w_flat = picks[:, TOP_K:2 * TOP_K].reshape(-1)
    order = jnp.argsort(e_flat, stable=True).astype(jnp.int32)
    sorted_e = e_flat[order]
    experts = jnp.arange(N_EXPERTS, dtype=jnp.int32)
    first = jnp.searchsorted(sorted_e, experts, side="left").astype(jnp.int32)
    counts = jnp.searchsorted(sorted_e, experts, side="right").astype(jnp.int32) - first
    padded = ((counts + tm - 1) // tm) * tm
    group_end = jnp.cumsum(padded)
    group_start = group_end - padded
    n_valid = (group_end[-1] // tm).astype(jnp.int32)
    tile_start = jnp.arange(n_tiles, dtype=jnp.int32) * tm
    tile_e = jnp.minimum(jnp.searchsorted(group_end, tile_start, side="right").astype(jnp.int32), N_EXPERTS - 1)
    tile_cnt = jnp.clip(counts[tile_e] - (tile_start - group_start[tile_e]), 0, tm)
    tile_cnt = jnp.where(jnp.arange(n_tiles) < n_valid, tile_cnt, 0).astype(jnp.int32)
    last_e = tile_e[jnp.maximum(n_valid - 1, 0)]
    tile_e = jnp.where(jnp.arange(n_tiles) < n_valid, tile_e, last_e)
    j = jnp.arange(tm, dtype=jnp.int32)[None, :]
    ok = j < tile_cnt[:, None]
    q = jnp.clip(first[tile_e][:, None] + (tile_start - group_start[tile_e])[:, None] + j, 0, n_pairs - 1)
    p = order[q]
    tok = p // TOP_K
    src = jnp.where(ok, tok, 0)
    dst = jnp.where(ok, (p % TOP_K) * t + tok, 0)
    w = jnp.where(ok, w_flat[p], 0.0)
    return (tile_e, tile_cnt, n_valid.reshape(1), src[:, None, :], dst[:, None, :], w[:, None, :])


def moe_routed(hp, picks, we_gate, we_up, we_down):
    t = hp.shape[0]
    tm = MOE_TM
    tile_e, tile_cnt, n_valid, src, dst, w = _dispatch_plan(picks, t)
    n_tiles = tile_e.shape[0]
    smem_rows = pl.BlockSpec((1, 1, tm), lambda i, te, cnt, nv: (i, 0, 0), memory_space=pltpu.SMEM)
    grid_spec = pltpu.PrefetchScalarGridSpec(
        num_scalar_prefetch=3,
        grid=(n_tiles,),
        in_specs=[
            smem_rows, smem_rows,
            pl.BlockSpec((1, 1, tm), lambda i, te, cnt, nv: (i, 0, 0)),
            pl.BlockSpec(memory_space=pl.ANY),
            pl.BlockSpec((1, D_MODEL, D_EXPERT), lambda i, te, cnt, nv: (te[i], 0, 0)),
            pl.BlockSpec((1, D_MODEL, D_EXPERT), lambda i, te, cnt, nv: (te[i], 0, 0)),
            pl.BlockSpec((1, D_EXPERT, D_MODEL), lambda i, te, cnt, nv: (te[i], 0, 0)),
        ],
        out_specs=pl.BlockSpec(memory_space=pl.ANY),
        scratch_shapes=[pltpu.VMEM((tm, PACK_W), jnp.uint32), pltpu.VMEM((tm, PACK_W), jnp.uint32),
                        pltpu.SemaphoreType.DMA((2,))],
    )
    return pl.pallas_call(
        _routed_body,
        grid_spec=grid_spec,
        out_shape=jax.ShapeDtypeStruct((TOP_K * t, PACK_W), jnp.uint32),
        compiler_params=_cparams(1),
        name="moe_routed",
    )(tile_e, tile_cnt, n_valid, src, dst, w, hp, we_gate, we_up, we_down)


COMBINE_TM = 128


def _combine_body(h_ref, yk_ref, wg_ref, wu_ref, wd_ref, x_ref, ga_ref, o_ref):
    h = h_ref[...]
    hg = _dot(h, wg_ref[...])
    hu = _dot(h, wu_ref[...])
    act = ((hg * _sigmoid(hg)) * hu).astype(BF16)
    for half in range(2):
        cols = slice(half * PACK_W, (half + 1) * PACK_W)
        acc = _dot(act, wd_ref[:, cols])
        for k in range(TOP_K):
            acc = acc + _unpack_bf16_pairs(yk_ref[k])[half]
        o_ref[:, cols] = x_ref[:, cols] + ga_ref[:, cols] * acc


def moe_combine(h, yk, ws_gate, ws_up, ws_down, x, ga):
    t = h.shape[0]
    tm = COMBINE_TM
    tok = lambda dt_cols: pl.BlockSpec((tm, dt_cols), lambda i: (i, 0))
    full = lambda a: pl.BlockSpec(a.shape, lambda i: (0,) * a.ndim)
    return pl.pallas_call(
        _combine_body,
        grid=(t // tm,),
        in_specs=[tok(D_MODEL), pl.BlockSpec((TOP_K, tm, PACK_W), lambda i: (0, i, 0)),
                  full(ws_gate), full(ws_up), full(ws_down), tok(D_MODEL), pl.BlockSpec((1, D_MODEL), lambda i: (0, 0))],
        out_specs=tok(D_MODEL),
        out_shape=jax.ShapeDtypeStruct((t, D_MODEL), F32),
        compiler_params=_cparams(1),
        name="moe_combine",
    )(h, yk.reshape(TOP_K, t, PACK_W), ws_gate, ws_up, ws_down, x, ga)


def _pad_cols(a, n):
    return jnp.pad(a, [(0, 0)] * (a.ndim - 1) + [(0, n - a.shape[-1])])


def _rw_params(i, rw_conv, rw_w0, rw_w2, rw_a0, rw_a2, rw_g2, rw_kk, rw_ka, rw_rk, rw_lnx_g, rw_lnx_b):
    w2 = jnp.zeros((2, LORA_ALL, RW_W), F32)
    a2 = jnp.zeros((2, LORA_ALL, RW_W), F32)
    for d in range(2):
        w2 = w2.at[d, d * LORA_W:(d + 1) * LORA_W].set(rw_w2[i, d])
        a2 = a2.at[d, 2 * LORA_W + d * LORA_A:2 * LORA_W + (d + 1) * LORA_A].set(rw_a2[i, d])
    taps = jnp.pad(rw_conv[i], ((0, SUBLANES - 3), (0, RW_COLS_PAD - RW_COLS)))
    return dict(
        taps=taps, w0=rw_w0[i], a0=rw_a0[i], w2=w2.astype(BF16), a2=a2.astype(BF16), g2=rw_g2[i].astype(BF16),
        kks=rw_kk[i][None], ka=rw_ka[i][None], rk=rw_rk[i].reshape(1, RW_W),
        lnx_g=rw_lnx_g[i][None], lnx_b=rw_lnx_b[i][None])


def kernel(x, c, ctx, c_ctx, w_ada, b_ada, norm1_g, norm2_g, w_in, rw_conv, rw_w0, rw_w2, rw_a0, rw_a2, rw_g2,
           rw_kk, rw_ka, rw_rk, rw_lnx_g, rw_lnx_b, na_rpb, sw_sink, w_out, w_router, router_bias, we_gate,
           we_up, we_down, ws_gate, ws_up, ws_down, final_g):
    assert x.shape[0] == 1 and x.shape[2] == D_MODEL and ctx.shape[1] == CTX_LEN
    xl = x[0]
    xc = ctx[0]
    cond8 = jnp.zeros((SUBLANES, D_MODEL), F32).at[0].set(c[0]).at[1].set(c_ctx)
    mods = ada_mod(cond8, w_ada, b_ada)

    for i in range(DEPTH):
        ctx_needed = i < DEPTH - 1
        mod_l = [m[None] for m in mods[i, 0].reshape(6, D_MODEL)]
        mod_c = [m[None] for m in mods[i, 1].reshape(6, D_MODEL)]
        w_rw = _pad_cols(w_in[i][:, :RW_COLS], RW_COLS_PAD).astype(BF16)
        w_att = w_in[i][:, RW_COLS:].astype(BF16)
        w_o = w_out[i].astype(BF16)
        w_o_parts = [w_o[:RW_W], w_o[RW_W:RW_W + NA_W], w_o[RW_W + NA_W:]]
        rw_prm = _rw_params(i, rw_conv, rw_w0, rw_w2, rw_a0, rw_a2, rw_g2, rw_kk, rw_ka, rw_rk, rw_lnx_g,
                            rw_lnx_b)
        n1 = norm1_g[i][None]
        n2 = norm2_g[i][None]

        hl = norm_mod(xl, n1, mod_l[1], mod_l[0])
        hc = norm_mod(xc, n1, mod_c[1], mod_c[0])
        url = matmul([hl], [w_rw], 768, F32, name="in_proj_rw")
        urc = matmul([hc], [w_rw], 768, F32, name="in_proj_rw_ctx")
        ual = matmul([hl], [w_att], 256, BF16, name="in_proj_att")
        uac = matmul([hc], [w_att], 256, BF16, name="in_proj_att_ctx")

        z = jnp.zeros((RW_PAIRS, LANES, LANES), F32)
        rwc, s_f, s_b = rwkv_group(urc, rw_prm, z, z, ctx_needed)
        rwl, _, _ = rwkv_group(url, rw_prm, s_f, s_b, True)
        nal = na_latent(ual, uac, na_rpb[i])
        swl = swa_latent(rope_qk(ual), ual, uac, sw_sink[i])
        xl = matmul([rwl, nal, swl], w_o_parts, 512, F32, residual=(xl, mod_l[2]), name="out_proj")

        wr = _pad_cols(w_router[i], LANES)
        rb = _pad_cols(router_bias[i][None], LANES)
        shared = (ws_gate[i].astype(BF16), ws_up[i].astype(BF16), ws_down[i].astype(BF16))

        def moe(xt, mod):
            h2, picks, hp = router(xt, n2, mod[4], mod[3], wr, rb)
            yk = moe_routed(hp, picks, we_gate[i], we_up[i], we_down[i])
            return moe_combine(h2, yk, *shared, xt, mod[5])

        xl = moe(xl, mod_l)
        if ctx_needed:
            nac = na_ctx(uac)
            swc = swa_ctx(uac, sw_sink[i])
            xc = matmul([rwc, nac, swc], w_o_parts, 512, F32, residual=(xc, mod_c[2]), name="out_proj_ctx")
            xc = moe(xc, mod_c)
    return final_norm(xl, final_g[None])[None]
```

```python
import functools

import jax
import jax.numpy as jnp
from jax import lax
from jax.experimental import pallas as pl
from jax.experimental.pallas import tpu as pltpu

F32 = jnp.float32
BF16 = jnp.bfloat16

D_MODEL = 4096
DEPTH = 2
GRID_W = 64
CTX_LEN = 256
NORM_EPS = 1e-6
NEG_INF = -1e30

RW_HEAD = 64
RW_W = D_MODEL // 2
RW_HEADS = RW_W // RW_HEAD
LORA_W = 96
LORA_A = 96
LORA_G = 256
GN_EPS = 64e-5
RW_COLS = 3 * RW_W + LORA_G + 2 * LORA_W + 2 * LORA_A
LORA_OFF = 3 * RW_W + LORA_G
LORA_ALL = 2 * LORA_W + 2 * LORA_A

NA_HEAD = 128
NA_W = D_MODEL // 4
NA_HEADS = NA_W // NA_HEAD
NA_KH = 8
NA_KW = 16
NA_COLS = 3 * NA_W

SW_HEAD = 64
SW_W = D_MODEL - RW_W - NA_W
SW_HEADS = SW_W // SW_HEAD
SW_KV = SW_HEADS // 8
SW_GROUP = SW_HEADS // SW_KV
SW_WIN = 128
SW_BLOCK = 128
ROPE_BASE = 10000.0
SW_COLS = SW_W + 2 * SW_KV * SW_HEAD
ATT_COLS = NA_COLS + SW_COLS

N_EXPERTS = 64
N_GROUPS = 8
EXPERTS_PER_GROUP = N_EXPERTS // N_GROUPS
TOPK_GROUPS = 4
TOP_K = 8
D_EXPERT = 256
ROUTE_SCALE = 2.5

LANES = 128
SUBLANES = 8
RW_COLS_PAD = 6912
CHUNK = 64
VMEM_LIMIT = 56 * 1024 * 1024


def _cparams(n_axes, vmem=VMEM_LIMIT):
    return pltpu.CompilerParams(dimension_semantics=("arbitrary",) * n_axes, vmem_limit_bytes=vmem)


def _dot(a, b):
    return jnp.dot(a, b, preferred_element_type=F32)


def _dot_nt(a, b):
    return lax.dot_general(a, b, (((1,), (1,)), ((), ())), preferred_element_type=F32)


def _split2(x):
    hi = x.astype(BF16)
    lo = (x - hi.astype(F32)).astype(BF16)
    return hi, lo


def _split3(x):
    hi = x.astype(BF16)
    r1 = x - hi.astype(F32)
    mid = r1.astype(BF16)
    lo = (r1 - mid.astype(F32)).astype(BF16)
    return hi, mid, lo


def _sigmoid(x):
    return 1.0 / (1.0 + jnp.exp(-x))


def _ada_body(s_ref, w_ref, b_ref, o_ref):
    c = s_ref[...]
    s = (c * _sigmoid(c)).astype(BF16)
    o_ref[0] = _dot(s, w_ref[0].astype(BF16)) + b_ref[0]


def ada_mod(cond8, w_ada, b_ada):
    bn = 512
    n = 6 * D_MODEL
    return pl.pallas_call(
        _ada_body,
        grid=(DEPTH, n // bn),
        in_specs=[
            pl.BlockSpec((SUBLANES, D_MODEL), lambda l, j: (0, 0)),
            pl.BlockSpec((1, D_MODEL, bn), lambda l, j: (l, 0, j)),
            pl.BlockSpec((1, 1, bn), lambda l, j: (l, 0, j)),
        ],
        out_specs=pl.BlockSpec((1, SUBLANES, bn), lambda l, j: (l, 0, j)),
        out_shape=jax.ShapeDtypeStruct((DEPTH, SUBLANES, n), F32),
        compiler_params=_cparams(2),
        name="ada_mod",
    )(cond8, w_ada, b_ada.reshape(DEPTH, 1, n))


def _norm_mod_body(x_ref, g_ref, sc_ref, sh_ref, o_ref):
    x = x_ref[...]
    ms = jnp.mean(x * x, axis=-1, keepdims=True)
    y = x * lax.rsqrt(ms + NORM_EPS) * g_ref[...]
    o_ref[...] = (y * (1.0 + sc_ref[...]) + sh_ref[...]).astype(o_ref.dtype)


def norm_mod(x, g, sc, sh, out_dtype=BF16):
    t = x.shape[0]
    tm = 256
    row = pl.BlockSpec((1, D_MODEL), lambda i: (0, 0))
    return pl.pallas_call(
        _norm_mod_body,
        grid=(t // tm,),
        in_specs=[pl.BlockSpec((tm, D_MODEL), lambda i: (i, 0)), row, row, row],
        out_specs=pl.BlockSpec((tm, D_MODEL), lambda i: (i, 0)),
        out_shape=jax.ShapeDtypeStruct((t, D_MODEL), out_dtype),
        compiler_params=_cparams(1),
        name="norm_mod",
    )(x, g, sc, sh)


def _final_norm_body(x_ref, g_ref, o_ref):
    x = x_ref[...]
    ms = jnp.mean(x * x, axis=-1, keepdims=True)
    o_ref[...] = x * lax.rsqrt(ms + NORM_EPS) * g_ref[...]


def final_norm(x, g):
    t = x.shape[0]
    tm = 256
    return pl.pallas_call(
        _final_norm_body,
        grid=(t // tm,),
        in_specs=[pl.BlockSpec((tm, D_MODEL), lambda i: (i, 0)), pl.BlockSpec((1, D_MODEL), lambda i: (0, 0))],
        out_specs=pl.BlockSpec((tm, D_MODEL), lambda i: (i, 0)),
        out_shape=jax.ShapeDtypeStruct((t, D_MODEL), F32),
        compiler_params=_cparams(1),
        name="final_norm",
    )(x, g)


def _mm_body(*refs, n_in, residual):
    a_refs = refs[:n_in]
    w_refs = refs[n_in:2 * n_in]
    rest = refs[2 * n_in:]
    acc = _dot(a_refs[0][...], w_refs[0][...])
    for a_ref, w_ref in zip(a_refs[1:], w_refs[1:]):
        acc = acc + _dot(a_ref[...], w_ref[...])
    if residual:
        x_ref, ga_ref, o_ref = rest
        o_ref[...] = x_ref[...] + ga_ref[...] * acc
    else:
        (o_ref,) = rest
        o_ref[...] = acc.astype(o_ref.dtype)


def matmul(a_list, w_list, bn, out_dtype, residual=None, name="matmul"):
    m = a_list[0].shape[0]
    n = w_list[0].shape[1]
    bm = 1024 if m % 1024 == 0 else 256
    in_specs = [pl.BlockSpec((bm, a.shape[1]), lambda i, j: (i, 0)) for a in a_list]
    in_specs += [pl.BlockSpec((w.shape[0], bn), lambda i, j: (0, j)) for w in w_list]
    args = list(a_list) + list(w_list)
    if residual is not None:
        in_specs += [pl.BlockSpec((bm, bn), lambda i, j: (i, j)), pl.BlockSpec((1, bn), lambda i, j: (0, j))]
        args += list(residual)
    return pl.pallas_call(
        functools.partial(_mm_body, n_in=len(a_list), residual=residual is not None),
        grid=(m // bm, n // bn),
        in_specs=in_specs,
        out_specs=pl.BlockSpec((bm, bn), lambda i, j: (i, j)),
        out_shape=jax.ShapeDtypeStruct((m, n), out_dtype),
        compiler_params=_cparams(2),
        name=name,
    )(*args)


RW_TM = 128


def _seg_sum_bcast(x, e_ref, et_ref):
    hi, lo = _split2(x)
    s = _dot(hi, e_ref[...]) + _dot(lo, e_ref[...])
    shi, slo = _split2(s)
    return _dot(shi, et_ref[...]) + _dot(slo, et_ref[...])


def _rw_prep_body(u_ref, up_ref, un_ref, taps_ref, w0_ref, a0_ref, w2_ref, a2_ref, g2_ref, kks_ref, ka_ref,
                  rk_ref, e_ref, et_ref, tri_ref, sel_ref,
                  v_ref, g_ref, bv_ref, rt_ref, kkt_ref, kbar_ref, bbar_ref, gam_ref):
    i = pl.program_id(0)
    nt = pl.num_programs(0)
    tm = RW_TM
    rows = lax.broadcasted_iota(jnp.int32, (tm, 1), 0)
    has_prev = (i > 0).astype(F32)
    has_next = (i < nt - 1).astype(F32)

    def conv(c0, c1):
        u = u_ref[:, c0:c1]
        prev_row = up_ref[SUBLANES - 1:SUBLANES, c0:c1] * has_prev
        next_row = un_ref[0:1, c0:c1] * has_next
        u_prev = jnp.where(rows == 0, prev_row, pltpu.roll(u, 1, 0))
        u_next = jnp.where(rows == tm - 1, next_row, pltpu.roll(u, tm - 1, 0))
        return (u_prev * taps_ref[0:1, c0:c1] + u * taps_ref[1:2, c0:c1] + u_next * taps_ref[2:3, c0:c1])

    r = conv(0, RW_W)
    k = conv(RW_W, 2 * RW_W)
    v = conv(2 * RW_W, 3 * RW_W)
    gd = conv(3 * RW_W, LORA_OFF)
    lo_in = conv(LORA_OFF, LORA_OFF + LORA_ALL)

    v_ref[...] = v.astype(v_ref.dtype)
    g_ref[...] = _dot(_sigmoid(gd).astype(BF16), g2_ref[...])

    kk = k * kks_ref[...]
    ssq = _seg_sum_bcast(kk * kk, e_ref, et_ref)
    kkn = kk / jnp.maximum(jnp.sqrt(ssq), 1e-12)

    tanh_lo = jnp.tanh(lo_in).astype(BF16)
    raw_lo = lo_in.astype(BF16)
    bonus = jnp.zeros((tm, RW_W), F32)
    for d in range(2):
        w_pre = w0_ref[d:d + 1, :] + _dot(tanh_lo, w2_ref[d])
        z = -w_pre
        softplus = jnp.maximum(z, 0.0) + jnp.log(1.0 + jnp.exp(-jnp.abs(z)))
        w_log = -softplus - 0.5
        lw = -jnp.exp(w_log)
        a = _sigmoid(a0_ref[d:d + 1, :] + _dot(raw_lo, a2_ref[d]))
        kd = k * (1.0 + (a - 1.0) * ka_ref[...])
        b = kkn * a
        bonus = bonus + r * kd * rk_ref[...]
        l1, l2, l3 = _split3(lw)
        tri = tri_ref[d]
        sel = sel_ref[...]
        cum = _dot(tri, l1) + _dot(tri, l2) + _dot(tri, l3)
        tot = _dot(sel, l1) + _dot(sel, l2) + _dot(sel, l3)
        e_inv = jnp.exp(-cum)
        rt_ref[d] = (r * jnp.exp(cum)).astype(BF16)
        kkt_ref[d] = (kkn * jnp.exp(cum - lw)).astype(BF16)
        kbar_ref[d] = (kd * e_inv).T.astype(BF16)
        bbar_ref[d] = (b * e_inv).T.astype(BF16)
        gam_ref[d] = jnp.exp(tot)
    bv_ref[...] = _seg_sum_bcast(bonus, e_ref, et_ref) * v


def _rw_consts():
    tm = RW_TM
    t = jnp.arange(tm)
    same = (t[:, None] // CHUNK) == (t[None, :] // CHUNK)
    tri_f = same & (t[None, :] <= t[:, None])
    tri_b = same & (t[None, :] >= t[:, None])
    tri = jnp.stack([tri_f, tri_b]).astype(BF16)
    sel = ((t[None, :] // CHUNK) == jnp.arange(SUBLANES)[:, None]).astype(BF16)
    lane_head = jnp.arange(RW_W) // RW_HEAD
    e = (lane_head[:, None] == jnp.arange(LANES)[None, :]).astype(BF16)
    return tri, sel, e, e.T


def rw_prep(u, prm):
    t = u.shape[0]
    tm = RW_TM
    nt = t // tm
    hb = tm // SUBLANES
    tri, sel, e, et = _rw_consts()
    full = lambda shp: pl.BlockSpec(shp, lambda i: (0,) * len(shp))
    in_specs = [
        pl.BlockSpec((tm, RW_COLS_PAD), lambda i: (i, 0)),
        pl.BlockSpec((SUBLANES, RW_COLS_PAD), lambda i: (jnp.maximum(i * hb - 1, 0), 0)),
        pl.BlockSpec((SUBLANES, RW_COLS_PAD), lambda i: (jnp.minimum((i + 1) * hb, nt * hb - 1), 0)),
        full((SUBLANES, RW_COLS_PAD)),
        full((2, RW_W)), full((2, RW_W)),
        full((2, LORA_ALL, RW_W)), full((2, LORA_ALL, RW_W)),
        full((LORA_G, RW_W)),
        full((1, RW_W)), full((1, RW_W)), full((1, RW_W)),
        full((RW_W, LANES)), full((LANES, RW_W)),
        full((2, tm, tm)), full((SUBLANES, tm)),
    ]
    tok = lambda dt: jax.ShapeDtypeStruct((t, RW_W), dt)
    tok2 = lambda dt: jax.ShapeDtypeStruct((2, t, RW_W), dt)
    spec1 = pl.BlockSpec((tm, RW_W), lambda i: (i, 0))
    spec2 = pl.BlockSpec((2, tm, RW_W), lambda i: (0, i, 0))
    chan2 = jax.ShapeDtypeStruct((2, RW_W, t), BF16)
    spec2t = pl.BlockSpec((2, RW_W, tm), lambda i: (0, 0, i))
    out_shape = [tok(BF16), tok(F32), tok(F32), tok2(BF16), tok2(BF16), chan2, chan2,
                 jax.ShapeDtypeStruct((2, nt * SUBLANES, RW_W), F32)]
    out_specs = [spec1, spec1, spec1, spec2, spec2, spec2t, spec2t,
                 pl.BlockSpec((2, SUBLANES, RW_W), lambda i: (0, i, 0))]
    return pl.pallas_call(
        _rw_prep_body,
        grid=(nt,),
        in_specs=in_specs,
        out_specs=out_specs,
        out_shape=out_shape,
        compiler_params=_cparams(1),
        name="rw_prep",
    )(u, u, u, prm["taps"], prm["w0"], prm["a0"], prm["w2"], prm["a2"], prm["g2"], prm["kks"], prm["ka"],
      prm["rk"], e, et, tri, sel)


RW_PAIRS = RW_HEADS // 2
SCAN_G = 8
SC_T = RW_TM


def _bmm(a, b):
    return lax.dot_general(a, b, (((2,), (1,)), ((0,), (0,))), preferred_element_type=F32)


def _rw_scan_body(rt_ref, kkt_ref, v_ref, kbt_ref, bbt_ref, gam_ref, h0_ref, y_ref, hfin_ref, h_scr, *, reverse):
    c = pl.program_id(1)
    nc = pl.num_programs(1)

    @pl.when(c == 0)
    def _():
        h_scr[...] = h0_ref[...]

    gn = SCAN_G
    n = SC_T
    row = lax.broadcasted_iota(jnp.int32, (n, n), 0)
    col = lax.broadcasted_iota(jnp.int32, (n, n), 1)
    same = (row // CHUNK) == (col // CHUNK)
    earlier = (col > row) if reverse else (col < row)
    strict = same & earlier
    incl = same & (earlier | (col == row))
    eye = (row == col).astype(F32)
    level_masks = []
    b = 1
    while b < CHUNK:
        level_masks.append(((row // (2 * b)) == (col // (2 * b))) & ((row // b) != (col // b)))
        b *= 2
    head_a = (col < RW_HEAD).astype(BF16)
    head_b = (col >= RW_HEAD).astype(BF16)

    lanes = lambda g: slice(g * n, (g + 1) * n)
    rt_p = [rt_ref[0, :, lanes(g)] for g in range(gn)]
    kk_p = [kkt_ref[0, :, lanes(g)] for g in range(gn)]
    v_p = [v_ref[:, lanes(g)] for g in range(gn)]
    kbt = kbt_ref[0].reshape(gn, n, n)
    bbt = bbt_ref[0].reshape(gn, n, n)

    lhs = jnp.stack([jnp.concatenate([rt_p[g] * head_a, kk_p[g] * head_a, rt_p[g] * head_b, kk_p[g] * head_b],
                                     axis=0) for g in range(gn)])
    ak = _bmm(lhs, kbt)
    ab = _bmm(lhs, bbt)
    per_head = lambda x, i: jnp.concatenate([x[:, i * n:(i + 1) * n], x[:, (i + 2) * n:(i + 3) * n]], axis=0)
    ark = jnp.where(incl, per_head(ak, 0), 0.0)
    lk = jnp.where(strict, per_head(ak, 1), 0.0)
    arb = jnp.where(incl, per_head(ab, 0), 0.0)
    lb = jnp.where(strict, per_head(ab, 1), 0.0)

    x = eye - jnp.where(level_masks[0], lb, 0.0)
    for mask in level_masks[1:]:
        xb = x.astype(BF16)
        off = jnp.where(mask, lb, 0.0).astype(BF16)
        x = x - _bmm(xb, _bmm(off, xb).astype(BF16))
    tb = x.astype(BF16)

    kkm = jnp.stack([kk_p[g] * head_a for g in range(gn)] + [kk_p[g] * head_b for g in range(gn)])
    vm = jnp.stack([v_p[g] * head_a for g in range(gn)] + [v_p[g] * head_b for g in range(gn)])
    wk = _bmm(tb, kkm)
    u0 = _bmm(tb, _bmm(lk.astype(BF16), vm).astype(BF16))
    arbb = arb.astype(BF16)
    qw = _bmm(arbb, wk.astype(BF16))
    y0 = _bmm(jnp.concatenate([ark.astype(BF16), -arbb], axis=2),
              jnp.concatenate([vm, u0.astype(BF16)], axis=1))
    pair = lambda t: t[:gn] + t[gn:]
    wk_p = pair(wk).astype(BF16)
    u0_p = pair(u0).astype(BF16)
    qhat = (jnp.stack(rt_p).astype(F32) - pair(qw)).astype(BF16)
    y0_p = pair(y0)
    vu = jnp.concatenate([jnp.stack(v_p), u0_p], axis=1)

    h = h_scr[...]
    chunks = range(n // CHUNK)
    for j in (reversed(chunks) if reverse else chunks):
        in_chunk = ((col // CHUNK) == j).astype(BF16)
        kj = kbt * in_chunk
        bj = bbt * in_chunk
        hloc = jnp.where(same, _bmm(jnp.concatenate([kj, -bj], axis=2), vu), 0.0)
        wb = jnp.where(same, _bmm(bj, wk_p), 0.0).astype(BF16)
        gamma = jnp.stack([jnp.broadcast_to(gam_ref[0, j:j + 1, lanes(g)], (n, n)).T for g in range(gn)])
        hb = h.astype(BF16)
        rows = slice(j * CHUNK, (j + 1) * CHUNK)
        yj = _bmm(qhat[:, rows], hb) + y0_p[:, rows]
        for g in range(gn):
            y_ref[rows, lanes(g)] = yj[g]
        h = gamma * (h - _bmm(wb, hb) + hloc)
    h_scr[...] = h

    @pl.when(c == nc - 1)
    def _():
        hfin_ref[...] = h


def rw_scan(rt, kkt, v, kbar_t, bbar_t, gam, h0, d):
    t = v.shape[0]
    nsc = t // SC_T
    reverse = d == 1
    gw = SCAN_G * LANES
    tix = (lambda c: nsc - 1 - c) if reverse else (lambda c: c)
    tm_spec = pl.BlockSpec((1, SC_T, gw), lambda p, c: (d, tix(c), p))
    cm_spec = pl.BlockSpec((1, gw, SC_T), lambda p, c: (d, p, tix(c)))
    st_spec = pl.BlockSpec((SCAN_G, LANES, LANES), lambda p, c: (p, 0, 0))
    return pl.pallas_call(
        functools.partial(_rw_scan_body, reverse=reverse),
        grid=(RW_PAIRS // SCAN_G, nsc),
        in_specs=[tm_spec, tm_spec, pl.BlockSpec((SC_T, gw), lambda p, c: (tix(c), p)), cm_spec, cm_spec,
                  pl.BlockSpec((1, SUBLANES, gw), lambda p, c: (d, tix(c), p)), st_spec],
        out_specs=[pl.BlockSpec((SC_T, gw), lambda p, c: (tix(c), p)), st_spec],
        out_shape=[jax.ShapeDtypeStruct((t, RW_W), F32),
                   jax.ShapeDtypeStruct((RW_PAIRS, LANES, LANES), F32)],
        scratch_shapes=[pltpu.VMEM((SCAN_G, LANES, LANES), F32)],
        compiler_params=_cparams(2),
        name="rw_scan_rev" if reverse else "rw_scan_fwd",
    )(rt, kkt, v, kbar_t, bbar_t, gam, h0)


def _rw_post_body(yf_ref, yb_ref, bv_ref, g_ref, lg_ref, lb_ref, e_ref, et_ref, o_ref):
    y = yf_ref[...] + yb_ref[...]
    mu = _seg_sum_bcast(y, e_ref, et_ref) * (1.0 / RW_HEAD)
    dlt = y - mu
    var = _seg_sum_bcast(dlt * dlt, e_ref, et_ref) * (1.0 / RW_HEAD)
    yn = dlt * lax.rsqrt(var + GN_EPS)
    o_ref[...] = ((yn * lg_ref[...] + lb_ref[...] + bv_ref[...]) * g_ref[...]).astype(o_ref.dtype)


def rw_post(yf, yb, bv, g, lnx_g, lnx_b):
    t = yf.shape[0]
    tm = 256
    _, _, e, et = _rw_consts()
    tokspec = pl.BlockSpec((tm, RW_W), lambda i: (i, 0))
    row = pl.BlockSpec((1, RW_W), lambda i: (0, 0))
    return pl.pallas_call(
        _rw_post_body,
        grid=(t // tm,),
        in_specs=[tokspec, tokspec, tokspec, tokspec, row, row,
                  pl.BlockSpec((RW_W, LANES), lambda i: (0, 0)), pl.BlockSpec((LANES, RW_W), lambda i: (0, 0))],
        out_specs=tokspec,
        out_shape=jax.ShapeDtypeStruct((t, RW_W), BF16),
        compiler_params=_cparams(1),
        name="rw_post",
    )(yf, yb, bv, g, lnx_g, lnx_b, e, et)


def rwkv_group(u, prm, h0_f, h0_b, with_output):
    v, g, bv, rt, kkt, kbar_t, bbar_t, gam = rw_prep(u, prm)
    outs = []
    finals = []
    for d, h0 in ((0, h0_f), (1, h0_b)):
        y, hfin = rw_scan(rt, kkt, v, kbar_t, bbar_t, gam, h0, d)
        outs.append(y)
        finals.append(hfin)
    if not with_output:
        return None, finals[0], finals[1]
    out = rw_post(outs[0], outs[1], bv, g, prm["lnx_g"], prm["lnx_b"])
    return out, finals[0], finals[1]


NA_RQ = 4


def _na_body(q_ref, k_ref, v_ref, kc_ref, vc_ref, bt_ref, o_ref, *, n_rows):
    rb = pl.program_id(1)
    scale = NA_HEAD ** -0.5
    kc = kc_ref[...]
    vc = vc_ref[...]
    for qi in range(NA_RQ):
        r = rb * NA_RQ + qi
        start = jnp.clip(r - NA_KH // 2, 0, n_rows - NA_KH)
        dr0 = start - r + NA_KH - 1
        tok0 = pl.multiple_of(start * GRID_W, GRID_W)
        q = q_ref[qi * GRID_W:(qi + 1) * GRID_W, :]
        kw = k_ref[pl.ds(tok0, NA_KH * GRID_W), :]
        vw = v_ref[pl.ds(tok0, NA_KH * GRID_W), :]
        s = _dot_nt(q, kw) * scale
        bias = jnp.concatenate([bt_ref[0, dr0 + 2 * m] for m in range(NA_KH // 2)], axis=1)
        s = s + bias
        sc = _dot_nt(q, kc) * scale
        mx = jnp.maximum(jnp.max(s, axis=-1, keepdims=True), jnp.max(sc, axis=-1, keepdims=True))
        p = jnp.exp(s - mx)
        pc = jnp.exp(sc - mx)
        den = jnp.sum(p, axis=-1, keepdims=True) + jnp.sum(pc, axis=-1, keepdims=True)
        o = (_dot(p.astype(BF16), vw) + _dot(pc.astype(BF16), vc)) / den
        o_ref[qi * GRID_W:(qi + 1) * GRID_W, :] = o.astype(o_ref.dtype)


def _na_bias_table(rpb):
    cols = jnp.arange(GRID_W)
    col_start = jnp.clip(cols - NA_KW // 2, 0, GRID_W - NA_KW)
    col_ok = (cols[None, :] >= col_start[:, None]) & (cols[None, :] < col_start[:, None] + NA_KW)
    dc = jnp.clip(cols[None, :] - cols[:, None] + NA_KW - 1, 0, 2 * NA_KW - 2)
    bt = rpb.astype(F32)[:, :, dc]
    bt = jnp.where(col_ok[None, None], bt, NEG_INF)
    return jnp.concatenate([bt[:, :-1], bt[:, 1:]], axis=-1)


def na_latent(att_l, att_c, rpb):
    s_len = att_l.shape[0]
    n_rows = s_len // GRID_W
    qb = NA_W // NA_HEAD
    bt = _na_bias_table(rpb)
    return pl.pallas_call(
        functools.partial(_na_body, n_rows=n_rows),
        grid=(NA_HEADS, n_rows // NA_RQ),
        in_specs=[
            pl.BlockSpec((NA_RQ * GRID_W, NA_HEAD), lambda h, r: (r, h)),
            pl.BlockSpec((s_len, NA_HEAD), lambda h, r: (0, qb + h)),
            pl.BlockSpec((s_len, NA_HEAD), lambda h, r: (0, 2 * qb + h)),
            pl.BlockSpec((CTX_LEN, NA_HEAD), lambda h, r: (0, qb + h)),
            pl.BlockSpec((CTX_LEN, NA_HEAD), lambda h, r: (0, 2 * qb + h)),
            pl.BlockSpec((1, 2 * NA_KH - 2, GRID_W, 2 * GRID_W), lambda h, r: (h, 0, 0, 0)),
        ],
        out_specs=pl.BlockSpec((NA_RQ * GRID_W, NA_HEAD), lambda h, r: (r, h)),
        out_shape=jax.ShapeDtypeStruct((s_len, NA_W), BF16),
        compiler_params=_cparams(2),
        name="na_latent",
    )(att_l, att_l, att_l, att_c, att_c, bt)


def _na_ctx_body(q_ref, k_ref, v_ref, o_ref):
    s = _dot_nt(q_ref[...], k_ref[...]) * (NA_HEAD ** -0.5)
    mx = jnp.max(s, axis=-1, keepdims=True)
    p = jnp.exp(s - mx)
    den = jnp.sum(p, axis=-1, keepdims=True)
    o_ref[...] = (_dot(p.astype(BF16), v_ref[...]) / den).astype(o_ref.dtype)


def na_ctx(att_c):
    qb = NA_W // NA_HEAD
    blk = lambda off: pl.BlockSpec((CTX_LEN, NA_HEAD), lambda h: (0, off + h))
    return pl.pallas_call(
        _na_ctx_body,
        grid=(NA_HEADS,),
        in_specs=[blk(0), blk(qb), blk(2 * qb)],
        out_specs=blk(0),
        out_shape=jax.ShapeDtypeStruct((CTX_LEN, NA_W), BF16),
        compiler_params=_cparams(1),
        name="na_ctx",
    )(att_c, att_c, att_c)


def _rope_body(x_ref, c_ref, s1_ref, s2_ref, o_ref):
    x = x_ref[...].astype(F32)
    quarter = SW_HEAD // 4
    x_up = pltpu.roll(x, LANES - quarter, 1)
    x_dn = pltpu.roll(x, quarter, 1)
    o_ref[...] = (x * c_ref[...] + x_up * s1_ref[...] + x_dn * s2_ref[...]).astype(o_ref.dtype)


def _rope_tables(s_len):
    t = jnp.arange(s_len)
    row = (t // GRID_W).astype(F32)
    col = (t % GRID_W).astype(F32)
    half = SW_HEAD // 2
    inv = ROPE_BASE ** (-jnp.arange(0, half, 2, dtype=F32) / half)
    ang_r = row[:, None] * inv[None, :]
    ang_c = col[:, None] * inv[None, :]
    cos = jnp.concatenate([jnp.cos(ang_r)] * 2 + [jnp.cos(ang_c)] * 2, axis=-1)
    sin_r, sin_c = jnp.sin(ang_r), jnp.sin(ang_c)
    zero = jnp.zeros_like(sin_r)
    s_up = jnp.concatenate([-sin_r, zero, -sin_c, zero], axis=-1)
    s_dn = jnp.concatenate([zero, sin_r, zero, sin_c], axis=-1)
    tile2 = lambda a: jnp.concatenate([a, a], axis=-1)
    return tile2(cos), tile2(s_up), tile2(s_dn)


def rope_qk(att_l):
    s_len = att_l.shape[0]
    tm = 512
    nblk = (SW_W + SW_KV * SW_HEAD) // LANES
    off = NA_COLS // LANES
    cos, s_up, s_dn = _rope_tables(s_len)
    tab = pl.BlockSpec((tm, LANES), lambda i, j: (i, 0))
    return pl.pallas_call(
        _rope_body,
        grid=(s_len // tm, nblk),
        in_specs=[pl.BlockSpec((tm, LANES), lambda i, j: (i, off + j)), tab, tab, tab],
        out_specs=pl.BlockSpec((tm, LANES), lambda i, j: (i, j)),
        out_shape=jax.ShapeDtypeStruct((s_len, nblk * LANES), BF16),
        compiler_params=_cparams(2),
        name="rope_qk",
    )(att_l, cos, s_up, s_dn)


def _sw_softmax_out(q8, s_parts, v_parts, sink_col):
    mx = sink_col
    for s in s_parts:
        mx = jnp.maximum(mx, jnp.max(s, axis=-1, keepdims=True))
    den = jnp.exp(sink_col - mx)
    o = None
    for s, v in zip(s_parts, v_parts):
        p = jnp.exp(s - mx)
        den = den + jnp.sum(p, axis=-1, keepdims=True)
        pv = _dot(p.astype(BF16), v)
        o = pv if o is None else o + pv
    return o / den


def _sink_column(sink_ref, kh, rows_per_head):
    n = SW_GROUP * rows_per_head
    grp = lax.broadcasted_iota(jnp.int32, (n, 1), 0) // rows_per_head
    col = jnp.zeros((n, 1), F32)
    for g in range(SW_GROUP):
        col = jnp.where(grp == g, sink_ref[kh * SW_GROUP + g], col)
    return col


def _sw_body(sink_ref, q_ref, k_ref, v_ref, kc_ref, vc_ref, o_ref, *, s_len):
    n = pl.program_id(0)
    scale = SW_HEAD ** -0.5
    win = 3 * SW_BLOCK
    start = pl.multiple_of(jnp.clip((n - 1) * SW_BLOCK, 0, s_len - win), SW_BLOCK)
    k2 = k_ref[pl.ds(start, win), :]
    v2 = v_ref[pl.ds(start, win), :]
    qpos = n * SW_BLOCK + lax.broadcasted_iota(jnp.int32, (SW_BLOCK, win), 0)
    kpos = start + lax.broadcasted_iota(jnp.int32, (SW_BLOCK, win), 1)
    ok = jnp.abs(kpos - qpos) <= SW_WIN
    for kh in range(SW_KV):
        lanes = slice(kh * SW_HEAD, (kh + 1) * SW_HEAD)
        q8 = jnp.concatenate(
            [q_ref[:, (kh * SW_GROUP + g) * SW_HEAD:(kh * SW_GROUP + g + 1) * SW_HEAD] for g in range(SW_GROUP)],
            axis=0)
        s_loc = _dot_nt(q8, k2[:, lanes]) * scale
        s_loc = jnp.where(ok[None], s_loc.reshape(SW_GROUP, SW_BLOCK, win), NEG_INF).reshape(
            SW_GROUP * SW_BLOCK, win)
        s_ctx = _dot_nt(q8, kc_ref[:, lanes]) * scale
        o = _sw_softmax_out(q8, [s_loc, s_ctx], [v2[:, lanes], vc_ref[:, lanes]],
                            _sink_column(sink_ref, kh, SW_BLOCK))
        for g in range(SW_GROUP):
            c0 = (kh * SW_GROUP + g) * SW_HEAD
            o_ref[:, c0:c0 + SW_HEAD] = o[g * SW_BLOCK:(g + 1) * SW_BLOCK].astype(o_ref.dtype)


def swa_latent(qk_rot, att_l, att_c, sink):
    s_len = qk_rot.shape[0]
    kcol = SW_W // LANES
    ck = (NA_COLS + SW_W) // LANES
    return pl.pallas_call(
        functools.partial(_sw_body, s_len=s_len),
        grid=(s_len // SW_BLOCK,),
        in_specs=[
            pl.BlockSpec(memory_space=pltpu.SMEM),
            pl.BlockSpec((SW_BLOCK, SW_W), lambda n: (n, 0)),
            pl.BlockSpec((s_len, LANES), lambda n: (0, kcol)),
            pl.BlockSpec((s_len, LANES), lambda n: (0, ck + 1)),
            pl.BlockSpec((CTX_LEN, LANES), lambda n: (0, ck)),
            pl.BlockSpec((CTX_LEN, LANES), lambda n: (0, ck + 1)),
        ],
        out_specs=pl.BlockSpec((SW_BLOCK, SW_W), lambda n: (n, 0)),
        out_shape=jax.ShapeDtypeStruct((s_len, SW_W), BF16),
        compiler_params=_cparams(1),
        name="swa_latent",
    )(sink, qk_rot, qk_rot, att_l, att_c, att_c)


def _sw_ctx_body(sink_ref, q_ref, k_ref, v_ref, o_ref):
    scale = SW_HEAD ** -0.5
    for kh in range(SW_KV):
        lanes = slice(kh * SW_HEAD, (kh + 1) * SW_HEAD)
        q8 = jnp.concatenate(
            [q_ref[:, (kh * SW_GROUP + g) * SW_HEAD:(kh * SW_GROUP + g + 1) * SW_HEAD] for g in range(SW_GROUP)],
            axis=0)
        s = _dot_nt(q8, k_ref[:, lanes]) * scale
        o = _sw_softmax_out(q8, [s], [v_ref[:, lanes]], _sink_column(sink_ref, kh, CTX_LEN))
        for g in range(SW_GROUP):
            c0 = (kh * SW_GROUP + g) * SW_HEAD
            o_ref[:, c0:c0 + SW_HEAD] = o[g * CTX_LEN:(g + 1) * CTX_LEN].astype(o_ref.dtype)


def swa_ctx(att_c, sink):
    qo = NA_COLS // SW_W
    ck = (NA_COLS + SW_W) // LANES
    return pl.pallas_call(
        _sw_ctx_body,
        grid=(1,),
        in_specs=[
            pl.BlockSpec(memory_space=pltpu.SMEM),
            pl.BlockSpec((CTX_LEN, SW_W), lambda n: (0, qo)),
            pl.BlockSpec((CTX_LEN, LANES), lambda n: (0, ck)),
            pl.BlockSpec((CTX_LEN, LANES), lambda n: (0, ck + 1)),
        ],
        out_specs=pl.BlockSpec((CTX_LEN, SW_W), lambda n: (0, 0)),
        out_shape=jax.ShapeDtypeStruct((CTX_LEN, SW_W), BF16),
        compiler_params=_cparams(1),
        name="swa_ctx",
    )(sink, att_c, att_c, att_c)


def _router_body(x_ref, g_ref, sc_ref, sh_ref, wr_ref, rb_ref, h_ref, pick_ref, hp_ref):
    x = x_ref[...]
    tm = x.shape[0]
    ms = jnp.mean(x * x, axis=-1, keepdims=True)
    h = (x * lax.rsqrt(ms + NORM_EPS) * g_ref[...]) * (1.0 + sc_ref[...]) + sh_ref[...]
    h_ref[...] = h.astype(h_ref.dtype)

    hh, hl = _split2(h)
    wh, wl = _split2(wr_ref[...])
    logits = _dot(hh, wh) + _dot(hl, wh) + _dot(hh, wl)
    lane = lax.broadcasted_iota(jnp.int32, (tm, LANES), 1)
    lane_f = lane.astype(F32)
    valid = lane < N_EXPERTS
    neg = -jnp.inf
    big = float(2 * LANES)
    scores = _sigmoid(logits)
    biased = jnp.where(valid, scores + rb_ref[...], neg)
    grp = lane // EXPERTS_PER_GROUP

    def first_argmax(vals):
        m = jnp.max(vals, axis=-1, keepdims=True)
        idx = jnp.min(jnp.where(vals == m, lane_f, big), axis=-1, keepdims=True)
        return m, idx

    gscore = jnp.full((tm, LANES), neg, F32)
    for gi in range(N_GROUPS):
        vg = jnp.where(grp == gi, biased, neg)
        m1, i1 = first_argmax(vg)
        m2 = jnp.max(jnp.where(lane_f == i1, neg, vg), axis=-1, keepdims=True)
        gscore = jnp.where(lane == gi * EXPERTS_PER_GROUP, m1 + m2, gscore)
    keep = jnp.zeros((tm, LANES), jnp.bool_)
    for _ in range(TOPK_GROUPS):
        _, gi = first_argmax(gscore)
        keep = keep | (grp.astype(F32) == jnp.floor(gi * (1.0 / EXPERTS_PER_GROUP)))
        gscore = jnp.where(lane_f == gi, neg, gscore)
    cur = jnp.where(valid, jnp.where(keep, biased, NEG_INF), neg)
    sel = jnp.zeros((tm, LANES), jnp.bool_)
    picks = jnp.zeros((tm, LANES), F32)
    for k in range(TOP_K):
        _, ei = first_argmax(cur)
        hit = lane_f == ei
        sel = sel | hit
        cur = jnp.where(hit, neg, cur)
        picks = jnp.where(lane == k, ei, picks)
        picks = jnp.where(lane == TOP_K + k, jnp.sum(jnp.where(hit, scores, 0.0), axis=-1, keepdims=True), picks)
    w = jnp.where(sel, scores, 0.0)
    norm = ROUTE_SCALE / jnp.sum(w, axis=-1, keepdims=True)
    pick_ref[...] = jnp.where((lane >= TOP_K) & (lane < 2 * TOP_K), picks * norm, picks)

    bits = pltpu.bitcast(h.astype(BF16).astype(F32), jnp.uint32)
    half = D_MODEL // 2
    hp_ref[...] = (bits[:, half:] & jnp.uint32(0xFFFF0000)) | (bits[:, :half] >> 16)


def router(x, g, sc, sh, w_router_pad, r_bias_pad):
    t = x.shape[0]
    tm = 256
    row = pl.BlockSpec((1, D_MODEL), lambda i: (0, 0))
    return pl.pallas_call(
        _router_body,
        grid=(t // tm,),
        in_specs=[pl.BlockSpec((tm, D_MODEL), lambda i: (i, 0)), row, row, row,
                  pl.BlockSpec((D_MODEL, LANES), lambda i: (0, 0)), pl.BlockSpec((1, LANES), lambda i: (0, 0))],
        out_specs=[pl.BlockSpec((tm, D_MODEL), lambda i: (i, 0)), pl.BlockSpec((tm, LANES), lambda i: (i, 0)),
                   pl.BlockSpec((tm, D_MODEL // 2), lambda i: (i, 0))],
        out_shape=[jax.ShapeDtypeStruct((t, D_MODEL), BF16), jax.ShapeDtypeStruct((t, LANES), F32),
                   jax.ShapeDtypeStruct((t, D_MODEL // 2), jnp.uint32)],
        compiler_params=_cparams(1),
        name="router",
    )(x, g, sc, sh, w_router_pad, r_bias_pad)


MOE_TM = 512
PACK_W = D_MODEL // 2
MOE_DOWN_COLS = 1024


def _unpack_bf16_pairs(words):
    lo = pltpu.bitcast(words << 16, F32)
    hi = pltpu.bitcast(words & jnp.uint32(0xFFFF0000), F32)
    return lo, hi


def _pack_bf16_pairs(lo, hi):
    lo_bits = pltpu.bitcast(lo.astype(BF16).astype(F32), jnp.uint32) >> 16
    hi_bits = pltpu.bitcast(hi.astype(BF16).astype(F32), jnp.uint32) & jnp.uint32(0xFFFF0000)
    return hi_bits | lo_bits


def _routed_body(te_ref, nv_ref, src_ref, dst_ref, hp_hbm, wg_ref, wu_ref, wd_ref, yk_hbm, xbuf, ybuf, gsem, ssem,
                 *, n_real_rows):
    s = pl.program_id(0)
    tm = MOE_TM
    nv = nv_ref[0]

    def gather_wait(slot):
        pltpu.make_async_copy(hp_hbm.at[pl.ds(0, tm)], xbuf.at[slot], gsem.at[slot]).wait()

    def scatter_wait(slot):
        pltpu.make_async_copy(ybuf.at[slot], yk_hbm.at[pl.ds(0, tm)], ssem.at[slot]).wait()

    @pl.when(s == 0)
    def _():
        ybuf[1] = jnp.zeros((tm, PACK_W), jnp.uint32)
        fill = pltpu.make_async_copy(ybuf.at[1], yk_hbm.at[pl.ds(n_real_rows, tm)], ssem.at[1])
        fill.start()
        fill.wait()

    @pl.when(s < nv)
    def _():
        slot = s % 2
        for r in range(tm):
            pltpu.make_async_copy(hp_hbm.at[pl.ds(src_ref[0, 0, r], 1)], xbuf.at[slot, pl.ds(r, 1)],
                                  gsem.at[slot]).start()

    @pl.when((s >= 1) & (s <= nv))
    def _():
        t = s - 1
        slot = t % 2
        gather_wait(slot)
        x_lo, x_hi = _unpack_bf16_pairs(xbuf[slot])
        x_lo = x_lo.astype(BF16)
        x_hi = x_hi.astype(BF16)
        hg = _dot(x_lo, wg_ref[0, 0, :PACK_W].astype(BF16)) + _dot(x_hi, wg_ref[0, 0, PACK_W:].astype(BF16))
        hu = _dot(x_lo, wu_ref[0, 0, :PACK_W].astype(BF16)) + _dot(x_hi, wu_ref[0, 0, PACK_W:].astype(BF16))
        act = ((hg * _sigmoid(hg)) * hu).astype(BF16)
        for c0 in range(0, PACK_W, MOE_DOWN_COLS):
            y_lo = _dot(act, wd_ref[0, 0, :, c0:c0 + MOE_DOWN_COLS].astype(BF16))
            y_hi = _dot(act, wd_ref[0, 0, :, PACK_W + c0:PACK_W + c0 + MOE_DOWN_COLS].astype(BF16))
            ybuf[slot, :, c0:c0 + MOE_DOWN_COLS] = _pack_bf16_pairs(y_lo, y_hi)

        @pl.when(t >= 1)
        def _():
            scatter_wait(1 - slot)

        for r in range(tm):
            pltpu.make_async_copy(ybuf.at[slot, pl.ds(r, 1)], yk_hbm.at[pl.ds(dst_ref[0, 0, r], 1)],
                                  ssem.at[slot]).start()

        @pl.when(s == nv)
        def _():
            scatter_wait(slot)


def _dispatch_plan(picks, t):
    tm = MOE_TM
    n_pairs = TOP_K * t
    n_tiles = n_pairs // tm + N_EXPERTS
    e_flat = picks[:, :TOP_K].astype(jnp.int32).reshape(-1)
    pair = jnp.arange(n_pairs, dtype=jnp.int32)
    tok = pair // TOP_K
    dst = (pair % TOP_K) * t + tok
    experts = jnp.arange(N_EXPERTS, dtype=jnp.int32)
    counts = jnp.sum((e_flat[:, None] == experts[None, :]).astype(jnp.int32), axis=0)
    fill = (-counts) % tm
    filler_e = jnp.repeat(experts, tm)
    filler_j = jnp.tile(jnp.arange(tm, dtype=jnp.int32), N_EXPERTS)
    filler_key = jnp.where(filler_j < jnp.repeat(fill, tm), 2 * filler_e + 1, 2 * N_EXPERTS)
    keys = jnp.concatenate([2 * e_flat, filler_key])
    src_all = jnp.concatenate([tok, jnp.zeros_like(filler_j)])
    dst_all = jnp.concatenate([dst, n_pairs + filler_j])
    _, src_sorted, dst_sorted = lax.sort((keys, src_all, dst_all), num_keys=1, is_stable=True)
    tile_end = jnp.cumsum((counts + fill) // tm)
    n_valid = tile_end[-1].astype(jnp.int32)
    ti = jnp.minimum(jnp.arange(n_tiles, dtype=jnp.int32), n_valid - 1)
    tile_e = jnp.sum((tile_end[None, :] <= ti[:, None]).astype(jnp.int32), axis=1)
    return (tile_e.astype(jnp.int32), n_valid.reshape(1), src_sorted.reshape(n_tiles, 1, tm),
            dst_sorted.reshape(n_tiles, 1, tm))


def moe_routed(hp, picks, layer, we_gate, we_up, we_down):
    t = hp.shape[0]
    tm = MOE_TM
    tile_e, n_valid, src, dst = _dispatch_plan(picks, t)
    n_tiles = tile_e.shape[0]
    last = n_tiles - 1
    cur = lambda s: jnp.minimum(s, last)
    prev = lambda s: jnp.clip(s - 1, 0, last)
    smem = lambda f: pl.BlockSpec((1, 1, tm), lambda s, te, nv: (f(s), 0, 0), memory_space=pltpu.SMEM)
    wspec = lambda shp: pl.BlockSpec((1, 1) + shp, lambda s, te, nv: (layer, te[prev(s)], 0, 0))
    grid_spec = pltpu.PrefetchScalarGridSpec(
        num_scalar_prefetch=2,
        grid=(n_tiles + 1,),
        in_specs=[
            smem(cur), smem(prev),
            pl.BlockSpec(memory_space=pl.ANY),
            wspec((D_MODEL, D_EXPERT)), wspec((D_MODEL, D_EXPERT)), wspec((D_EXPERT, D_MODEL)),
        ],
        out_specs=pl.BlockSpec(memory_space=pl.ANY),
        scratch_shapes=[pltpu.VMEM((2, tm, PACK_W), jnp.uint32), pltpu.VMEM((2, tm, PACK_W), jnp.uint32),
                        pltpu.SemaphoreType.DMA((2,)), pltpu.SemaphoreType.DMA((2,))],
    )
    return pl.pallas_call(
        functools.partial(_routed_body, n_real_rows=TOP_K * t),
        grid_spec=grid_spec,
        out_shape=jax.ShapeDtypeStruct((TOP_K * t + tm, PACK_W), jnp.uint32),
        compiler_params=_cparams(1),
        name="moe_routed",
    )(tile_e, n_valid, src, dst, hp, we_gate, we_up, we_down)


COMBINE_TM = 128


def _combine_body(h_ref, pick_ref, *rest):
    yk_refs = rest[:TOP_K]
    wg_ref, wu_ref, wd_ref, x_ref, ga_ref, o_ref = rest[TOP_K:]
    h = h_ref[...]
    hg = _dot(h, wg_ref[...])
    hu = _dot(h, wu_ref[...])
    act = ((hg * _sigmoid(hg)) * hu).astype(BF16)
    picks = pick_ref[...]
    for half in range(2):
        cols = slice(half * PACK_W, (half + 1) * PACK_W)
        acc = _dot(act, wd_ref[:, cols])
        for k in range(TOP_K):
            acc = acc + picks[:, TOP_K + k:TOP_K + k + 1] * _unpack_bf16_pairs(yk_refs[k][...])[half]
        o_ref[:, cols] = x_ref[:, cols] + ga_ref[:, cols] * acc


def moe_combine(h, picks, yk, ws_gate, ws_up, ws_down, x, ga):
    t = h.shape[0]
    tm = COMBINE_TM
    nb = t // tm
    tok = lambda cols: pl.BlockSpec((tm, cols), lambda i: (i, 0))
    full = lambda a: pl.BlockSpec(a.shape, lambda i: (0,) * a.ndim)
    slot = lambda k: pl.BlockSpec((tm, PACK_W), lambda i: (k * nb + i, 0))
    return pl.pallas_call(
        _combine_body,
        grid=(nb,),
        in_specs=[tok(D_MODEL), tok(LANES)] + [slot(k) for k in range(TOP_K)] + [
            full(ws_gate), full(ws_up), full(ws_down), tok(D_MODEL), pl.BlockSpec((1, D_MODEL), lambda i: (0, 0))],
        out_specs=tok(D_MODEL),
        out_shape=jax.ShapeDtypeStruct((t, D_MODEL), F32),
        compiler_params=_cparams(1),
        name="moe_combine",
    )(h, picks, *([yk] * TOP_K), ws_gate, ws_up, ws_down, x, ga)


def _pad_cols(a, n):
    return jnp.pad(a, [(0, 0)] * (a.ndim - 1) + [(0, n - a.shape[-1])])


def _rw_params(i, rw_conv, rw_w0, rw_w2, rw_a0, rw_a2, rw_g2, rw_kk, rw_ka, rw_rk, rw_lnx_g, rw_lnx_b):
    w2 = jnp.zeros((2, LORA_ALL, RW_W), F32)
    a2 = jnp.zeros((2, LORA_ALL, RW_W), F32)
    for d in range(2):
        w2 = w2.at[d, d * LORA_W:(d + 1) * LORA_W].set(rw_w2[i, d])
        a2 = a2.at[d, 2 * LORA_W + d * LORA_A:2 * LORA_W + (d + 1) * LORA_A].set(rw_a2[i, d])
    taps = jnp.pad(rw_conv[i], ((0, SUBLANES - 3), (0, RW_COLS_PAD - RW_COLS)))
    return dict(
        taps=taps, w0=rw_w0[i], a0=rw_a0[i], w2=w2.astype(BF16), a2=a2.astype(BF16), g2=rw_g2[i].astype(BF16),
        kks=rw_kk[i][None], ka=rw_ka[i][None], rk=rw_rk[i].reshape(1, RW_W),
        lnx_g=rw_lnx_g[i][None], lnx_b=rw_lnx_b[i][None])


def kernel(x, c, ctx, c_ctx, w_ada, b_ada, norm1_g, norm2_g, w_in, rw_conv, rw_w0, rw_w2, rw_a0, rw_a2, rw_g2,
           rw_kk, rw_ka, rw_rk, rw_lnx_g, rw_lnx_b, na_rpb, sw_sink, w_out, w_router, router_bias, we_gate,
           we_up, we_down, ws_gate, ws_up, ws_down, final_g):
    assert x.shape[0] == 1 and x.shape[2] == D_MODEL and ctx.shape[1] == CTX_LEN
    xl = x[0]
    xc = ctx[0]
    cond8 = jnp.zeros((SUBLANES, D_MODEL), F32).at[0].set(c[0]).at[1].set(c_ctx)
    mods = ada_mod(cond8, w_ada, b_ada)

    for i in range(DEPTH):
        ctx_needed = i < DEPTH - 1
        mod_l = [m[None] for m in mods[i, 0].reshape(6, D_MODEL)]
        mod_c = [m[None] for m in mods[i, 1].reshape(6, D_MODEL)]
        w_rw = _pad_cols(w_in[i][:, :RW_COLS], RW_COLS_PAD).astype(BF16)
        w_att = w_in[i][:, RW_COLS:].astype(BF16)
        w_o = w_out[i].astype(BF16)
        w_o_parts = [w_o[:RW_W], w_o[RW_W:RW_W + NA_W], w_o[RW_W + NA_W:]]
        rw_prm = _rw_params(i, rw_conv, rw_w0, rw_w2, rw_a0, rw_a2, rw_g2, rw_kk, rw_ka, rw_rk, rw_lnx_g,
                            rw_lnx_b)
        n1 = norm1_g[i][None]
        n2 = norm2_g[i][None]

        hl = norm_mod(xl, n1, mod_l[1], mod_l[0])
        hc = norm_mod(xc, n1, mod_c[1], mod_c[0])
        url = matmul([hl], [w_rw], 768, F32, name="in_proj_rw")
        urc = matmul([hc], [w_rw], 768, F32, name="in_proj_rw_ctx")
        ual = matmul([hl], [w_att], 256, BF16, name="in_proj_att")
        uac = matmul([hc], [w_att], 256, BF16, name="in_proj_att_ctx")

        z = jnp.zeros((RW_PAIRS, LANES, LANES), F32)
        rwc, s_f, s_b = rwkv_group(urc, rw_prm, z, z, ctx_needed)
        rwl, _, _ = rwkv_group(url, rw_prm, s_f, s_b, True)
        nal = na_latent(ual, uac, na_rpb[i])
        swl = swa_latent(rope_qk(ual), ual, uac, sw_sink[i])
        xl = matmul([rwl, nal, swl], w_o_parts, 512, F32, residual=(xl, mod_l[2]), name="out_proj")

        wr = _pad_cols(w_router[i], LANES)
        rb = _pad_cols(router_bias[i][None], LANES)
        shared = (ws_gate[i].astype(BF16), ws_up[i].astype(BF16), ws_down[i].astype(BF16))

        def moe(xt, mod):
            h2, picks, hp = router(xt, n2, mod[4], mod[3], wr, rb)
            yk = moe_routed(hp, picks, i, we_gate, we_up, we_down)
            return moe_combine(h2, picks, yk, *shared, xt, mod[5])

        xl = moe(xl, mod_l)
        if ctx_needed:
            nac = na_ctx(uac)
            swc = swa_ctx(uac, sw_sink[i])
            xc = matmul([rwc, nac, swc], w_o_parts, 512, F32, residual=(xc, mod_c[2]), name="out_proj_ctx")
            xc = moe(xc, mod_c)
    return final_norm(xl, final_g[None])[None]
```

```python
import functools

import jax
import jax.numpy as jnp
from jax import lax
from jax.experimental import pallas as pl
from jax.experimental.pallas import tpu as pltpu

F32 = jnp.float32
BF16 = jnp.bfloat16

D_MODEL = 4096
DEPTH = 2
GRID_W = 64
CTX_LEN = 256
NORM_EPS = 1e-6
NEG_INF = -1e30

RW_HEAD = 64
RW_W = D_MODEL // 2
RW_HEADS = RW_W // RW_HEAD
LORA_W = 96
LORA_A = 96
LORA_G = 256
GN_EPS = 64e-5
RW_COLS = 3 * RW_W + LORA_G + 2 * LORA_W + 2 * LORA_A
LORA_OFF = 3 * RW_W + LORA_G
LORA_ALL = 2 * LORA_W + 2 * LORA_A

NA_HEAD = 128
NA_W = D_MODEL // 4
NA_HEADS = NA_W // NA_HEAD
NA_KH = 8
NA_KW = 16
NA_COLS = 3 * NA_W

SW_HEAD = 64
SW_W = D_MODEL - RW_W - NA_W
SW_HEADS = SW_W // SW_HEAD
SW_KV = SW_HEADS // 8
SW_GROUP = SW_HEADS // SW_KV
SW_WIN = 128
SW_BLOCK = 128
ROPE_BASE = 10000.0
SW_COLS = SW_W + 2 * SW_KV * SW_HEAD
ATT_COLS = NA_COLS + SW_COLS

N_EXPERTS = 64
N_GROUPS = 8
EXPERTS_PER_GROUP = N_EXPERTS // N_GROUPS
TOPK_GROUPS = 4
TOP_K = 8
D_EXPERT = 256
ROUTE_SCALE = 2.5

LANES = 128
SUBLANES = 8
RW_COLS_PAD = 6912
CHUNK = 64
VMEM_LIMIT = 56 * 1024 * 1024


def _cparams(n_axes, vmem=VMEM_LIMIT):
    return pltpu.CompilerParams(dimension_semantics=("arbitrary",) * n_axes, vmem_limit_bytes=vmem)


def _dot(a, b):
    return jnp.dot(a, b, preferred_element_type=F32)


def _dot_nt(a, b):
    return lax.dot_general(a, b, (((1,), (1,)), ((), ())), preferred_element_type=F32)


def _split2(x):
    hi = x.astype(BF16)
    lo = (x - hi.astype(F32)).astype(BF16)
    return hi, lo


def _split3(x):
    hi = x.astype(BF16)
    r1 = x - hi.astype(F32)
    mid = r1.astype(BF16)
    lo = (r1 - mid.astype(F32)).astype(BF16)
    return hi, mid, lo


def _sigmoid(x):
    return 1.0 / (1.0 + jnp.exp(-x))


def _ada_body(s_ref, w_ref, b_ref, o_ref):
    c = s_ref[...]
    s = (c * _sigmoid(c)).astype(BF16)
    o_ref[0] = _dot(s, w_ref[0].astype(BF16)) + b_ref[0]


def ada_mod(cond8, w_ada, b_ada):
    bn = 512
    n = 6 * D_MODEL
    return pl.pallas_call(
        _ada_body,
        grid=(DEPTH, n // bn),
        in_specs=[
            pl.BlockSpec((SUBLANES, D_MODEL), lambda l, j: (0, 0)),
            pl.BlockSpec((1, D_MODEL, bn), lambda l, j: (l, 0, j)),
            pl.BlockSpec((1, 1, bn), lambda l, j: (l, 0, j)),
        ],
        out_specs=pl.BlockSpec((1, SUBLANES, bn), lambda l, j: (l, 0, j)),
        out_shape=jax.ShapeDtypeStruct((DEPTH, SUBLANES, n), F32),
        compiler_params=_cparams(2),
        name="ada_mod",
    )(cond8, w_ada, b_ada.reshape(DEPTH, 1, n))


def _norm_mod_body(x_ref, g_ref, sc_ref, sh_ref, o_ref):
    x = x_ref[...]
    ms = jnp.mean(x * x, axis=-1, keepdims=True)
    y = x * lax.rsqrt(ms + NORM_EPS) * g_ref[...]
    o_ref[...] = (y * (1.0 + sc_ref[...]) + sh_ref[...]).astype(o_ref.dtype)


def norm_mod(x, g, sc, sh, out_dtype=BF16):
    t = x.shape[0]
    tm = 256
    row = pl.BlockSpec((1, D_MODEL), lambda i: (0, 0))
    return pl.pallas_call(
        _norm_mod_body,
        grid=(t // tm,),
        in_specs=[pl.BlockSpec((tm, D_MODEL), lambda i: (i, 0)), row, row, row],
        out_specs=pl.BlockSpec((tm, D_MODEL), lambda i: (i, 0)),
        out_shape=jax.ShapeDtypeStruct((t, D_MODEL), out_dtype),
        compiler_params=_cparams(1),
        name="norm_mod",
    )(x, g, sc, sh)


def _final_norm_body(x_ref, g_ref, o_ref):
    x = x_ref[...]
    ms = jnp.mean(x * x, axis=-1, keepdims=True)
    o_ref[...] = x * lax.rsqrt(ms + NORM_EPS) * g_ref[...]


def final_norm(x, g):
    t = x.shape[0]
    tm = 256
    return pl.pallas_call(
        _final_norm_body,
        grid=(t // tm,),
        in_specs=[pl.BlockSpec((tm, D_MODEL), lambda i: (i, 0)), pl.BlockSpec((1, D_MODEL), lambda i: (0, 0))],
        out_specs=pl.BlockSpec((tm, D_MODEL), lambda i: (i, 0)),
        out_shape=jax.ShapeDtypeStruct((t, D_MODEL), F32),
        compiler_params=_cparams(1),
        name="final_norm",
    )(x, g)


def _mm_body(*refs, n_in, residual):
    a_refs = refs[:n_in]
    w_refs = refs[n_in:2 * n_in]
    rest = refs[2 * n_in:]
    acc = _dot(a_refs[0][...], w_refs[0][...])
    for a_ref, w_ref in zip(a_refs[1:], w_refs[1:]):
        acc = acc + _dot(a_ref[...], w_ref[...])
    if residual:
        x_ref, ga_ref, o_ref = rest
        o_ref[...] = x_ref[...] + ga_ref[...] * acc
    else:
        (o_ref,) = rest
        o_ref[...] = acc.astype(o_ref.dtype)


def matmul(a_list, w_list, bn, out_dtype, residual=None, name="matmul"):
    m = a_list[0].shape[0]
    n = w_list[0].shape[1]
    bm = 1024 if m % 1024 == 0 else 256
    in_specs = [pl.BlockSpec((bm, a.shape[1]), lambda i, j: (i, 0)) for a in a_list]
    in_specs += [pl.BlockSpec((w.shape[0], bn), lambda i, j: (0, j)) for w in w_list]
    args = list(a_list) + list(w_list)
    if residual is not None:
        in_specs += [pl.BlockSpec((bm, bn), lambda i, j: (i, j)), pl.BlockSpec((1, bn), lambda i, j: (0, j))]
        args += list(residual)
    return pl.pallas_call(
        functools.partial(_mm_body, n_in=len(a_list), residual=residual is not None),
        grid=(m // bm, n // bn),
        in_specs=in_specs,
        out_specs=pl.BlockSpec((bm, bn), lambda i, j: (i, j)),
        out_shape=jax.ShapeDtypeStruct((m, n), out_dtype),
        compiler_params=_cparams(2),
        name=name,
    )(*args)


RW_TM = 128


def _seg_sum_bcast(x, e_ref, et_ref):
    hi, lo = _split2(x)
    s = _dot(hi, e_ref[...]) + _dot(lo, e_ref[...])
    shi, slo = _split2(s)
    return _dot(shi, et_ref[...]) + _dot(slo, et_ref[...])


def _rw_prep_body(u_ref, up_ref, un_ref, taps_ref, w0_ref, a0_ref, w2_ref, a2_ref, g2_ref, kks_ref, ka_ref,
                  rk_ref, e_ref, et_ref, tri_ref, sel_ref,
                  v_ref, g_ref, bv_ref, rt_ref, kkt_ref, kbar_ref, bbar_ref, gam_ref):
    i = pl.program_id(0)
    nt = pl.num_programs(0)
    tm = RW_TM
    rows = lax.broadcasted_iota(jnp.int32, (tm, 1), 0)
    has_prev = (i > 0).astype(F32)
    has_next = (i < nt - 1).astype(F32)

    def conv(c0, c1):
        u = u_ref[:, c0:c1]
        prev_row = up_ref[SUBLANES - 1:SUBLANES, c0:c1] * has_prev
        next_row = un_ref[0:1, c0:c1] * has_next
        u_prev = jnp.where(rows == 0, prev_row, pltpu.roll(u, 1, 0))
        u_next = jnp.where(rows == tm - 1, next_row, pltpu.roll(u, tm - 1, 0))
        return (u_prev * taps_ref[0:1, c0:c1] + u * taps_ref[1:2, c0:c1] + u_next * taps_ref[2:3, c0:c1])

    r = conv(0, RW_W)
    k = conv(RW_W, 2 * RW_W)
    v = conv(2 * RW_W, 3 * RW_W)
    gd = conv(3 * RW_W, LORA_OFF)
    lo_in = conv(LORA_OFF, LORA_OFF + LORA_ALL)

    v_ref[...] = v.astype(v_ref.dtype)
    g_ref[...] = _dot(_sigmoid(gd).astype(BF16), g2_ref[...])

    kk = k * kks_ref[...]
    ssq = _seg_sum_bcast(kk * kk, e_ref, et_ref)
    kkn = kk / jnp.maximum(jnp.sqrt(ssq), 1e-12)

    tanh_lo = jnp.tanh(lo_in).astype(BF16)
    raw_lo = lo_in.astype(BF16)
    bonus = jnp.zeros((tm, RW_W), F32)
    for d in range(2):
        w_pre = w0_ref[d:d + 1, :] + _dot(tanh_lo, w2_ref[d])
        z = -w_pre
        softplus = jnp.maximum(z, 0.0) + jnp.log(1.0 + jnp.exp(-jnp.abs(z)))
        w_log = -softplus - 0.5
        lw = -jnp.exp(w_log)
        a = _sigmoid(a0_ref[d:d + 1, :] + _dot(raw_lo, a2_ref[d]))
        kd = k * (1.0 + (a - 1.0) * ka_ref[...])
        b = kkn * a
        bonus = bonus + r * kd * rk_ref[...]
        l1, l2, l3 = _split3(lw)
        tri = tri_ref[d]
        sel = sel_ref[...]
        cum = _dot(tri, l1) + _dot(tri, l2) + _dot(tri, l3)
        tot = _dot(sel, l1) + _dot(sel, l2) + _dot(sel, l3)
        e_inv = jnp.exp(-cum)
        rt_ref[d] = (r * jnp.exp(cum)).astype(BF16)
        kkt_ref[d] = (kkn * jnp.exp(cum - lw)).astype(BF16)
        kbar_ref[d] = (kd * e_inv).T.astype(BF16)
        bbar_ref[d] = (b * e_inv).T.astype(BF16)
        gam_ref[d] = jnp.exp(tot)
    bv_ref[...] = _seg_sum_bcast(bonus, e_ref, et_ref) * v


def _rw_consts():
    tm = RW_TM
    t = jnp.arange(tm)
    same = (t[:, None] // CHUNK) == (t[None, :] // CHUNK)
    tri_f = same & (t[None, :] <= t[:, None])
    tri_b = same & (t[None, :] >= t[:, None])
    tri = jnp.stack([tri_f, tri_b]).astype(BF16)
    sel = ((t[None, :] // CHUNK) == jnp.arange(SUBLANES)[:, None]).astype(BF16)
    lane_head = jnp.arange(RW_W) // RW_HEAD
    e = (lane_head[:, None] == jnp.arange(LANES)[None, :]).astype(BF16)
    return tri, sel, e, e.T


def rw_prep(u, prm):
    t = u.shape[0]
    tm = RW_TM
    nt = t // tm
    hb = tm // SUBLANES
    tri, sel, e, et = _rw_consts()
    full = lambda shp: pl.BlockSpec(shp, lambda i: (0,) * len(shp))
    in_specs = [
        pl.BlockSpec((tm, RW_COLS_PAD), lambda i: (i, 0)),
        pl.BlockSpec((SUBLANES, RW_COLS_PAD), lambda i: (jnp.maximum(i * hb - 1, 0), 0)),
        pl.BlockSpec((SUBLANES, RW_COLS_PAD), lambda i: (jnp.minimum((i + 1) * hb, nt * hb - 1), 0)),
        full((SUBLANES, RW_COLS_PAD)),
        full((2, RW_W)), full((2, RW_W)),
        full((2, LORA_ALL, RW_W)), full((2, LORA_ALL, RW_W)),
        full((LORA_G, RW_W)),
        full((1, RW_W)), full((1, RW_W)), full((1, RW_W)),
        full((RW_W, LANES)), full((LANES, RW_W)),
        full((2, tm, tm)), full((SUBLANES, tm)),
    ]
    tok = lambda dt: jax.ShapeDtypeStruct((t, RW_W), dt)
    tok2 = lambda dt: jax.ShapeDtypeStruct((2, t, RW_W), dt)
    spec1 = pl.BlockSpec((tm, RW_W), lambda i: (i, 0))
    spec2 = pl.BlockSpec((2, tm, RW_W), lambda i: (0, i, 0))
    chan2 = jax.ShapeDtypeStruct((2, RW_W, t), BF16)
    spec2t = pl.BlockSpec((2, RW_W, tm), lambda i: (0, 0, i))
    out_shape = [tok(BF16), tok(F32), tok(F32), tok2(BF16), tok2(BF16), chan2, chan2,
                 jax.ShapeDtypeStruct((2, nt * SUBLANES, RW_W), F32)]
    out_specs = [spec1, spec1, spec1, spec2, spec2, spec2t, spec2t,
                 pl.BlockSpec((2, SUBLANES, RW_W), lambda i: (0, i, 0))]
    return pl.pallas_call(
        _rw_prep_body,
        grid=(nt,),
        in_specs=in_specs,
        out_specs=out_specs,
        out_shape=out_shape,
        compiler_params=_cparams(1),
        name="rw_prep",
    )(u, u, u, prm["taps"], prm["w0"], prm["a0"], prm["w2"], prm["a2"], prm["g2"], prm["kks"], prm["ka"],
      prm["rk"], e, et, tri, sel)


RW_PAIRS = RW_HEADS // 2
SCAN_G = 8
SC_T = RW_TM


def _bmm(a, b):
    return lax.dot_general(a, b, (((2,), (1,)), ((0,), (0,))), preferred_element_type=F32)


def _rw_scan_body(rt_ref, kkt_ref, v_ref, kbt_ref, bbt_ref, gam_ref, h0_ref, y_ref, hfin_ref, h_scr, *, reverse):
    c = pl.program_id(1)
    nc = pl.num_programs(1)

    @pl.when(c == 0)
    def _():
        h_scr[...] = h0_ref[...]

    gn = SCAN_G
    n = SC_T
    row = lax.broadcasted_iota(jnp.int32, (n, n), 0)
    col = lax.broadcasted_iota(jnp.int32, (n, n), 1)
    same = (row // CHUNK) == (col // CHUNK)
    earlier = (col > row) if reverse else (col < row)
    strict = same & earlier
    incl = same & (earlier | (col == row))
    eye = (row == col).astype(F32)
    level_masks = []
    b = 1
    while b < CHUNK:
        level_masks.append(((row // (2 * b)) == (col // (2 * b))) & ((row // b) != (col // b)))
        b *= 2
    head_a = (col < RW_HEAD).astype(BF16)
    head_b = (col >= RW_HEAD).astype(BF16)

    lanes = lambda g: slice(g * n, (g + 1) * n)
    rt_p = [rt_ref[0, :, lanes(g)] for g in range(gn)]
    kk_p = [kkt_ref[0, :, lanes(g)] for g in range(gn)]
    v_p = [v_ref[:, lanes(g)] for g in range(gn)]
    kbt = kbt_ref[0].reshape(gn, n, n)
    bbt = bbt_ref[0].reshape(gn, n, n)

    lhs = jnp.stack([jnp.concatenate([rt_p[g] * head_a, kk_p[g] * head_a, rt_p[g] * head_b, kk_p[g] * head_b],
                                     axis=0) for g in range(gn)])
    ak = _bmm(lhs, kbt)
    ab = _bmm(lhs, bbt)
    per_head = lambda x, i: jnp.concatenate([x[:, i * n:(i + 1) * n], x[:, (i + 2) * n:(i + 3) * n]], axis=0)
    ark = jnp.where(incl, per_head(ak, 0), 0.0)
    lk = jnp.where(strict, per_head(ak, 1), 0.0)
    arb = jnp.where(incl, per_head(ab, 0), 0.0)
    lb = jnp.where(strict, per_head(ab, 1), 0.0)

    x = eye - jnp.where(level_masks[0], lb, 0.0)
    for mask in level_masks[1:]:
        xb = x.astype(BF16)
        off = jnp.where(mask, lb, 0.0).astype(BF16)
        x = x - _bmm(xb, _bmm(off, xb).astype(BF16))
    tb = x.astype(BF16)

    kkm = jnp.stack([kk_p[g] * head_a for g in range(gn)] + [kk_p[g] * head_b for g in range(gn)])
    vm = jnp.stack([v_p[g] * head_a for g in range(gn)] + [v_p[g] * head_b for g in range(gn)])
    wk = _bmm(tb, kkm)
    u0 = _bmm(tb, _bmm(lk.astype(BF16), vm).astype(BF16))
    arbb = arb.astype(BF16)
    qw = _bmm(arbb, wk.astype(BF16))
    y0 = _bmm(jnp.concatenate([ark.astype(BF16), -arbb], axis=2),
              jnp.concatenate([vm, u0.astype(BF16)], axis=1))
    pair = lambda t: t[:gn] + t[gn:]
    wk_p = pair(wk).astype(BF16)
    u0_p = pair(u0).astype(BF16)
    qhat = (jnp.stack(rt_p).astype(F32) - pair(qw)).astype(BF16)
    y0_p = pair(y0)
    vu = jnp.concatenate([jnp.stack(v_p), u0_p], axis=1)

    h = h_scr[...]
    chunks = range(n // CHUNK)
    for j in (reversed(chunks) if reverse else chunks):
        in_chunk = ((col // CHUNK) == j).astype(BF16)
        kj = kbt * in_chunk
        bj = bbt * in_chunk
        hloc = jnp.where(same, _bmm(jnp.concatenate([kj, -bj], axis=2), vu), 0.0)
        wb = jnp.where(same, _bmm(bj, wk_p), 0.0).astype(BF16)
        gamma = jnp.stack([jnp.broadcast_to(gam_ref[0, j:j + 1, lanes(g)], (n, n)).T for g in range(gn)])
        hb = h.astype(BF16)
        rows = slice(j * CHUNK, (j + 1) * CHUNK)
        yj = _bmm(qhat[:, rows], hb) + y0_p[:, rows]
        for g in range(gn):
            y_ref[rows, lanes(g)] = yj[g]
        h = gamma * (h - _bmm(wb, hb) + hloc)
    h_scr[...] = h

    @pl.when(c == nc - 1)
    def _():
        hfin_ref[...] = h


def rw_scan(rt, kkt, v, kbar_t, bbar_t, gam, h0, d):
    t = v.shape[0]
    nsc = t // SC_T
    reverse = d == 1
    gw = SCAN_G * LANES
    tix = (lambda c: nsc - 1 - c) if reverse else (lambda c: c)
    tm_spec = pl.BlockSpec((1, SC_T, gw), lambda p, c: (d, tix(c), p))
    cm_spec = pl.BlockSpec((1, gw, SC_T), lambda p, c: (d, p, tix(c)))
    st_spec = pl.BlockSpec((SCAN_G, LANES, LANES), lambda p, c: (p, 0, 0))
    return pl.pallas_call(
        functools.partial(_rw_scan_body, reverse=reverse),
        grid=(RW_PAIRS // SCAN_G, nsc),
        in_specs=[tm_spec, tm_spec, pl.BlockSpec((SC_T, gw), lambda p, c: (tix(c), p)), cm_spec, cm_spec,
                  pl.BlockSpec((1, SUBLANES, gw), lambda p, c: (d, tix(c), p)), st_spec],
        out_specs=[pl.BlockSpec((SC_T, gw), lambda p, c: (tix(c), p)), st_spec],
        out_shape=[jax.ShapeDtypeStruct((t, RW_W), F32),
                   jax.ShapeDtypeStruct((RW_PAIRS, LANES, LANES), F32)],
        scratch_shapes=[pltpu.VMEM((SCAN_G, LANES, LANES), F32)],
        compiler_params=_cparams(2),
        name="rw_scan_rev" if reverse else "rw_scan_fwd",
    )(rt, kkt, v, kbar_t, bbar_t, gam, h0)


def _rw_post_body(yf_ref, yb_ref, bv_ref, g_ref, lg_ref, lb_ref, e_ref, et_ref, o_ref):
    y = yf_ref[...] + yb_ref[...]
    mu = _seg_sum_bcast(y, e_ref, et_ref) * (1.0 / RW_HEAD)
    dlt = y - mu
    var = _seg_sum_bcast(dlt * dlt, e_ref, et_ref) * (1.0 / RW_HEAD)
    yn = dlt * lax.rsqrt(var + GN_EPS)
    o_ref[...] = ((yn * lg_ref[...] + lb_ref[...] + bv_ref[...]) * g_ref[...]).astype(o_ref.dtype)


def rw_post(yf, yb, bv, g, lnx_g, lnx_b):
    t = yf.shape[0]
    tm = 256
    _, _, e, et = _rw_consts()
    tokspec = pl.BlockSpec((tm, RW_W), lambda i: (i, 0))
    row = pl.BlockSpec((1, RW_W), lambda i: (0, 0))
    return pl.pallas_call(
        _rw_post_body,
        grid=(t // tm,),
        in_specs=[tokspec, tokspec, tokspec, tokspec, row, row,
                  pl.BlockSpec((RW_W, LANES), lambda i: (0, 0)), pl.BlockSpec((LANES, RW_W), lambda i: (0, 0))],
        out_specs=tokspec,
        out_shape=jax.ShapeDtypeStruct((t, RW_W), BF16),
        compiler_params=_cparams(1),
        name="rw_post",
    )(yf, yb, bv, g, lnx_g, lnx_b, e, et)


def rwkv_group(u, prm, h0_f, h0_b, with_output):
    v, g, bv, rt, kkt, kbar_t, bbar_t, gam = rw_prep(u, prm)
    outs = []
    finals = []
    for d, h0 in ((0, h0_f), (1, h0_b)):
        y, hfin = rw_scan(rt, kkt, v, kbar_t, bbar_t, gam, h0, d)
        outs.append(y)
        finals.append(hfin)
    if not with_output:
        return None, finals[0], finals[1]
    out = rw_post(outs[0], outs[1], bv, g, prm["lnx_g"], prm["lnx_b"])
    return out, finals[0], finals[1]


NA_RQ = 4


def _na_body(q_ref, k_ref, v_ref, kc_ref, vc_ref, bt_ref, o_ref, *, n_rows):
    rb = pl.program_id(1)
    scale = NA_HEAD ** -0.5
    kc = kc_ref[...]
    vc = vc_ref[...]
    for qi in range(NA_RQ):
        r = rb * NA_RQ + qi
        start = jnp.clip(r - NA_KH // 2, 0, n_rows - NA_KH)
        dr0 = start - r + NA_KH - 1
        tok0 = pl.multiple_of(start * GRID_W, GRID_W)
        q = q_ref[qi * GRID_W:(qi + 1) * GRID_W, :]
        kw = k_ref[pl.ds(tok0, NA_KH * GRID_W), :]
        vw = v_ref[pl.ds(tok0, NA_KH * GRID_W), :]
        s = _dot_nt(q, kw) * scale
        bias = jnp.concatenate([bt_ref[0, dr0 + 2 * m] for m in range(NA_KH // 2)], axis=1)
        s = s + bias
        sc = _dot_nt(q, kc) * scale
        mx = jnp.maximum(jnp.max(s, axis=-1, keepdims=True), jnp.max(sc, axis=-1, keepdims=True))
        p = jnp.exp(s - mx)
        pc = jnp.exp(sc - mx)
        den = jnp.sum(p, axis=-1, keepdims=True) + jnp.sum(pc, axis=-1, keepdims=True)
        o = (_dot(p.astype(BF16), vw) + _dot(pc.astype(BF16), vc)) / den
        o_ref[qi * GRID_W:(qi + 1) * GRID_W, :] = o.astype(o_ref.dtype)


def _na_bias_table(rpb):
    cols = jnp.arange(GRID_W)
    col_start = jnp.clip(cols - NA_KW // 2, 0, GRID_W - NA_KW)
    col_ok = (cols[None, :] >= col_start[:, None]) & (cols[None, :] < col_start[:, None] + NA_KW)
    dc = jnp.clip(cols[None, :] - cols[:, None] + NA_KW - 1, 0, 2 * NA_KW - 2)
    bt = rpb.astype(F32)[:, :, dc]
    bt = jnp.where(col_ok[None, None], bt, NEG_INF)
    return jnp.concatenate([bt[:, :-1], bt[:, 1:]], axis=-1)


def na_latent(att_l, att_c, rpb):
    s_len = att_l.shape[0]
    n_rows = s_len // GRID_W
    qb = NA_W // NA_HEAD
    bt = _na_bias_table(rpb)
    return pl.pallas_call(
        functools.partial(_na_body, n_rows=n_rows),
        grid=(NA_HEADS, n_rows // NA_RQ),
        in_specs=[
            pl.BlockSpec((NA_RQ * GRID_W, NA_HEAD), lambda h, r: (r, h)),
            pl.BlockSpec((s_len, NA_HEAD), lambda h, r: (0, qb + h)),
            pl.BlockSpec((s_len, NA_HEAD), lambda h, r: (0, 2 * qb + h)),
            pl.BlockSpec((CTX_LEN, NA_HEAD), lambda h, r: (0, qb + h)),
            pl.BlockSpec((CTX_LEN, NA_HEAD), lambda h, r: (0, 2 * qb + h)),
            pl.BlockSpec((1, 2 * NA_KH - 2, GRID_W, 2 * GRID_W), lambda h, r: (h, 0, 0, 0)),
        ],
        out_specs=pl.BlockSpec((NA_RQ * GRID_W, NA_HEAD), lambda h, r: (r, h)),
        out_shape=jax.ShapeDtypeStruct((s_len, NA_W), BF16),
        compiler_params=_cparams(2),
        name="na_latent",
    )(att_l, att_l, att_l, att_c, att_c, bt)


def _na_ctx_body(q_ref, k_ref, v_ref, o_ref):
    s = _dot_nt(q_ref[...], k_ref[...]) * (NA_HEAD ** -0.5)
    mx = jnp.max(s, axis=-1, keepdims=True)
    p = jnp.exp(s - mx)
    den = jnp.sum(p, axis=-1, keepdims=True)
    o_ref[...] = (_dot(p.astype(BF16), v_ref[...]) / den).astype(o_ref.dtype)


def na_ctx(att_c):
    qb = NA_W // NA_HEAD
    blk = lambda off: pl.BlockSpec((CTX_LEN, NA_HEAD), lambda h: (0, off + h))
    return pl.pallas_call(
        _na_ctx_body,
        grid=(NA_HEADS,),
        in_specs=[blk(0), blk(qb), blk(2 * qb)],
        out_specs=blk(0),
        out_shape=jax.ShapeDtypeStruct((CTX_LEN, NA_W), BF16),
        compiler_params=_cparams(1),
        name="na_ctx",
    )(att_c, att_c, att_c)


def _rope_body(x_ref, c_ref, s1_ref, s2_ref, o_ref):
    x = x_ref[...].astype(F32)
    quarter = SW_HEAD // 4
    x_up = pltpu.roll(x, LANES - quarter, 1)
    x_dn = pltpu.roll(x, quarter, 1)
    o_ref[...] = (x * c_ref[...] + x_up * s1_ref[...] + x_dn * s2_ref[...]).astype(o_ref.dtype)


def _rope_tables(s_len):
    t = jnp.arange(s_len)
    row = (t // GRID_W).astype(F32)
    col = (t % GRID_W).astype(F32)
    half = SW_HEAD // 2
    inv = ROPE_BASE ** (-jnp.arange(0, half, 2, dtype=F32) / half)
    ang_r = row[:, None] * inv[None, :]
    ang_c = col[:, None] * inv[None, :]
    cos = jnp.concatenate([jnp.cos(ang_r)] * 2 + [jnp.cos(ang_c)] * 2, axis=-1)
    sin_r, sin_c = jnp.sin(ang_r), jnp.sin(ang_c)
    zero = jnp.zeros_like(sin_r)
    s_up = jnp.concatenate([-sin_r, zero, -sin_c, zero], axis=-1)
    s_dn = jnp.concatenate([zero, sin_r, zero, sin_c], axis=-1)
    tile2 = lambda a: jnp.concatenate([a, a], axis=-1)
    return tile2(cos), tile2(s_up), tile2(s_dn)


def rope_qk(att_l):
    s_len = att_l.shape[0]
    tm = 512
    nblk = (SW_W + SW_KV * SW_HEAD) // LANES
    off = NA_COLS // LANES
    cos, s_up, s_dn = _rope_tables(s_len)
    tab = pl.BlockSpec((tm, LANES), lambda i, j: (i, 0))
    return pl.pallas_call(
        _rope_body,
        grid=(s_len // tm, nblk),
        in_specs=[pl.BlockSpec((tm, LANES), lambda i, j: (i, off + j)), tab, tab, tab],
        out_specs=pl.BlockSpec((tm, LANES), lambda i, j: (i, j)),
        out_shape=jax.ShapeDtypeStruct((s_len, nblk * LANES), BF16),
        compiler_params=_cparams(2),
        name="rope_qk",
    )(att_l, cos, s_up, s_dn)


def _sw_softmax_out(q8, s_parts, v_parts, sink_col):
    mx = sink_col
    for s in s_parts:
        mx = jnp.maximum(mx, jnp.max(s, axis=-1, keepdims=True))
    den = jnp.exp(sink_col - mx)
    o = None
    for s, v in zip(s_parts, v_parts):
        p = jnp.exp(s - mx)
        den = den + jnp.sum(p, axis=-1, keepdims=True)
        pv = _dot(p.astype(BF16), v)
        o = pv if o is None else o + pv
    return o / den


def _sink_column(sink_ref, kh, rows_per_head):
    n = SW_GROUP * rows_per_head
    grp = lax.broadcasted_iota(jnp.int32, (n, 1), 0) // rows_per_head
    col = jnp.zeros((n, 1), F32)
    for g in range(SW_GROUP):
        col = jnp.where(grp == g, sink_ref[kh * SW_GROUP + g], col)
    return col


def _sw_body(sink_ref, q_ref, k_ref, v_ref, kc_ref, vc_ref, o_ref, *, s_len):
    n = pl.program_id(0)
    scale = SW_HEAD ** -0.5
    win = 3 * SW_BLOCK
    start = pl.multiple_of(jnp.clip((n - 1) * SW_BLOCK, 0, s_len - win), SW_BLOCK)
    k2 = k_ref[pl.ds(start, win), :]
    v2 = v_ref[pl.ds(start, win), :]
    qpos = n * SW_BLOCK + lax.broadcasted_iota(jnp.int32, (SW_BLOCK, win), 0)
    kpos = start + lax.broadcasted_iota(jnp.int32, (SW_BLOCK, win), 1)
    ok = jnp.abs(kpos - qpos) <= SW_WIN
    for kh in range(SW_KV):
        lanes = slice(kh * SW_HEAD, (kh + 1) * SW_HEAD)
        q8 = jnp.concatenate(
            [q_ref[:, (kh * SW_GROUP + g) * SW_HEAD:(kh * SW_GROUP + g + 1) * SW_HEAD] for g in range(SW_GROUP)],
            axis=0)
        s_loc = _dot_nt(q8, k2[:, lanes]) * scale
        s_loc = jnp.where(ok[None], s_loc.reshape(SW_GROUP, SW_BLOCK, win), NEG_INF).reshape(
            SW_GROUP * SW_BLOCK, win)
        s_ctx = _dot_nt(q8, kc_ref[:, lanes]) * scale
        o = _sw_softmax_out(q8, [s_loc, s_ctx], [v2[:, lanes], vc_ref[:, lanes]],
                            _sink_column(sink_ref, kh, SW_BLOCK))
        for g in range(SW_GROUP):
            c0 = (kh * SW_GROUP + g) * SW_HEAD
            o_ref[:, c0:c0 + SW_HEAD] = o[g * SW_BLOCK:(g + 1) * SW_BLOCK].astype(o_ref.dtype)


def swa_latent(qk_rot, att_l, att_c, sink):
    s_len = qk_rot.shape[0]
    kcol = SW_W // LANES
    ck = (NA_COLS + SW_W) // LANES
    return pl.pallas_call(
        functools.partial(_sw_body, s_len=s_len),
        grid=(s_len // SW_BLOCK,),
        in_specs=[
            pl.BlockSpec(memory_space=pltpu.SMEM),
            pl.BlockSpec((SW_BLOCK, SW_W), lambda n: (n, 0)),
            pl.BlockSpec((s_len, LANES), lambda n: (0, kcol)),
            pl.BlockSpec((s_len, LANES), lambda n: (0, ck + 1)),
            pl.BlockSpec((CTX_LEN, LANES), lambda n: (0, ck)),
            pl.BlockSpec((CTX_LEN, LANES), lambda n: (0, ck + 1)),
        ],
        out_specs=pl.BlockSpec((SW_BLOCK, SW_W), lambda n: (n, 0)),
        out_shape=jax.ShapeDtypeStruct((s_len, SW_W), BF16),
        compiler_params=_cparams(1),
        name="swa_latent",
    )(sink, qk_rot, qk_rot, att_l, att_c, att_c)


def _sw_ctx_body(sink_ref, q_ref, k_ref, v_ref, o_ref):
    scale = SW_HEAD ** -0.5
    for kh in range(SW_KV):
        lanes = slice(kh * SW_HEAD, (kh + 1) * SW_HEAD)
        q8 = jnp.concatenate(
            [q_ref[:, (kh * SW_GROUP + g) * SW_HEAD:(kh * SW_GROUP + g + 1) * SW_HEAD] for g in range(SW_GROUP)],
            axis=0)
        s = _dot_nt(q8, k_ref[:, lanes]) * scale
        o = _sw_softmax_out(q8, [s], [v_ref[:, lanes]], _sink_column(sink_ref, kh, CTX_LEN))
        for g in range(SW_GROUP):
            c0 = (kh * SW_GROUP + g) * SW_HEAD
            o_ref[:, c0:c0 + SW_HEAD] = o[g * CTX_LEN:(g + 1) * CTX_LEN].astype(o_ref.dtype)


def swa_ctx(att_c, sink):
    qo = NA_COLS // SW_W
    ck = (NA_COLS + SW_W) // LANES
    return pl.pallas_call(
        _sw_ctx_body,
        grid=(1,),
        in_specs=[
            pl.BlockSpec(memory_space=pltpu.SMEM),
            pl.BlockSpec((CTX_LEN, SW_W), lambda n: (0, qo)),
            pl.BlockSpec((CTX_LEN, LANES), lambda n: (0, ck)),
            pl.BlockSpec((CTX_LEN, LANES), lambda n: (0, ck + 1)),
        ],
        out_specs=pl.BlockSpec((CTX_LEN, SW_W), lambda n: (0, 0)),
        out_shape=jax.ShapeDtypeStruct((CTX_LEN, SW_W), BF16),
        compiler_params=_cparams(1),
        name="swa_ctx",
    )(sink, att_c, att_c, att_c)


def _router_body(x_ref, g_ref, sc_ref, sh_ref, wr_ref, rb_ref, h_ref, pick_ref, hp_ref):
    x = x_ref[...]
    tm = x.shape[0]
    ms = jnp.mean(x * x, axis=-1, keepdims=True)
    h = (x * lax.rsqrt(ms + NORM_EPS) * g_ref[...]) * (1.0 + sc_ref[...]) + sh_ref[...]
    h_ref[...] = h.astype(h_ref.dtype)

    hh, hl = _split2(h)
    wh, wl = _split2(wr_ref[...])
    logits = _dot(hh, wh) + _dot(hl, wh) + _dot(hh, wl)
    lane = lax.broadcasted_iota(jnp.int32, (tm, LANES), 1)
    lane_f = lane.astype(F32)
    valid = lane < N_EXPERTS
    neg = -jnp.inf
    big = float(2 * LANES)
    scores = _sigmoid(logits)
    biased = jnp.where(valid, scores + rb_ref[...], neg)
    grp = lane // EXPERTS_PER_GROUP

    def first_argmax(vals):
        m = jnp.max(vals, axis=-1, keepdims=True)
        idx = jnp.min(jnp.where(vals == m, lane_f, big), axis=-1, keepdims=True)
        return m, idx

    gscore = jnp.full((tm, LANES), neg, F32)
    for gi in range(N_GROUPS):
        vg = jnp.where(grp == gi, biased, neg)
        m1, i1 = first_argmax(vg)
        m2 = jnp.max(jnp.where(lane_f == i1, neg, vg), axis=-1, keepdims=True)
        gscore = jnp.where(lane == gi * EXPERTS_PER_GROUP, m1 + m2, gscore)
    keep = jnp.zeros((tm, LANES), jnp.bool_)
    for _ in range(TOPK_GROUPS):
        _, gi = first_argmax(gscore)
        keep = keep | (grp.astype(F32) == jnp.floor(gi * (1.0 / EXPERTS_PER_GROUP)))
        gscore = jnp.where(lane_f == gi, neg, gscore)
    cur = jnp.where(valid, jnp.where(keep, biased, NEG_INF), neg)
    sel = jnp.zeros((tm, LANES), jnp.bool_)
    picks = jnp.zeros((tm, LANES), F32)
    for k in range(TOP_K):
        _, ei = first_argmax(cur)
        hit = lane_f == ei
        sel = sel | hit
        cur = jnp.where(hit, neg, cur)
        picks = jnp.where(lane == k, ei, picks)
        picks = jnp.where(lane == TOP_K + k, jnp.sum(jnp.where(hit, scores, 0.0), axis=-1, keepdims=True), picks)
    w = jnp.where(sel, scores, 0.0)
    norm = ROUTE_SCALE / jnp.sum(w, axis=-1, keepdims=True)
    pick_ref[...] = jnp.where((lane >= TOP_K) & (lane < 2 * TOP_K), picks * norm, picks)

    bits = pltpu.bitcast(h.astype(BF16).astype(F32), jnp.uint32)
    half = D_MODEL // 2
    hp_ref[...] = (bits[:, half:] & jnp.uint32(0xFFFF0000)) | (bits[:, :half] >> 16)


def router(x, g, sc, sh, w_router_pad, r_bias_pad):
    t = x.shape[0]
    tm = 256
    row = pl.BlockSpec((1, D_MODEL), lambda i: (0, 0))
    return pl.pallas_call(
        _router_body,
        grid=(t // tm,),
        in_specs=[pl.BlockSpec((tm, D_MODEL), lambda i: (i, 0)), row, row, row,
                  pl.BlockSpec((D_MODEL, LANES), lambda i: (0, 0)), pl.BlockSpec((1, LANES), lambda i: (0, 0))],
        out_specs=[pl.BlockSpec((tm, D_MODEL), lambda i: (i, 0)), pl.BlockSpec((tm, LANES), lambda i: (i, 0)),
                   pl.BlockSpec((tm, D_MODEL // 2), lambda i: (i, 0))],
        out_shape=[jax.ShapeDtypeStruct((t, D_MODEL), BF16), jax.ShapeDtypeStruct((t, LANES), F32),
                   jax.ShapeDtypeStruct((t, D_MODEL // 2), jnp.uint32)],
        compiler_params=_cparams(1),
        name="router",
    )(x, g, sc, sh, w_router_pad, r_bias_pad)


MOE_TM = 512
MOE_TM_SMALL = 64
PACK_W = D_MODEL // 2
MOE_DOWN_COLS = 1024


def _unpack_bf16_pairs(words):
    lo = pltpu.bitcast(words << 16, F32)
    hi = pltpu.bitcast(words & jnp.uint32(0xFFFF0000), F32)
    return lo, hi


def _pack_bf16_pairs(lo, hi):
    lo_bits = pltpu.bitcast(lo.astype(BF16).astype(F32), jnp.uint32) >> 16
    hi_bits = pltpu.bitcast(hi.astype(BF16).astype(F32), jnp.uint32) & jnp.uint32(0xFFFF0000)
    return hi_bits | lo_bits


def _routed_body(te_ref, nv_ref, src_ref, dst_ref, hp_hbm, wg_ref, wu_ref, wd_ref, yk_hbm, xbuf, ybuf, gsem, ssem,
                 *, tm, n_real_rows):
    s = pl.program_id(0)
    nv = nv_ref[0]

    def gather_wait(slot):
        pltpu.make_async_copy(hp_hbm.at[pl.ds(0, tm)], xbuf.at[slot], gsem.at[slot]).wait()

    def scatter_wait(slot):
        pltpu.make_async_copy(ybuf.at[slot], yk_hbm.at[pl.ds(0, tm)], ssem.at[slot]).wait()

    @pl.when(s == 0)
    def _():
        ybuf[1] = jnp.zeros((tm, PACK_W), jnp.uint32)
        fill = pltpu.make_async_copy(ybuf.at[1], yk_hbm.at[pl.ds(n_real_rows, tm)], ssem.at[1])
        fill.start()
        fill.wait()

    @pl.when(s < nv)
    def _():
        slot = s % 2
        for r in range(tm):
            pltpu.make_async_copy(hp_hbm.at[pl.ds(src_ref[0, 0, r], 1)], xbuf.at[slot, pl.ds(r, 1)],
                                  gsem.at[slot]).start(priority=r % 2)

    @pl.when((s >= 1) & (s <= nv))
    def _():
        t = s - 1
        slot = t % 2
        gather_wait(slot)
        x_lo, x_hi = _unpack_bf16_pairs(xbuf[slot])
        x_lo = x_lo.astype(BF16)
        x_hi = x_hi.astype(BF16)
        hg = _dot(x_lo, wg_ref[0, 0, :PACK_W].astype(BF16)) + _dot(x_hi, wg_ref[0, 0, PACK_W:].astype(BF16))
        hu = _dot(x_lo, wu_ref[0, 0, :PACK_W].astype(BF16)) + _dot(x_hi, wu_ref[0, 0, PACK_W:].astype(BF16))
        act = ((hg * _sigmoid(hg)) * hu).astype(BF16)
        for c0 in range(0, PACK_W, MOE_DOWN_COLS):
            y_lo = _dot(act, wd_ref[0, 0, :, c0:c0 + MOE_DOWN_COLS].astype(BF16))
            y_hi = _dot(act, wd_ref[0, 0, :, PACK_W + c0:PACK_W + c0 + MOE_DOWN_COLS].astype(BF16))
            ybuf[slot, :, c0:c0 + MOE_DOWN_COLS] = _pack_bf16_pairs(y_lo, y_hi)

        @pl.when(t >= 1)
        def _():
            scatter_wait(1 - slot)

        for r in range(tm):
            pltpu.make_async_copy(ybuf.at[slot, pl.ds(r, 1)], yk_hbm.at[pl.ds(dst_ref[0, 0, r], 1)],
                                  ssem.at[slot]).start(priority=r % 2)

        @pl.when(s == nv)
        def _():
            scatter_wait(slot)


def _dispatch_plan(picks, t, tm):
    n_pairs = TOP_K * t
    n_tiles = n_pairs // tm + N_EXPERTS
    e_flat = picks[:, :TOP_K].astype(jnp.int32).reshape(-1)
    pair = jnp.arange(n_pairs, dtype=jnp.int32)
    dst = (pair % TOP_K) * t + pair // TOP_K
    experts = jnp.arange(N_EXPERTS, dtype=jnp.int32)
    counts = jnp.sum((e_flat[:, None] == experts[None, :]).astype(jnp.int32), axis=0)
    fill = (-counts) % tm
    filler_e = jnp.repeat(experts, tm)
    filler_j = jnp.tile(jnp.arange(tm, dtype=jnp.int32), N_EXPERTS)
    filler_key = jnp.where(filler_j < jnp.repeat(fill, tm), 2 * filler_e + 1, 2 * N_EXPERTS)
    dst_bits = (n_pairs + tm - 1).bit_length()
    assert 2 * N_EXPERTS < (1 << (31 - dst_bits))
    keys = jnp.concatenate([2 * e_flat, filler_key]) * (1 << dst_bits) + jnp.concatenate([dst, n_pairs + filler_j])
    dst_sorted = lax.sort(keys) & ((1 << dst_bits) - 1)
    src_sorted = jnp.where(dst_sorted < n_pairs, dst_sorted % t, 0)
    tile_end = jnp.cumsum((counts + fill) // tm)
    n_valid = tile_end[-1].astype(jnp.int32)
    ti = jnp.minimum(jnp.arange(n_tiles, dtype=jnp.int32), n_valid - 1)
    tile_e = jnp.sum((tile_end[None, :] <= ti[:, None]).astype(jnp.int32), axis=1)
    return (tile_e.astype(jnp.int32), n_valid.reshape(1), src_sorted.reshape(n_tiles, 1, tm),
            dst_sorted.reshape(n_tiles, 1, tm))


def moe_routed(hp, picks, layer, we_gate, we_up, we_down):
    t = hp.shape[0]
    tm = MOE_TM if TOP_K * t >= N_EXPERTS * MOE_TM else MOE_TM_SMALL
    tile_e, n_valid, src, dst = _dispatch_plan(picks, t, tm)
    n_tiles = tile_e.shape[0]
    last = n_tiles - 1
    cur = lambda s: jnp.minimum(s, last)
    prev = lambda s: jnp.clip(s - 1, 0, last)
    smem = lambda f: pl.BlockSpec((1, 1, tm), lambda s, te, nv: (f(s), 0, 0), memory_space=pltpu.SMEM)
    wspec = lambda shp: pl.BlockSpec((1, 1) + shp, lambda s, te, nv: (layer, te[prev(s)], 0, 0))
    grid_spec = pltpu.PrefetchScalarGridSpec(
        num_scalar_prefetch=2,
        grid=(n_tiles + 1,),
        in_specs=[
            smem(cur), smem(prev),
            pl.BlockSpec(memory_space=pl.ANY),
            wspec((D_MODEL, D_EXPERT)), wspec((D_MODEL, D_EXPERT)), wspec((D_EXPERT, D_MODEL)),
        ],
        out_specs=pl.BlockSpec(memory_space=pl.ANY),
        scratch_shapes=[pltpu.VMEM((2, tm, PACK_W), jnp.uint32), pltpu.VMEM((2, tm, PACK_W), jnp.uint32),
                        pltpu.SemaphoreType.DMA((2,)), pltpu.SemaphoreType.DMA((2,))],
    )
    return pl.pallas_call(
        functools.partial(_routed_body, tm=tm, n_real_rows=TOP_K * t),
        grid_spec=grid_spec,
        out_shape=jax.ShapeDtypeStruct((TOP_K * t + tm, PACK_W), jnp.uint32),
        compiler_params=_cparams(1),
        name="moe_routed",
    )(tile_e, n_valid, src, dst, hp, we_gate, we_up, we_down)


COMBINE_TM = 128


def _combine_body(h_ref, pick_ref, *rest):
    yk_refs = rest[:TOP_K]
    wg_ref, wu_ref, wd_ref, x_ref, ga_ref, o_ref = rest[TOP_K:]
    h = h_ref[...]
    hg = _dot(h, wg_ref[...])
    hu = _dot(h, wu_ref[...])
    act = ((hg * _sigmoid(hg)) * hu).astype(BF16)
    picks = pick_ref[...]
    for half in range(2):
        cols = slice(half * PACK_W, (half + 1) * PACK_W)
        acc = _dot(act, wd_ref[:, cols])
        for k in range(TOP_K):
            acc = acc + picks[:, TOP_K + k:TOP_K + k + 1] * _unpack_bf16_pairs(yk_refs[k][...])[half]
        o_ref[:, cols] = x_ref[:, cols] + ga_ref[:, cols] * acc


def moe_combine(h, picks, yk, ws_gate, ws_up, ws_down, x, ga):
    t = h.shape[0]
    tm = COMBINE_TM
    nb = t // tm
    tok = lambda cols: pl.BlockSpec((tm, cols), lambda i: (i, 0))
    full = lambda a: pl.BlockSpec(a.shape, lambda i: (0,) * a.ndim)
    slot = lambda k: pl.BlockSpec((tm, PACK_W), lambda i: (k * nb + i, 0))
    return pl.pallas_call(
        _combine_body,
        grid=(nb,),
        in_specs=[tok(D_MODEL), tok(LANES)] + [slot(k) for k in range(TOP_K)] + [
            full(ws_gate), full(ws_up), full(ws_down), tok(D_MODEL), pl.BlockSpec((1, D_MODEL), lambda i: (0, 0))],
        out_specs=tok(D_MODEL),
        out_shape=jax.ShapeDtypeStruct((t, D_MODEL), F32),
        compiler_params=_cparams(1),
        name="moe_combine",
    )(h, picks, *([yk] * TOP_K), ws_gate, ws_up, ws_down, x, ga)


def _pad_cols(a, n):
    return jnp.pad(a, [(0, 0)] * (a.ndim - 1) + [(0, n - a.shape[-1])])


def _rw_params(i, rw_conv, rw_w0, rw_w2, rw_a0, rw_a2, rw_g2, rw_kk, rw_ka, rw_rk, rw_lnx_g, rw_lnx_b):
    w2 = jnp.zeros((2, LORA_ALL, RW_W), F32)
    a2 = jnp.zeros((2, LORA_ALL, RW_W), F32)
    for d in range(2):
        w2 = w2.at[d, d * LORA_W:(d + 1) * LORA_W].set(rw_w2[i, d])
        a2 = a2.at[d, 2 * LORA_W + d * LORA_A:2 * LORA_W + (d + 1) * LORA_A].set(rw_a2[i, d])
    taps = jnp.pad(rw_conv[i], ((0, SUBLANES - 3), (0, RW_COLS_PAD - RW_COLS)))
    return dict(
        taps=taps, w0=rw_w0[i], a0=rw_a0[i], w2=w2.astype(BF16), a2=a2.astype(BF16), g2=rw_g2[i].astype(BF16),
        kks=rw_kk[i][None], ka=rw_ka[i][None], rk=rw_rk[i].reshape(1, RW_W),
        lnx_g=rw_lnx_g[i][None], lnx_b=rw_lnx_b[i][None])


def kernel(x, c, ctx, c_ctx, w_ada, b_ada, norm1_g, norm2_g, w_in, rw_conv, rw_w0, rw_w2, rw_a0, rw_a2, rw_g2,
           rw_kk, rw_ka, rw_rk, rw_lnx_g, rw_lnx_b, na_rpb, sw_sink, w_out, w_router, router_bias, we_gate,
           we_up, we_down, ws_gate, ws_up, ws_down, final_g):
    assert x.shape[0] == 1 and x.shape[2] == D_MODEL and ctx.shape[1] == CTX_LEN
    xl = x[0]
    xc = ctx[0]
    cond8 = jnp.zeros((SUBLANES, D_MODEL), F32).at[0].set(c[0]).at[1].set(c_ctx)
    mods = ada_mod(cond8, w_ada, b_ada)

    for i in range(DEPTH):
        ctx_needed = i < DEPTH - 1
        mod_l = [m[None] for m in mods[i, 0].reshape(6, D_MODEL)]
        mod_c = [m[None] for m in mods[i, 1].reshape(6, D_MODEL)]
        w_rw = _pad_cols(w_in[i][:, :RW_COLS], RW_COLS_PAD).astype(BF16)
        w_att = w_in[i][:, RW_COLS:].astype(BF16)
        w_o = w_out[i].astype(BF16)
        w_o_parts = [w_o[:RW_W], w_o[RW_W:RW_W + NA_W], w_o[RW_W + NA_W:]]
        rw_prm = _rw_params(i, rw_conv, rw_w0, rw_w2, rw_a0, rw_a2, rw_g2, rw_kk, rw_ka, rw_rk, rw_lnx_g,
                            rw_lnx_b)
        n1 = norm1_g[i][None]
        n2 = norm2_g[i][None]

        hl = norm_mod(xl, n1, mod_l[1], mod_l[0])
        hc = norm_mod(xc, n1, mod_c[1], mod_c[0])
        url = matmul([hl], [w_rw], 768, F32, name="in_proj_rw")
        urc = matmul([hc], [w_rw], 768, F32, name="in_proj_rw_ctx")
        ual = matmul([hl], [w_att], 256, BF16, name="in_proj_att")
        uac = matmul([hc], [w_att], 256, BF16, name="in_proj_att_ctx")

        z = jnp.zeros((RW_PAIRS, LANES, LANES), F32)
        rwc, s_f, s_b = rwkv_group(urc, rw_prm, z, z, ctx_needed)
        rwl, _, _ = rwkv_group(url, rw_prm, s_f, s_b, True)
        nal = na_latent(ual, uac, na_rpb[i])
        swl = swa_latent(rope_qk(ual), ual, uac, sw_sink[i])
        xl = matmul([rwl, nal, swl], w_o_parts, 512, F32, residual=(xl, mod_l[2]), name="out_proj")

        wr = _pad_cols(w_router[i], LANES)
        rb = _pad_cols(router_bias[i][None], LANES)
        shared = (ws_gate[i].astype(BF16), ws_up[i].astype(BF16), ws_down[i].astype(BF16))

        def moe(xt, mod):
            h2, picks, hp = router(xt, n2, mod[4], mod[3], wr, rb)
            yk = moe_routed(hp, picks, i, we_gate, we_up, we_down)
            return moe_combine(h2, picks, yk, *shared, xt, mod[5])

        xl = moe(xl, mod_l)
        if ctx_needed:
            nac = na_ctx(uac)
            swc = swa_ctx(uac, sw_sink[i])
            xc = matmul([rwc, nac, swc], w_o_parts, 512, F32, residual=(xc, mod_c[2]), name="out_proj_ctx")
            xc = moe(xc, mod_c)
    return final_norm(xl, final_g[None])[None]
```

```python
import functools

import jax
import jax.numpy as jnp
from jax import lax
from jax.experimental import pallas as pl
from jax.experimental.pallas import tpu as pltpu

F32 = jnp.float32
BF16 = jnp.bfloat16

D_MODEL = 4096
DEPTH = 2
GRID_W = 64
CTX_LEN = 256
NORM_EPS = 1e-6
NEG_INF = -1e30

RW_HEAD = 64
RW_W = D_MODEL // 2
RW_HEADS = RW_W // RW_HEAD
LORA_W = 96
LORA_A = 96
LORA_G = 256
GN_EPS = 64e-5
RW_COLS = 3 * RW_W + LORA_G + 2 * LORA_W + 2 * LORA_A
LORA_OFF = 3 * RW_W + LORA_G
LORA_ALL = 2 * LORA_W + 2 * LORA_A

NA_HEAD = 128
NA_W = D_MODEL // 4
NA_HEADS = NA_W // NA_HEAD
NA_KH = 8
NA_KW = 16
NA_COLS = 3 * NA_W

SW_HEAD = 64
SW_W = D_MODEL - RW_W - NA_W
SW_HEADS = SW_W // SW_HEAD
SW_KV = SW_HEADS // 8
SW_GROUP = SW_HEADS // SW_KV
SW_WIN = 128
SW_BLOCK = 128
ROPE_BASE = 10000.0
SW_COLS = SW_W + 2 * SW_KV * SW_HEAD
ATT_COLS = NA_COLS + SW_COLS

N_EXPERTS = 64
N_GROUPS = 8
EXPERTS_PER_GROUP = N_EXPERTS // N_GROUPS
TOPK_GROUPS = 4
TOP_K = 8
D_EXPERT = 256
ROUTE_SCALE = 2.5

LANES = 128
SUBLANES = 8
RW_COLS_PAD = 6912
CHUNK = 64
VMEM_LIMIT = 56 * 1024 * 1024


def _cparams(n_axes, vmem=VMEM_LIMIT):
    return pltpu.CompilerParams(dimension_semantics=("arbitrary",) * n_axes, vmem_limit_bytes=vmem)


def _dot(a, b):
    return jnp.dot(a, b, preferred_element_type=F32)


def _dot_nt(a, b):
    return lax.dot_general(a, b, (((1,), (1,)), ((), ())), preferred_element_type=F32)


def _split2(x):
    hi = x.astype(BF16)
    lo = (x - hi.astype(F32)).astype(BF16)
    return hi, lo


def _split3(x):
    hi = x.astype(BF16)
    r1 = x - hi.astype(F32)
    mid = r1.astype(BF16)
    lo = (r1 - mid.astype(F32)).astype(BF16)
    return hi, mid, lo


def _sigmoid(x):
    return 1.0 / (1.0 + jnp.exp(-x))


def _ada_body(s_ref, w_ref, b_ref, o_ref):
    c = s_ref[...]
    s = (c * _sigmoid(c)).astype(BF16)
    o_ref[0] = _dot(s, w_ref[0].astype(BF16)) + b_ref[0]


def ada_mod(cond8, w_ada, b_ada):
    bn = 512
    n = 6 * D_MODEL
    return pl.pallas_call(
        _ada_body,
        grid=(DEPTH, n // bn),
        in_specs=[
            pl.BlockSpec((SUBLANES, D_MODEL), lambda l, j: (0, 0)),
            pl.BlockSpec((1, D_MODEL, bn), lambda l, j: (l, 0, j)),
            pl.BlockSpec((1, 1, bn), lambda l, j: (l, 0, j)),
        ],
        out_specs=pl.BlockSpec((1, SUBLANES, bn), lambda l, j: (l, 0, j)),
        out_shape=jax.ShapeDtypeStruct((DEPTH, SUBLANES, n), F32),
        compiler_params=_cparams(2),
        name="ada_mod",
    )(cond8, w_ada, b_ada.reshape(DEPTH, 1, n))


def _norm_mod_body(x_ref, g_ref, sc_ref, sh_ref, o_ref):
    x = x_ref[...]
    ms = jnp.mean(x * x, axis=-1, keepdims=True)
    y = x * lax.rsqrt(ms + NORM_EPS) * g_ref[...]
    o_ref[...] = (y * (1.0 + sc_ref[...]) + sh_ref[...]).astype(o_ref.dtype)


def norm_mod(x, g, sc, sh, out_dtype=BF16):
    t = x.shape[0]
    tm = 256
    row = pl.BlockSpec((1, D_MODEL), lambda i: (0, 0))
    return pl.pallas_call(
        _norm_mod_body,
        grid=(t // tm,),
        in_specs=[pl.BlockSpec((tm, D_MODEL), lambda i: (i, 0)), row, row, row],
        out_specs=pl.BlockSpec((tm, D_MODEL), lambda i: (i, 0)),
        out_shape=jax.ShapeDtypeStruct((t, D_MODEL), out_dtype),
        compiler_params=_cparams(1),
        name="norm_mod",
    )(x, g, sc, sh)


def _final_norm_body(x_ref, g_ref, o_ref):
    x = x_ref[...]
    ms = jnp.mean(x * x, axis=-1, keepdims=True)
    o_ref[...] = x * lax.rsqrt(ms + NORM_EPS) * g_ref[...]


def final_norm(x, g):
    t = x.shape[0]
    tm = 256
    return pl.pallas_call(
        _final_norm_body,
        grid=(t // tm,),
        in_specs=[pl.BlockSpec((tm, D_MODEL), lambda i: (i, 0)), pl.BlockSpec((1, D_MODEL), lambda i: (0, 0))],
        out_specs=pl.BlockSpec((tm, D_MODEL), lambda i: (i, 0)),
        out_shape=jax.ShapeDtypeStruct((t, D_MODEL), F32),
        compiler_params=_cparams(1),
        name="final_norm",
    )(x, g)


def _mm_body(*refs, n_in, residual):
    a_refs = refs[:n_in]
    w_refs = refs[n_in:2 * n_in]
    rest = refs[2 * n_in:]
    acc = _dot(a_refs[0][...], w_refs[0][...])
    for a_ref, w_ref in zip(a_refs[1:], w_refs[1:]):
        acc = acc + _dot(a_ref[...], w_ref[...])
    if residual:
        x_ref, ga_ref, o_ref = rest
        o_ref[...] = x_ref[...] + ga_ref[...] * acc
    else:
        (o_ref,) = rest
        o_ref[...] = acc.astype(o_ref.dtype)


def matmul(a_list, w_list, bn, out_dtype, residual=None, name="matmul"):
    m = a_list[0].shape[0]
    n = w_list[0].shape[1]
    bm = 1024 if m % 1024 == 0 else 256
    in_specs = [pl.BlockSpec((bm, a.shape[1]), lambda i, j: (i, 0)) for a in a_list]
    in_specs += [pl.BlockSpec((w.shape[0], bn), lambda i, j: (0, j)) for w in w_list]
    args = list(a_list) + list(w_list)
    if residual is not None:
        in_specs += [pl.BlockSpec((bm, bn), lambda i, j: (i, j)), pl.BlockSpec((1, bn), lambda i, j: (0, j))]
        args += list(residual)
    return pl.pallas_call(
        functools.partial(_mm_body, n_in=len(a_list), residual=residual is not None),
        grid=(m // bm, n // bn),
        in_specs=in_specs,
        out_specs=pl.BlockSpec((bm, bn), lambda i, j: (i, j)),
        out_shape=jax.ShapeDtypeStruct((m, n), out_dtype),
        compiler_params=_cparams(2),
        name=name,
    )(*args)


RW_TM = 128


def _seg_sum_bcast(x, e_ref, et_ref):
    hi, lo = _split2(x)
    s = _dot(hi, e_ref[...]) + _dot(lo, e_ref[...])
    shi, slo = _split2(s)
    return _dot(shi, et_ref[...]) + _dot(slo, et_ref[...])


def _rw_prep_body(u_ref, up_ref, un_ref, taps_ref, w0_ref, a0_ref, w2_ref, a2_ref, g2_ref, kks_ref, ka_ref,
                  rk_ref, e_ref, et_ref, tri_ref, sel_ref,
                  v_ref, g_ref, bv_ref, rt_ref, kkt_ref, kbar_ref, bbar_ref, gam_ref):
    i = pl.program_id(0)
    nt = pl.num_programs(0)
    tm = RW_TM
    rows = lax.broadcasted_iota(jnp.int32, (tm, 1), 0)
    has_prev = (i > 0).astype(F32)
    has_next = (i < nt - 1).astype(F32)

    def conv(c0, c1):
        u = u_ref[:, c0:c1]
        prev_row = up_ref[SUBLANES - 1:SUBLANES, c0:c1] * has_prev
        next_row = un_ref[0:1, c0:c1] * has_next
        u_prev = jnp.where(rows == 0, prev_row, pltpu.roll(u, 1, 0))
        u_next = jnp.where(rows == tm - 1, next_row, pltpu.roll(u, tm - 1, 0))
        return (u_prev * taps_ref[0:1, c0:c1] + u * taps_ref[1:2, c0:c1] + u_next * taps_ref[2:3, c0:c1])

    r = conv(0, RW_W)
    k = conv(RW_W, 2 * RW_W)
    v = conv(2 * RW_W, 3 * RW_W)
    gd = conv(3 * RW_W, LORA_OFF)
    lo_in = conv(LORA_OFF, LORA_OFF + LORA_ALL)

    v_ref[...] = v.astype(v_ref.dtype)
    g_ref[...] = _dot(_sigmoid(gd).astype(BF16), g2_ref[...])

    kk = k * kks_ref[...]
    ssq = _seg_sum_bcast(kk * kk, e_ref, et_ref)
    kkn = kk / jnp.maximum(jnp.sqrt(ssq), 1e-12)

    tanh_lo = jnp.tanh(lo_in).astype(BF16)
    raw_lo = lo_in.astype(BF16)
    bonus = jnp.zeros((tm, RW_W), F32)
    for d in range(2):
        w_pre = w0_ref[d:d + 1, :] + _dot(tanh_lo, w2_ref[d])
        z = -w_pre
        softplus = jnp.maximum(z, 0.0) + jnp.log(1.0 + jnp.exp(-jnp.abs(z)))
        w_log = -softplus - 0.5
        lw = -jnp.exp(w_log)
        a = _sigmoid(a0_ref[d:d + 1, :] + _dot(raw_lo, a2_ref[d]))
        kd = k * (1.0 + (a - 1.0) * ka_ref[...])
        b = kkn * a
        bonus = bonus + r * kd * rk_ref[...]
        l1, l2, l3 = _split3(lw)
        tri = tri_ref[d]
        sel = sel_ref[...]
        cum = _dot(tri, l1) + _dot(tri, l2) + _dot(tri, l3)
        tot = _dot(sel, l1) + _dot(sel, l2) + _dot(sel, l3)
        e_inv = jnp.exp(-cum)
        rt_ref[d] = (r * jnp.exp(cum)).astype(BF16)
        kkt_ref[d] = (kkn * jnp.exp(cum - lw)).astype(BF16)
        kbar_ref[d] = (kd * e_inv).T.astype(BF16)
        bbar_ref[d] = (b * e_inv).T.astype(BF16)
        gam_ref[d] = jnp.exp(tot)
    bv_ref[...] = _seg_sum_bcast(bonus, e_ref, et_ref) * v


def _rw_consts():
    tm = RW_TM
    t = jnp.arange(tm)
    same = (t[:, None] // CHUNK) == (t[None, :] // CHUNK)
    tri_f = same & (t[None, :] <= t[:, None])
    tri_b = same & (t[None, :] >= t[:, None])
    tri = jnp.stack([tri_f, tri_b]).astype(BF16)
    sel = ((t[None, :] // CHUNK) == jnp.arange(SUBLANES)[:, None]).astype(BF16)
    lane_head = jnp.arange(RW_W) // RW_HEAD
    e = (lane_head[:, None] == jnp.arange(LANES)[None, :]).astype(BF16)
    return tri, sel, e, e.T


def rw_prep(u, prm):
    t = u.shape[0]
    tm = RW_TM
    nt = t // tm
    hb = tm // SUBLANES
    tri, sel, e, et = _rw_consts()
    full = lambda shp: pl.BlockSpec(shp, lambda i: (0,) * len(shp))
    in_specs = [
        pl.BlockSpec((tm, RW_COLS_PAD), lambda i: (i, 0)),
        pl.BlockSpec((SUBLANES, RW_COLS_PAD), lambda i: (jnp.maximum(i * hb - 1, 0), 0)),
        pl.BlockSpec((SUBLANES, RW_COLS_PAD), lambda i: (jnp.minimum((i + 1) * hb, nt * hb - 1), 0)),
        full((SUBLANES, RW_COLS_PAD)),
        full((2, RW_W)), full((2, RW_W)),
        full((2, LORA_ALL, RW_W)), full((2, LORA_ALL, RW_W)),
        full((LORA_G, RW_W)),
        full((1, RW_W)), full((1, RW_W)), full((1, RW_W)),
        full((RW_W, LANES)), full((LANES, RW_W)),
        full((2, tm, tm)), full((SUBLANES, tm)),
    ]
    tok = lambda dt: jax.ShapeDtypeStruct((t, RW_W), dt)
    tok2 = lambda dt: jax.ShapeDtypeStruct((2, t, RW_W), dt)
    spec1 = pl.BlockSpec((tm, RW_W), lambda i: (i, 0))
    spec2 = pl.BlockSpec((2, tm, RW_W), lambda i: (0, i, 0))
    chan2 = jax.ShapeDtypeStruct((2, RW_W, t), BF16)
    spec2t = pl.BlockSpec((2, RW_W, tm), lambda i: (0, 0, i))
    out_shape = [tok(BF16), tok(F32), tok(F32), tok2(BF16), tok2(BF16), chan2, chan2,
                 jax.ShapeDtypeStruct((2, nt * SUBLANES, RW_W), F32)]
    out_specs = [spec1, spec1, spec1, spec2, spec2, spec2t, spec2t,
                 pl.BlockSpec((2, SUBLANES, RW_W), lambda i: (0, i, 0))]
    return pl.pallas_call(
        _rw_prep_body,
        grid=(nt,),
        in_specs=in_specs,
        out_specs=out_specs,
        out_shape=out_shape,
        compiler_params=_cparams(1),
        name="rw_prep",
    )(u, u, u, prm["taps"], prm["w0"], prm["a0"], prm["w2"], prm["a2"], prm["g2"], prm["kks"], prm["ka"],
      prm["rk"], e, et, tri, sel)


RW_PAIRS = RW_HEADS // 2
SCAN_G = 8
SC_T = RW_TM


def _bmm(a, b):
    return lax.dot_general(a, b, (((2,), (1,)), ((0,), (0,))), preferred_element_type=F32)


def _rw_scan_body(rt_ref, kkt_ref, v_ref, kbt_ref, bbt_ref, gam_ref, h0_ref, y_ref, hfin_ref, h_scr, *, reverse):
    c = pl.program_id(1)
    nc = pl.num_programs(1)

    @pl.when(c == 0)
    def _():
        h_scr[...] = h0_ref[...]

    gn = SCAN_G
    n = SC_T
    row = lax.broadcasted_iota(jnp.int32, (n, n), 0)
    col = lax.broadcasted_iota(jnp.int32, (n, n), 1)
    same = (row // CHUNK) == (col // CHUNK)
    earlier = (col > row) if reverse else (col < row)
    strict = same & earlier
    incl = same & (earlier | (col == row))
    eye = (row == col).astype(F32)
    level_masks = []
    b = 1
    while b < CHUNK:
        level_masks.append(((row // (2 * b)) == (col // (2 * b))) & ((row // b) != (col // b)))
        b *= 2
    head_a = (col < RW_HEAD).astype(BF16)
    head_b = (col >= RW_HEAD).astype(BF16)

    lanes = lambda g: slice(g * n, (g + 1) * n)
    rt_p = [rt_ref[0, :, lanes(g)] for g in range(gn)]
    kk_p = [kkt_ref[0, :, lanes(g)] for g in range(gn)]
    v_p = [v_ref[:, lanes(g)] for g in range(gn)]
    kbt = kbt_ref[0].reshape(gn, n, n)
    bbt = bbt_ref[0].reshape(gn, n, n)

    lhs = jnp.stack([jnp.concatenate([rt_p[g] * head_a, kk_p[g] * head_a, rt_p[g] * head_b, kk_p[g] * head_b],
                                     axis=0) for g in range(gn)])
    ak = _bmm(lhs, kbt)
    ab = _bmm(lhs, bbt)
    per_head = lambda x, i: jnp.concatenate([x[:, i * n:(i + 1) * n], x[:, (i + 2) * n:(i + 3) * n]], axis=0)
    ark = jnp.where(incl, per_head(ak, 0), 0.0)
    lk = jnp.where(strict, per_head(ak, 1), 0.0)
    arb = jnp.where(incl, per_head(ab, 0), 0.0)
    lb = jnp.where(strict, per_head(ab, 1), 0.0)

    x = eye - jnp.where(level_masks[0], lb, 0.0)
    for mask in level_masks[1:]:
        xb = x.astype(BF16)
        off = jnp.where(mask, lb, 0.0).astype(BF16)
        x = x - _bmm(xb, _bmm(off, xb).astype(BF16))
    tb = x.astype(BF16)

    kkm = jnp.stack([kk_p[g] * head_a for g in range(gn)] + [kk_p[g] * head_b for g in range(gn)])
    vm = jnp.stack([v_p[g] * head_a for g in range(gn)] + [v_p[g] * head_b for g in range(gn)])
    wk = _bmm(tb, kkm)
    u0 = _bmm(tb, _bmm(lk.astype(BF16), vm).astype(BF16))
    arbb = arb.astype(BF16)
    qw = _bmm(arbb, wk.astype(BF16))
    y0 = _bmm(jnp.concatenate([ark.astype(BF16), -arbb], axis=2),
              jnp.concatenate([vm, u0.astype(BF16)], axis=1))
    pair = lambda t: t[:gn] + t[gn:]
    wk_p = pair(wk).astype(BF16)
    u0_p = pair(u0).astype(BF16)
    qhat = (jnp.stack(rt_p).astype(F32) - pair(qw)).astype(BF16)
    y0_p = pair(y0)
    vu = jnp.concatenate([jnp.stack(v_p), u0_p], axis=1)

    h = h_scr[...]
    chunks = range(n // CHUNK)
    for j in (reversed(chunks) if reverse else chunks):
        in_chunk = ((col // CHUNK) == j).astype(BF16)
        kj = kbt * in_chunk
        bj = bbt * in_chunk
        hloc = jnp.where(same, _bmm(jnp.concatenate([kj, -bj], axis=2), vu), 0.0)
        wb = jnp.where(same, _bmm(bj, wk_p), 0.0).astype(BF16)
        gamma = jnp.stack([jnp.broadcast_to(gam_ref[0, j:j + 1, lanes(g)], (n, n)).T for g in range(gn)])
        hb = h.astype(BF16)
        rows = slice(j * CHUNK, (j + 1) * CHUNK)
        yj = _bmm(qhat[:, rows], hb) + y0_p[:, rows]
        for g in range(gn):
            y_ref[rows, lanes(g)] = yj[g]
        h = gamma * (h - _bmm(wb, hb) + hloc)
    h_scr[...] = h

    @pl.when(c == nc - 1)
    def _():
        hfin_ref[...] = h


def rw_scan(rt, kkt, v, kbar_t, bbar_t, gam, h0, d):
    t = v.shape[0]
    nsc = t // SC_T
    reverse = d == 1
    gw = SCAN_G * LANES
    tix = (lambda c: nsc - 1 - c) if reverse else (lambda c: c)
    tm_spec = pl.BlockSpec((1, SC_T, gw), lambda p, c: (d, tix(c), p))
    cm_spec = pl.BlockSpec((1, gw, SC_T), lambda p, c: (d, p, tix(c)))
    st_spec = pl.BlockSpec((SCAN_G, LANES, LANES), lambda p, c: (p, 0, 0))
    return pl.pallas_call(
        functools.partial(_rw_scan_body, reverse=reverse),
        grid=(RW_PAIRS // SCAN_G, nsc),
        in_specs=[tm_spec, tm_spec, pl.BlockSpec((SC_T, gw), lambda p, c: (tix(c), p)), cm_spec, cm_spec,
                  pl.BlockSpec((1, SUBLANES, gw), lambda p, c: (d, tix(c), p)), st_spec],
        out_specs=[pl.BlockSpec((SC_T, gw), lambda p, c: (tix(c), p)), st_spec],
        out_shape=[jax.ShapeDtypeStruct((t, RW_W), F32),
                   jax.ShapeDtypeStruct((RW_PAIRS, LANES, LANES), F32)],
        scratch_shapes=[pltpu.VMEM((SCAN_G, LANES, LANES), F32)],
        compiler_params=_cparams(2),
        name="rw_scan_rev" if reverse else "rw_scan_fwd",
    )(rt, kkt, v, kbar_t, bbar_t, gam, h0)


def _rw_post_body(yf_ref, yb_ref, bv_ref, g_ref, lg_ref, lb_ref, e_ref, et_ref, o_ref):
    y = yf_ref[...] + yb_ref[...]
    mu = _seg_sum_bcast(y, e_ref, et_ref) * (1.0 / RW_HEAD)
    dlt = y - mu
    var = _seg_sum_bcast(dlt * dlt, e_ref, et_ref) * (1.0 / RW_HEAD)
    yn = dlt * lax.rsqrt(var + GN_EPS)
    o_ref[...] = ((yn * lg_ref[...] + lb_ref[...] + bv_ref[...]) * g_ref[...]).astype(o_ref.dtype)


def rw_post(yf, yb, bv, g, lnx_g, lnx_b):
    t = yf.shape[0]
    tm = 256
    _, _, e, et = _rw_consts()
    tokspec = pl.BlockSpec((tm, RW_W), lambda i: (i, 0))
    row = pl.BlockSpec((1, RW_W), lambda i: (0, 0))
    return pl.pallas_call(
        _rw_post_body,
        grid=(t // tm,),
        in_specs=[tokspec, tokspec, tokspec, tokspec, row, row,
                  pl.BlockSpec((RW_W, LANES), lambda i: (0, 0)), pl.BlockSpec((LANES, RW_W), lambda i: (0, 0))],
        out_specs=tokspec,
        out_shape=jax.ShapeDtypeStruct((t, RW_W), BF16),
        compiler_params=_cparams(1),
        name="rw_post",
    )(yf, yb, bv, g, lnx_g, lnx_b, e, et)


def rwkv_group(u, prm, h0_f, h0_b, with_output):
    v, g, bv, rt, kkt, kbar_t, bbar_t, gam = rw_prep(u, prm)
    outs = []
    finals = []
    for d, h0 in ((0, h0_f), (1, h0_b)):
        y, hfin = rw_scan(rt, kkt, v, kbar_t, bbar_t, gam, h0, d)
        outs.append(y)
        finals.append(hfin)
    if not with_output:
        return None, finals[0], finals[1]
    out = rw_post(outs[0], outs[1], bv, g, prm["lnx_g"], prm["lnx_b"])
    return out, finals[0], finals[1]


NA_RQ = 4


def _na_body(q_ref, k_ref, v_ref, kc_ref, vc_ref, bt_ref, o_ref, *, n_rows):
    rb = pl.program_id(1)
    scale = NA_HEAD ** -0.5
    kc = kc_ref[...]
    vc = vc_ref[...]
    for qi in range(NA_RQ):
        r = rb * NA_RQ + qi
        start = jnp.clip(r - NA_KH // 2, 0, n_rows - NA_KH)
        dr0 = start - r + NA_KH - 1
        tok0 = pl.multiple_of(start * GRID_W, GRID_W)
        q = q_ref[qi * GRID_W:(qi + 1) * GRID_W, :]
        kw = k_ref[pl.ds(tok0, NA_KH * GRID_W), :]
        vw = v_ref[pl.ds(tok0, NA_KH * GRID_W), :]
        s = _dot_nt(q, kw) * scale
        bias = jnp.concatenate([bt_ref[0, dr0 + 2 * m] for m in range(NA_KH // 2)], axis=1)
        s = s + bias
        sc = _dot_nt(q, kc) * scale
        mx = jnp.maximum(jnp.max(s, axis=-1, keepdims=True), jnp.max(sc, axis=-1, keepdims=True))
        p = jnp.exp(s - mx)
        pc = jnp.exp(sc - mx)
        den = jnp.sum(p, axis=-1, keepdims=True) + jnp.sum(pc, axis=-1, keepdims=True)
        o = (_dot(p.astype(BF16), vw) + _dot(pc.astype(BF16), vc)) / den
        o_ref[qi * GRID_W:(qi + 1) * GRID_W, :] = o.astype(o_ref.dtype)


def _na_bias_table(rpb):
    cols = jnp.arange(GRID_W)
    col_start = jnp.clip(cols - NA_KW // 2, 0, GRID_W - NA_KW)
    col_ok = (cols[None, :] >= col_start[:, None]) & (cols[None, :] < col_start[:, None] + NA_KW)
    dc = jnp.clip(cols[None, :] - cols[:, None] + NA_KW - 1, 0, 2 * NA_KW - 2)
    bt = rpb.astype(F32)[:, :, dc]
    bt = jnp.where(col_ok[None, None], bt, NEG_INF)
    return jnp.concatenate([bt[:, :-1], bt[:, 1:]], axis=-1)


def na_latent(att_l, att_c, rpb):
    s_len = att_l.shape[0]
    n_rows = s_len // GRID_W
    qb = NA_W // NA_HEAD
    bt = _na_bias_table(rpb)
    return pl.pallas_call(
        functools.partial(_na_body, n_rows=n_rows),
        grid=(NA_HEADS, n_rows // NA_RQ),
        in_specs=[
            pl.BlockSpec((NA_RQ * GRID_W, NA_HEAD), lambda h, r: (r, h)),
            pl.BlockSpec((s_len, NA_HEAD), lambda h, r: (0, qb + h)),
            pl.BlockSpec((s_len, NA_HEAD), lambda h, r: (0, 2 * qb + h)),
            pl.BlockSpec((CTX_LEN, NA_HEAD), lambda h, r: (0, qb + h)),
            pl.BlockSpec((CTX_LEN, NA_HEAD), lambda h, r: (0, 2 * qb + h)),
            pl.BlockSpec((1, 2 * NA_KH - 2, GRID_W, 2 * GRID_W), lambda h, r: (h, 0, 0, 0)),
        ],
        out_specs=pl.BlockSpec((NA_RQ * GRID_W, NA_HEAD), lambda h, r: (r, h)),
        out_shape=jax.ShapeDtypeStruct((s_len, NA_W), BF16),
        compiler_params=_cparams(2),
        name="na_latent",
    )(att_l, att_l, att_l, att_c, att_c, bt)


def _na_ctx_body(q_ref, k_ref, v_ref, o_ref):
    s = _dot_nt(q_ref[...], k_ref[...]) * (NA_HEAD ** -0.5)
    mx = jnp.max(s, axis=-1, keepdims=True)
    p = jnp.exp(s - mx)
    den = jnp.sum(p, axis=-1, keepdims=True)
    o_ref[...] = (_dot(p.astype(BF16), v_ref[...]) / den).astype(o_ref.dtype)


def na_ctx(att_c):
    qb = NA_W // NA_HEAD
    blk = lambda off: pl.BlockSpec((CTX_LEN, NA_HEAD), lambda h: (0, off + h))
    return pl.pallas_call(
        _na_ctx_body,
        grid=(NA_HEADS,),
        in_specs=[blk(0), blk(qb), blk(2 * qb)],
        out_specs=blk(0),
        out_shape=jax.ShapeDtypeStruct((CTX_LEN, NA_W), BF16),
        compiler_params=_cparams(1),
        name="na_ctx",
    )(att_c, att_c, att_c)


def _rope_body(x_ref, c_ref, s1_ref, s2_ref, o_ref):
    x = x_ref[...].astype(F32)
    quarter = SW_HEAD // 4
    x_up = pltpu.roll(x, LANES - quarter, 1)
    x_dn = pltpu.roll(x, quarter, 1)
    o_ref[...] = (x * c_ref[...] + x_up * s1_ref[...] + x_dn * s2_ref[...]).astype(o_ref.dtype)


def _rope_tables(s_len):
    t = jnp.arange(s_len)
    row = (t // GRID_W).astype(F32)
    col = (t % GRID_W).astype(F32)
    half = SW_HEAD // 2
    inv = ROPE_BASE ** (-jnp.arange(0, half, 2, dtype=F32) / half)
    ang_r = row[:, None] * inv[None, :]
    ang_c = col[:, None] * inv[None, :]
    cos = jnp.concatenate([jnp.cos(ang_r)] * 2 + [jnp.cos(ang_c)] * 2, axis=-1)
    sin_r, sin_c = jnp.sin(ang_r), jnp.sin(ang_c)
    zero = jnp.zeros_like(sin_r)
    s_up = jnp.concatenate([-sin_r, zero, -sin_c, zero], axis=-1)
    s_dn = jnp.concatenate([zero, sin_r, zero, sin_c], axis=-1)
    tile2 = lambda a: jnp.concatenate([a, a], axis=-1)
    return tile2(cos), tile2(s_up), tile2(s_dn)


def rope_qk(att_l):
    s_len = att_l.shape[0]
    tm = 512
    nblk = (SW_W + SW_KV * SW_HEAD) // LANES
    off = NA_COLS // LANES
    cos, s_up, s_dn = _rope_tables(s_len)
    tab = pl.BlockSpec((tm, LANES), lambda i, j: (i, 0))
    return pl.pallas_call(
        _rope_body,
        grid=(s_len // tm, nblk),
        in_specs=[pl.BlockSpec((tm, LANES), lambda i, j: (i, off + j)), tab, tab, tab],
        out_specs=pl.BlockSpec((tm, LANES), lambda i, j: (i, j)),
        out_shape=jax.ShapeDtypeStruct((s_len, nblk * LANES), BF16),
        compiler_params=_cparams(2),
        name="rope_qk",
    )(att_l, cos, s_up, s_dn)


def _sw_softmax_out(q8, s_parts, v_parts, sink_col):
    mx = sink_col
    for s in s_parts:
        mx = jnp.maximum(mx, jnp.max(s, axis=-1, keepdims=True))
    den = jnp.exp(sink_col - mx)
    o = None
    for s, v in zip(s_parts, v_parts):
        p = jnp.exp(s - mx)
        den = den + jnp.sum(p, axis=-1, keepdims=True)
        pv = _dot(p.astype(BF16), v)
        o = pv if o is None else o + pv
    return o / den


def _sink_column(sink_ref, kh, rows_per_head):
    n = SW_GROUP * rows_per_head
    grp = lax.broadcasted_iota(jnp.int32, (n, 1), 0) // rows_per_head
    col = jnp.zeros((n, 1), F32)
    for g in range(SW_GROUP):
        col = jnp.where(grp == g, sink_ref[kh * SW_GROUP + g], col)
    return col


def _sw_body(sink_ref, q_ref, k_ref, v_ref, kc_ref, vc_ref, o_ref, *, s_len):
    n = pl.program_id(0)
    scale = SW_HEAD ** -0.5
    win = 3 * SW_BLOCK
    start = pl.multiple_of(jnp.clip((n - 1) * SW_BLOCK, 0, s_len - win), SW_BLOCK)
    k2 = k_ref[pl.ds(start, win), :]
    v2 = v_ref[pl.ds(start, win), :]
    qpos = n * SW_BLOCK + lax.broadcasted_iota(jnp.int32, (SW_BLOCK, win), 0)
    kpos = start + lax.broadcasted_iota(jnp.int32, (SW_BLOCK, win), 1)
    ok = jnp.abs(kpos - qpos) <= SW_WIN
    for kh in range(SW_KV):
        lanes = slice(kh * SW_HEAD, (kh + 1) * SW_HEAD)
        q8 = jnp.concatenate(
            [q_ref[:, (kh * SW_GROUP + g) * SW_HEAD:(kh * SW_GROUP + g + 1) * SW_HEAD] for g in range(SW_GROUP)],
            axis=0)
        s_loc = _dot_nt(q8, k2[:, lanes]) * scale
        s_loc = jnp.where(ok[None], s_loc.reshape(SW_GROUP, SW_BLOCK, win), NEG_INF).reshape(
            SW_GROUP * SW_BLOCK, win)
        s_ctx = _dot_nt(q8, kc_ref[:, lanes]) * scale
        o = _sw_softmax_out(q8, [s_loc, s_ctx], [v2[:, lanes], vc_ref[:, lanes]],
                            _sink_column(sink_ref, kh, SW_BLOCK))
        for g in range(SW_GROUP):
            c0 = (kh * SW_GROUP + g) * SW_HEAD
            o_ref[:, c0:c0 + SW_HEAD] = o[g * SW_BLOCK:(g + 1) * SW_BLOCK].astype(o_ref.dtype)


def swa_latent(qk_rot, att_l, att_c, sink):
    s_len = qk_rot.shape[0]
    kcol = SW_W // LANES
    ck = (NA_COLS + SW_W) // LANES
    return pl.pallas_call(
        functools.partial(_sw_body, s_len=s_len),
        grid=(s_len // SW_BLOCK,),
        in_specs=[
            pl.BlockSpec(memory_space=pltpu.SMEM),
            pl.BlockSpec((SW_BLOCK, SW_W), lambda n: (n, 0)),
            pl.BlockSpec((s_len, LANES), lambda n: (0, kcol)),
            pl.BlockSpec((s_len, LANES), lambda n: (0, ck + 1)),
            pl.BlockSpec((CTX_LEN, LANES), lambda n: (0, ck)),
            pl.BlockSpec((CTX_LEN, LANES), lambda n: (0, ck + 1)),
        ],
        out_specs=pl.BlockSpec((SW_BLOCK, SW_W), lambda n: (n, 0)),
        out_shape=jax.ShapeDtypeStruct((s_len, SW_W), BF16),
        compiler_params=_cparams(1),
        name="swa_latent",
    )(sink, qk_rot, qk_rot, att_l, att_c, att_c)


def _sw_ctx_body(sink_ref, q_ref, k_ref, v_ref, o_ref):
    scale = SW_HEAD ** -0.5
    for kh in range(SW_KV):
        lanes = slice(kh * SW_HEAD, (kh + 1) * SW_HEAD)
        q8 = jnp.concatenate(
            [q_ref[:, (kh * SW_GROUP + g) * SW_HEAD:(kh * SW_GROUP + g + 1) * SW_HEAD] for g in range(SW_GROUP)],
            axis=0)
        s = _dot_nt(q8, k_ref[:, lanes]) * scale
        o = _sw_softmax_out(q8, [s], [v_ref[:, lanes]], _sink_column(sink_ref, kh, CTX_LEN))
        for g in range(SW_GROUP):
            c0 = (kh * SW_GROUP + g) * SW_HEAD
            o_ref[:, c0:c0 + SW_HEAD] = o[g * CTX_LEN:(g + 1) * CTX_LEN].astype(o_ref.dtype)


def swa_ctx(att_c, sink):
    qo = NA_COLS // SW_W
    ck = (NA_COLS + SW_W) // LANES
    return pl.pallas_call(
        _sw_ctx_body,
        grid=(1,),
        in_specs=[
            pl.BlockSpec(memory_space=pltpu.SMEM),
            pl.BlockSpec((CTX_LEN, SW_W), lambda n: (0, qo)),
            pl.BlockSpec((CTX_LEN, LANES), lambda n: (0, ck)),
            pl.BlockSpec((CTX_LEN, LANES), lambda n: (0, ck + 1)),
        ],
        out_specs=pl.BlockSpec((CTX_LEN, SW_W), lambda n: (0, 0)),
        out_shape=jax.ShapeDtypeStruct((CTX_LEN, SW_W), BF16),
        compiler_params=_cparams(1),
        name="swa_ctx",
    )(sink, att_c, att_c, att_c)


def _router_body(x_ref, g_ref, sc_ref, sh_ref, wr_ref, rb_ref, h_ref, pick_ref, hp_ref):
    x = x_ref[...]
    tm = x.shape[0]
    ms = jnp.mean(x * x, axis=-1, keepdims=True)
    h = (x * lax.rsqrt(ms + NORM_EPS) * g_ref[...]) * (1.0 + sc_ref[...]) + sh_ref[...]
    h_ref[...] = h.astype(h_ref.dtype)

    hh, hl = _split2(h)
    wh, wl = _split2(wr_ref[...])
    logits = _dot(hh, wh) + _dot(hl, wh) + _dot(hh, wl)
    lane = lax.broadcasted_iota(jnp.int32, (tm, LANES), 1)
    lane_f = lane.astype(F32)
    valid = lane < N_EXPERTS
    neg = -jnp.inf
    big = float(2 * LANES)
    scores = _sigmoid(logits)
    biased = jnp.where(valid, scores + rb_ref[...], neg)
    grp = lane // EXPERTS_PER_GROUP

    def first_argmax(vals):
        m = jnp.max(vals, axis=-1, keepdims=True)
        idx = jnp.min(jnp.where(vals == m, lane_f, big), axis=-1, keepdims=True)
        return m, idx

    gscore = jnp.full((tm, LANES), neg, F32)
    for gi in range(N_GROUPS):
        vg = jnp.where(grp == gi, biased, neg)
        m1, i1 = first_argmax(vg)
        m2 = jnp.max(jnp.where(lane_f == i1, neg, vg), axis=-1, keepdims=True)
        gscore = jnp.where(lane == gi * EXPERTS_PER_GROUP, m1 + m2, gscore)
    keep = jnp.zeros((tm, LANES), jnp.bool_)
    for _ in range(TOPK_GROUPS):
        _, gi = first_argmax(gscore)
        keep = keep | (grp.astype(F32) == jnp.floor(gi * (1.0 / EXPERTS_PER_GROUP)))
        gscore = jnp.where(lane_f == gi, neg, gscore)
    cur = jnp.where(valid, jnp.where(keep, biased, NEG_INF), neg)
    sel = jnp.zeros((tm, LANES), jnp.bool_)
    picks = jnp.zeros((tm, LANES), F32)
    for k in range(TOP_K):
        _, ei = first_argmax(cur)
        hit = lane_f == ei
        sel = sel | hit
        cur = jnp.where(hit, neg, cur)
        picks = jnp.where(lane == k, ei, picks)
        picks = jnp.where(lane == TOP_K + k, jnp.sum(jnp.where(hit, scores, 0.0), axis=-1, keepdims=True), picks)
    w = jnp.where(sel, scores, 0.0)
    norm = ROUTE_SCALE / jnp.sum(w, axis=-1, keepdims=True)
    pick_ref[...] = jnp.where((lane >= TOP_K) & (lane < 2 * TOP_K), picks * norm, picks)

    bits = pltpu.bitcast(h.astype(BF16).astype(F32), jnp.uint32)
    half = D_MODEL // 2
    words = (bits[:, half:] & jnp.uint32(0xFFFF0000)) | (bits[:, :half] >> 16)
    for a in range(TOK_ROWS):
        hp_ref[pl.ds(a, tm, stride=TOK_ROWS), :] = words[:, a * LANES:(a + 1) * LANES]


def router(x, g, sc, sh, w_router_pad, r_bias_pad):
    t = x.shape[0]
    tm = 256
    row = pl.BlockSpec((1, D_MODEL), lambda i: (0, 0))
    return pl.pallas_call(
        _router_body,
        grid=(t // tm,),
        in_specs=[pl.BlockSpec((tm, D_MODEL), lambda i: (i, 0)), row, row, row,
                  pl.BlockSpec((D_MODEL, LANES), lambda i: (0, 0)), pl.BlockSpec((1, LANES), lambda i: (0, 0))],
        out_specs=[pl.BlockSpec((tm, D_MODEL), lambda i: (i, 0)), pl.BlockSpec((tm, LANES), lambda i: (i, 0)),
                   pl.BlockSpec((tm * TOK_ROWS, LANES), lambda i: (i, 0))],
        out_shape=[jax.ShapeDtypeStruct((t, D_MODEL), BF16), jax.ShapeDtypeStruct((t, LANES), F32),
                   jax.ShapeDtypeStruct((t * TOK_ROWS, LANES), jnp.uint32)],
        compiler_params=_cparams(1),
        name="router",
    )(x, g, sc, sh, w_router_pad, r_bias_pad)


MOE_TM = 512
MOE_TM_SMALL = 64
PACK_W = D_MODEL // 2
TOK_ROWS = PACK_W // LANES
TOK_PITCH = TOK_ROWS + 4
MOE_DOWN_COLS = 2 * LANES


def _unpack_bf16_pairs(words):
    lo = pltpu.bitcast(words << 16, F32)
    hi = pltpu.bitcast(words & jnp.uint32(0xFFFF0000), F32)
    return lo, hi


def _pack_bf16_pairs(lo, hi):
    lo_bits = pltpu.bitcast(lo.astype(BF16).astype(F32), jnp.uint32) >> 16
    hi_bits = pltpu.bitcast(hi.astype(BF16).astype(F32), jnp.uint32) & jnp.uint32(0xFFFF0000)
    return hi_bits | lo_bits


def _routed_body(te_ref, nv_ref, src_ref, dst_ref, hp_hbm, wg_ref, wu_ref, wd_ref, yk_hbm, xbuf, ybuf, gsem, ssem,
                 *, tm, n_real_rows):
    s = pl.program_id(0)
    nv = nv_ref[0]

    tile_rows = tm * TOK_ROWS

    def gather_wait(slot):
        pltpu.make_async_copy(hp_hbm.at[pl.ds(0, tile_rows)], xbuf.at[slot, pl.ds(0, tile_rows)],
                              gsem.at[slot]).wait()

    def scatter_wait(slot):
        pltpu.make_async_copy(ybuf.at[slot, pl.ds(0, tile_rows)], yk_hbm.at[pl.ds(0, tile_rows)],
                              ssem.at[slot]).wait()

    @pl.when(s == 0)
    def _():
        ybuf[1, pl.ds(0, tile_rows), :] = jnp.zeros((tile_rows, LANES), jnp.uint32)
        fill = pltpu.make_async_copy(ybuf.at[1, pl.ds(0, tile_rows)],
                                     yk_hbm.at[pl.ds(n_real_rows * TOK_ROWS, tile_rows)], ssem.at[1])
        fill.start()
        fill.wait()

    @pl.when(s < nv)
    def _():
        slot = s % 2
        for r in range(tm):
            pltpu.make_async_copy(hp_hbm.at[pl.ds(src_ref[0, 0, r], TOK_ROWS)],
                                  xbuf.at[slot, pl.ds(r * TOK_PITCH, TOK_ROWS)], gsem.at[slot]).start(priority=r % 2)

    @pl.when((s >= 1) & (s <= nv))
    def _():
        t = s - 1
        slot = t % 2
        gather_wait(slot)
        hg = jnp.zeros((tm, D_EXPERT), F32)
        hu = jnp.zeros((tm, D_EXPERT), F32)
        for a in range(TOK_ROWS):
            x_lo, x_hi = _unpack_bf16_pairs(xbuf[slot, pl.ds(a, tm, stride=TOK_PITCH), :])
            x_a = jnp.concatenate([x_lo.astype(BF16), x_hi.astype(BF16)], axis=1)
            lo_rows = slice(a * LANES, (a + 1) * LANES)
            hi_rows = slice(PACK_W + a * LANES, PACK_W + (a + 1) * LANES)
            hg = hg + _dot(x_a, jnp.concatenate([wg_ref[0, 0, lo_rows], wg_ref[0, 0, hi_rows]], axis=0).astype(BF16))
            hu = hu + _dot(x_a, jnp.concatenate([wu_ref[0, 0, lo_rows], wu_ref[0, 0, hi_rows]], axis=0).astype(BF16))
        act = ((hg * _sigmoid(hg)) * hu).astype(BF16)
        for c in range(PACK_W // MOE_DOWN_COLS):
            c0 = c * MOE_DOWN_COLS
            y_lo = _dot(act, wd_ref[0, 0, :, c0:c0 + MOE_DOWN_COLS].astype(BF16))
            y_hi = _dot(act, wd_ref[0, 0, :, PACK_W + c0:PACK_W + c0 + MOE_DOWN_COLS].astype(BF16))
            words = _pack_bf16_pairs(y_lo, y_hi)
            for j in range(MOE_DOWN_COLS // LANES):
                a = c * (MOE_DOWN_COLS // LANES) + j
                ybuf[slot, pl.ds(a, tm, stride=TOK_PITCH), :] = words[:, j * LANES:(j + 1) * LANES]

        @pl.when(t >= 1)
        def _():
            scatter_wait(1 - slot)

        for r in range(tm):
            pltpu.make_async_copy(ybuf.at[slot, pl.ds(r * TOK_PITCH, TOK_ROWS)],
                                  yk_hbm.at[pl.ds(dst_ref[0, 0, r], TOK_ROWS)], ssem.at[slot]).start(priority=r % 2)

        @pl.when(s == nv)
        def _():
            scatter_wait(slot)


def _dispatch_plan(picks, t, tm):
    n_pairs = TOP_K * t
    n_tiles = n_pairs // tm + N_EXPERTS
    e_flat = picks[:, :TOP_K].astype(jnp.int32).reshape(-1)
    pair = jnp.arange(n_pairs, dtype=jnp.int32)
    dst = (pair % TOP_K) * t + pair // TOP_K
    experts = jnp.arange(N_EXPERTS, dtype=jnp.int32)
    counts = jnp.sum((e_flat[:, None] == experts[None, :]).astype(jnp.int32), axis=0)
    fill = (-counts) % tm
    filler_e = jnp.repeat(experts, tm)
    filler_j = jnp.tile(jnp.arange(tm, dtype=jnp.int32), N_EXPERTS)
    filler_key = jnp.where(filler_j < jnp.repeat(fill, tm), 2 * filler_e + 1, 2 * N_EXPERTS)
    dst_bits = (n_pairs + tm - 1).bit_length()
    assert 2 * N_EXPERTS < (1 << (31 - dst_bits))
    keys = jnp.concatenate([2 * e_flat, filler_key]) * (1 << dst_bits) + jnp.concatenate([dst, n_pairs + filler_j])
    dst_sorted = lax.sort(keys) & ((1 << dst_bits) - 1)
    src_sorted = jnp.where(dst_sorted < n_pairs, dst_sorted % t, 0)
    tile_end = jnp.cumsum((counts + fill) // tm)
    n_valid = tile_end[-1].astype(jnp.int32)
    ti = jnp.minimum(jnp.arange(n_tiles, dtype=jnp.int32), n_valid - 1)
    tile_e = jnp.sum((tile_end[None, :] <= ti[:, None]).astype(jnp.int32), axis=1)
    return (tile_e.astype(jnp.int32), n_valid.reshape(1), (src_sorted * TOK_ROWS).reshape(n_tiles, 1, tm),
            (dst_sorted * TOK_ROWS).reshape(n_tiles, 1, tm))


def moe_routed(hp, picks, layer, we_gate, we_up, we_down):
    t = hp.shape[0] // TOK_ROWS
    tm = MOE_TM if TOP_K * t >= N_EXPERTS * MOE_TM else MOE_TM_SMALL
    tile_e, n_valid, src, dst = _dispatch_plan(picks, t, tm)
    n_tiles = tile_e.shape[0]
    last = n_tiles - 1
    cur = lambda s: jnp.minimum(s, last)
    prev = lambda s: jnp.clip(s - 1, 0, last)
    smem = lambda f: pl.BlockSpec((1, 1, tm), lambda s, te, nv: (f(s), 0, 0), memory_space=pltpu.SMEM)
    wspec = lambda shp: pl.BlockSpec((1, 1) + shp, lambda s, te, nv: (layer, te[prev(s)], 0, 0))
    grid_spec = pltpu.PrefetchScalarGridSpec(
        num_scalar_prefetch=2,
        grid=(n_tiles + 1,),
        in_specs=[
            smem(cur), smem(prev),
            pl.BlockSpec(memory_space=pl.ANY),
            wspec((D_MODEL, D_EXPERT)), wspec((D_MODEL, D_EXPERT)), wspec((D_EXPERT, D_MODEL)),
        ],
        out_specs=pl.BlockSpec(memory_space=pl.ANY),
        scratch_shapes=[pltpu.VMEM((2, tm * TOK_PITCH, LANES), jnp.uint32),
                        pltpu.VMEM((2, tm * TOK_PITCH, LANES), jnp.uint32),
                        pltpu.SemaphoreType.DMA((2,)), pltpu.SemaphoreType.DMA((2,))],
    )
    return pl.pallas_call(
        functools.partial(_routed_body, tm=tm, n_real_rows=TOP_K * t),
        grid_spec=grid_spec,
        out_shape=jax.ShapeDtypeStruct(((TOP_K * t + tm) * TOK_ROWS, LANES), jnp.uint32),
        compiler_params=_cparams(1),
        name="moe_routed",
    )(tile_e, n_valid, src, dst, hp, we_gate, we_up, we_down)


COMBINE_TM = 128


def _combine_body(h_ref, pick_ref, *rest):
    yk_refs = rest[:TOP_K]
    wg_ref, wu_ref, wd_ref, x_ref, ga_ref, o_ref = rest[TOP_K:]
    h = h_ref[...]
    hg = _dot(h, wg_ref[...])
    hu = _dot(h, wu_ref[...])
    act = ((hg * _sigmoid(hg)) * hu).astype(BF16)
    picks = pick_ref[...]
    tm = h.shape[0]
    shared = _dot(act, wd_ref[...])
    for a in range(TOK_ROWS):
        acc_lo = shared[:, a * LANES:(a + 1) * LANES]
        acc_hi = shared[:, PACK_W + a * LANES:PACK_W + (a + 1) * LANES]
        for k in range(TOP_K):
            lo, hi = _unpack_bf16_pairs(yk_refs[k][pl.ds(a, tm, stride=TOK_ROWS), :])
            w_k = picks[:, TOP_K + k:TOP_K + k + 1]
            acc_lo = acc_lo + w_k * lo
            acc_hi = acc_hi + w_k * hi
        for base, acc in ((0, acc_lo), (PACK_W, acc_hi)):
            cols = slice(base + a * LANES, base + (a + 1) * LANES)
            o_ref[:, cols] = x_ref[:, cols] + ga_ref[:, cols] * acc


def moe_combine(h, picks, yk, ws_gate, ws_up, ws_down, x, ga):
    t = h.shape[0]
    tm = COMBINE_TM
    nb = t // tm
    tok = lambda cols: pl.BlockSpec((tm, cols), lambda i: (i, 0))
    full = lambda a: pl.BlockSpec(a.shape, lambda i: (0,) * a.ndim)
    slot = lambda k: pl.BlockSpec((tm * TOK_ROWS, LANES), lambda i: (k * nb + i, 0))
    return pl.pallas_call(
        _combine_body,
        grid=(nb,),
        in_specs=[tok(D_MODEL), tok(LANES)] + [slot(k) for k in range(TOP_K)] + [
            full(ws_gate), full(ws_up), full(ws_down), tok(D_MODEL), pl.BlockSpec((1, D_MODEL), lambda i: (0, 0))],
        out_specs=tok(D_MODEL),
        out_shape=jax.ShapeDtypeStruct((t, D_MODEL), F32),
        compiler_params=_cparams(1),
        name="moe_combine",
    )(h, picks, *([yk] * TOP_K), ws_gate, ws_up, ws_down, x, ga)


def _pad_cols(a, n):
    return jnp.pad(a, [(0, 0)] * (a.ndim - 1) + [(0, n - a.shape[-1])])


def _rw_params(i, rw_conv, rw_w0, rw_w2, rw_a0, rw_a2, rw_g2, rw_kk, rw_ka, rw_rk, rw_lnx_g, rw_lnx_b):
    w2 = jnp.zeros((2, LORA_ALL, RW_W), F32)
    a2 = jnp.zeros((2, LORA_ALL, RW_W), F32)
    for d in range(2):
        w2 = w2.at[d, d * LORA_W:(d + 1) * LORA_W].set(rw_w2[i, d])
        a2 = a2.at[d, 2 * LORA_W + d * LORA_A:2 * LORA_W + (d + 1) * LORA_A].set(rw_a2[i, d])
    taps = jnp.pad(rw_conv[i], ((0, SUBLANES - 3), (0, RW_COLS_PAD - RW_COLS)))
    return dict(
        taps=taps, w0=rw_w0[i], a0=rw_a0[i], w2=w2.astype(BF16), a2=a2.astype(BF16), g2=rw_g2[i].astype(BF16),
        kks=rw_kk[i][None], ka=rw_ka[i][None], rk=rw_rk[i].reshape(1, RW_W),
        lnx_g=rw_lnx_g[i][None], lnx_b=rw_lnx_b[i][None])


def kernel(x, c, ctx, c_ctx, w_ada, b_ada, norm1_g, norm2_g, w_in, rw_conv, rw_w0, rw_w2, rw_a0, rw_a2, rw_g2,
           rw_kk, rw_ka, rw_rk, rw_lnx_g, rw_lnx_b, na_rpb, sw_sink, w_out, w_router, router_bias, we_gate,
           we_up, we_down, ws_gate, ws_up, ws_down, final_g):
    assert x.shape[0] == 1 and x.shape[2] == D_MODEL and ctx.shape[1] == CTX_LEN
    xl = x[0]
    xc = ctx[0]
    cond8 = jnp.zeros((SUBLANES, D_MODEL), F32).at[0].set(c[0]).at[1].set(c_ctx)
    mods = ada_mod(cond8, w_ada, b_ada)

    for i in range(DEPTH):
        ctx_needed = i < DEPTH - 1
        mod_l = [m[None] for m in mods[i, 0].reshape(6, D_MODEL)]
        mod_c = [m[None] for m in mods[i, 1].reshape(6, D_MODEL)]
        w_rw = _pad_cols(w_in[i][:, :RW_COLS], RW_COLS_PAD).astype(BF16)
        w_att = w_in[i][:, RW_COLS:].astype(BF16)
        w_o = w_out[i].astype(BF16)
        w_o_parts = [w_o[:RW_W], w_o[RW_W:RW_W + NA_W], w_o[RW_W + NA_W:]]
        rw_prm = _rw_params(i, rw_conv, rw_w0, rw_w2, rw_a0, rw_a2, rw_g2, rw_kk, rw_ka, rw_rk, rw_lnx_g,
                            rw_lnx_b)
        n1 = norm1_g[i][None]
        n2 = norm2_g[i][None]

        hl = norm_mod(xl, n1, mod_l[1], mod_l[0])
        hc = norm_mod(xc, n1, mod_c[1], mod_c[0])
        url = matmul([hl], [w_rw], 768, F32, name="in_proj_rw")
        urc = matmul([hc], [w_rw], 768, F32, name="in_proj_rw_ctx")
        ual = matmul([hl], [w_att], 256, BF16, name="in_proj_att")
        uac = matmul([hc], [w_att], 256, BF16, name="in_proj_att_ctx")

        z = jnp.zeros((RW_PAIRS, LANES, LANES), F32)
        rwc, s_f, s_b = rwkv_group(urc, rw_prm, z, z, ctx_needed)
        rwl, _, _ = rwkv_group(url, rw_prm, s_f, s_b, True)
        nal = na_latent(ual, uac, na_rpb[i])
        swl = swa_latent(rope_qk(ual), ual, uac, sw_sink[i])
        xl = matmul([rwl, nal, swl], w_o_parts, 512, F32, residual=(xl, mod_l[2]), name="out_proj")

        wr = _pad_cols(w_router[i], LANES)
        rb = _pad_cols(router_bias[i][None], LANES)
        shared = (ws_gate[i].astype(BF16), ws_up[i].astype(BF16), ws_down[i].astype(BF16))

        def moe(xt, mod):
            h2, picks, hp = router(xt, n2, mod[4], mod[3], wr, rb)
            yk = moe_routed(hp, picks, i, we_gate, we_up, we_down)
            return moe_combine(h2, picks, yk, *shared, xt, mod[5])

        xl = moe(xl, mod_l)
        if ctx_needed:
            nac = na_ctx(uac)
            swc = swa_ctx(uac, sw_sink[i])
            xc = matmul([rwc, nac, swc], w_o_parts, 512, F32, residual=(xc, mod_c[2]), name="out_proj_ctx")
            xc = moe(xc, mod_c)
    return final_norm(xl, final_g[None])[None]
```

```python
import functools

import jax
import jax.numpy as jnp
from jax import lax
from jax.experimental import pallas as pl
from jax.experimental.pallas import tpu as pltpu

F32 = jnp.float32
BF16 = jnp.bfloat16

D_MODEL = 4096
DEPTH = 2
GRID_W = 64
CTX_LEN = 256
NORM_EPS = 1e-6
NEG_INF = -1e30

RW_HEAD = 64
RW_W = D_MODEL // 2
RW_HEADS = RW_W // RW_HEAD
LORA_W = 96
LORA_A = 96
LORA_G = 256
GN_EPS = 64e-5
RW_COLS = 3 * RW_W + LORA_G + 2 * LORA_W + 2 * LORA_A
LORA_OFF = 3 * RW_W + LORA_G
LORA_ALL = 2 * LORA_W + 2 * LORA_A

NA_HEAD = 128
NA_W = D_MODEL // 4
NA_HEADS = NA_W // NA_HEAD
NA_KH = 8
NA_KW = 16
NA_COLS = 3 * NA_W

SW_HEAD = 64
SW_W = D_MODEL - RW_W - NA_W
SW_HEADS = SW_W // SW_HEAD
SW_KV = SW_HEADS // 8
SW_GROUP = SW_HEADS // SW_KV
SW_WIN = 128
SW_BLOCK = 128
ROPE_BASE = 10000.0
SW_COLS = SW_W + 2 * SW_KV * SW_HEAD
ATT_COLS = NA_COLS + SW_COLS

N_EXPERTS = 64
N_GROUPS = 8
EXPERTS_PER_GROUP = N_EXPERTS // N_GROUPS
TOPK_GROUPS = 4
TOP_K = 8
D_EXPERT = 256
ROUTE_SCALE = 2.5

LANES = 128
SUBLANES = 8
RW_COLS_PAD = 6912
CHUNK = 64
VMEM_LIMIT = 56 * 1024 * 1024


def _cparams(n_axes, vmem=VMEM_LIMIT):
    return pltpu.CompilerParams(dimension_semantics=("arbitrary",) * n_axes, vmem_limit_bytes=vmem)


def _dot(a, b):
    return jnp.dot(a, b, preferred_element_type=F32)


def _dot_nt(a, b):
    return lax.dot_general(a, b, (((1,), (1,)), ((), ())), preferred_element_type=F32)


def _split2(x):
    hi = x.astype(BF16)
    lo = (x - hi.astype(F32)).astype(BF16)
    return hi, lo


def _split3(x):
    hi = x.astype(BF16)
    r1 = x - hi.astype(F32)
    mid = r1.astype(BF16)
    lo = (r1 - mid.astype(F32)).astype(BF16)
    return hi, mid, lo


def _sigmoid(x):
    return 1.0 / (1.0 + jnp.exp(-x))


def _ada_body(s_ref, w_ref, b_ref, o_ref):
    c = s_ref[...]
    s = (c * _sigmoid(c)).astype(BF16)
    o_ref[0] = _dot(s, w_ref[0].astype(BF16)) + b_ref[0]


def ada_mod(cond8, w_ada, b_ada):
    bn = 512
    n = 6 * D_MODEL
    return pl.pallas_call(
        _ada_body,
        grid=(DEPTH, n // bn),
        in_specs=[
            pl.BlockSpec((SUBLANES, D_MODEL), lambda l, j: (0, 0)),
            pl.BlockSpec((1, D_MODEL, bn), lambda l, j: (l, 0, j)),
            pl.BlockSpec((1, 1, bn), lambda l, j: (l, 0, j)),
        ],
        out_specs=pl.BlockSpec((1, SUBLANES, bn), lambda l, j: (l, 0, j)),
        out_shape=jax.ShapeDtypeStruct((DEPTH, SUBLANES, n), F32),
        compiler_params=_cparams(2),
        name="ada_mod",
    )(cond8, w_ada, b_ada.reshape(DEPTH, 1, n))


def _norm_mod_body(x_ref, g_ref, sc_ref, sh_ref, o_ref):
    x = x_ref[...]
    ms = jnp.mean(x * x, axis=-1, keepdims=True)
    y = x * lax.rsqrt(ms + NORM_EPS) * g_ref[...]
    o_ref[...] = (y * (1.0 + sc_ref[...]) + sh_ref[...]).astype(o_ref.dtype)


def norm_mod(x, g, sc, sh, out_dtype=BF16):
    t = x.shape[0]
    tm = 256
    row = pl.BlockSpec((1, D_MODEL), lambda i: (0, 0))
    return pl.pallas_call(
        _norm_mod_body,
        grid=(t // tm,),
        in_specs=[pl.BlockSpec((tm, D_MODEL), lambda i: (i, 0)), row, row, row],
        out_specs=pl.BlockSpec((tm, D_MODEL), lambda i: (i, 0)),
        out_shape=jax.ShapeDtypeStruct((t, D_MODEL), out_dtype),
        compiler_params=_cparams(1),
        name="norm_mod",
    )(x, g, sc, sh)


def _final_norm_body(x_ref, g_ref, o_ref):
    x = x_ref[...]
    ms = jnp.mean(x * x, axis=-1, keepdims=True)
    o_ref[...] = x * lax.rsqrt(ms + NORM_EPS) * g_ref[...]


def final_norm(x, g):
    t = x.shape[0]
    tm = 256
    return pl.pallas_call(
        _final_norm_body,
        grid=(t // tm,),
        in_specs=[pl.BlockSpec((tm, D_MODEL), lambda i: (i, 0)), pl.BlockSpec((1, D_MODEL), lambda i: (0, 0))],
        out_specs=pl.BlockSpec((tm, D_MODEL), lambda i: (i, 0)),
        out_shape=jax.ShapeDtypeStruct((t, D_MODEL), F32),
        compiler_params=_cparams(1),
        name="final_norm",
    )(x, g)


def _mm_body(*refs, n_in, residual):
    a_refs = refs[:n_in]
    w_refs = refs[n_in:2 * n_in]
    rest = refs[2 * n_in:]
    acc = _dot(a_refs[0][...], w_refs[0][0].astype(BF16))
    for a_ref, w_ref in zip(a_refs[1:], w_refs[1:]):
        acc = acc + _dot(a_ref[...], w_ref[0].astype(BF16))
    if residual:
        x_ref, ga_ref, o_ref = rest
        o_ref[...] = x_ref[...] + ga_ref[...] * acc
    else:
        (o_ref,) = rest
        o_ref[...] = acc.astype(o_ref.dtype)


def matmul(a_list, w_list, n, bn, out_dtype, residual=None, name="matmul"):
    m = a_list[0].shape[0]
    bm = 1024 if m % 1024 == 0 else 256
    in_specs = [pl.BlockSpec((bm, a.shape[1]), lambda i, j: (i, 0)) for a in a_list]
    for a, (_, layer, row_blk) in zip(a_list, w_list):
        in_specs.append(pl.BlockSpec((1, a.shape[1], bn), lambda i, j, layer=layer, row_blk=row_blk: (layer, row_blk, j)))
    args = list(a_list) + [w for w, _, _ in w_list]
    if residual is not None:
        in_specs += [pl.BlockSpec((bm, bn), lambda i, j: (i, j)), pl.BlockSpec((1, bn), lambda i, j: (0, j))]
        args += list(residual)
    return pl.pallas_call(
        functools.partial(_mm_body, n_in=len(a_list), residual=residual is not None),
        grid=(m // bm, n // bn),
        in_specs=in_specs,
        out_specs=pl.BlockSpec((bm, bn), lambda i, j: (i, j)),
        out_shape=jax.ShapeDtypeStruct((m, n), out_dtype),
        compiler_params=_cparams(2),
        name=name,
    )(*args)


RW_TM = 128


def _seg_sum_bcast(x, e_ref, et_ref):
    hi, lo = _split2(x)
    s = _dot(hi, e_ref[...]) + _dot(lo, e_ref[...])
    shi, slo = _split2(s)
    return _dot(shi, et_ref[...]) + _dot(slo, et_ref[...])


def _rw_prep_body(u_ref, up_ref, un_ref, taps_ref, w0_ref, a0_ref, w2_ref, a2_ref, g2_ref, kks_ref, ka_ref,
                  rk_ref, e_ref, et_ref, tri_ref, sel_ref,
                  v_ref, g_ref, bv_ref, rt_ref, kkt_ref, kbar_ref, bbar_ref, gam_ref):
    i = pl.program_id(0)
    nt = pl.num_programs(0)
    tm = RW_TM
    rows = lax.broadcasted_iota(jnp.int32, (tm, 1), 0)
    has_prev = (i > 0).astype(F32)
    has_next = (i < nt - 1).astype(F32)

    def conv(c0, c1):
        u = u_ref[:, c0:c1]
        prev_row = up_ref[SUBLANES - 1:SUBLANES, c0:c1] * has_prev
        next_row = un_ref[0:1, c0:c1] * has_next
        u_prev = jnp.where(rows == 0, prev_row, pltpu.roll(u, 1, 0))
        u_next = jnp.where(rows == tm - 1, next_row, pltpu.roll(u, tm - 1, 0))
        return (u_prev * taps_ref[0:1, c0:c1] + u * taps_ref[1:2, c0:c1] + u_next * taps_ref[2:3, c0:c1])

    r = conv(0, RW_W)
    k = conv(RW_W, 2 * RW_W)
    v = conv(2 * RW_W, 3 * RW_W)
    gd = conv(3 * RW_W, LORA_OFF)
    lo_in = conv(LORA_OFF, LORA_OFF + LORA_ALL)

    v_ref[...] = v.astype(v_ref.dtype)
    g_ref[...] = _dot(_sigmoid(gd).astype(BF16), g2_ref[...])

    kk = k * kks_ref[...]
    ssq = _seg_sum_bcast(kk * kk, e_ref, et_ref)
    kkn = kk / jnp.maximum(jnp.sqrt(ssq), 1e-12)

    tanh_lo = jnp.tanh(lo_in).astype(BF16)
    raw_lo = lo_in.astype(BF16)
    bonus = jnp.zeros((tm, RW_W), F32)
    for d in range(2):
        w_pre = w0_ref[d:d + 1, :] + _dot(tanh_lo, w2_ref[d])
        z = -w_pre
        softplus = jnp.maximum(z, 0.0) + jnp.log(1.0 + jnp.exp(-jnp.abs(z)))
        w_log = -softplus - 0.5
        lw = -jnp.exp(w_log)
        a = _sigmoid(a0_ref[d:d + 1, :] + _dot(raw_lo, a2_ref[d]))
        kd = k * (1.0 + (a - 1.0) * ka_ref[...])
        b = kkn * a
        bonus = bonus + r * kd * rk_ref[...]
        l1, l2, l3 = _split3(lw)
        tri = tri_ref[d]
        sel = sel_ref[...]
        cum = _dot(tri, l1) + _dot(tri, l2) + _dot(tri, l3)
        tot = _dot(sel, l1) + _dot(sel, l2) + _dot(sel, l3)
        e_inv = jnp.exp(-cum)
        rt_ref[d] = (r * jnp.exp(cum)).astype(BF16)
        kkt_ref[d] = (kkn * jnp.exp(cum - lw)).astype(BF16)
        kbar_ref[d] = (kd * e_inv).T.astype(BF16)
        bbar_ref[d] = (b * e_inv).T.astype(BF16)
        gam_ref[d] = jnp.exp(tot)
    bv_ref[...] = _seg_sum_bcast(bonus, e_ref, et_ref) * v


def _rw_consts():
    tm = RW_TM
    t = jnp.arange(tm)
    same = (t[:, None] // CHUNK) == (t[None, :] // CHUNK)
    tri_f = same & (t[None, :] <= t[:, None])
    tri_b = same & (t[None, :] >= t[:, None])
    tri = jnp.stack([tri_f, tri_b]).astype(BF16)
    sel = ((t[None, :] // CHUNK) == jnp.arange(SUBLANES)[:, None]).astype(BF16)
    lane_head = jnp.arange(RW_W) // RW_HEAD
    e = (lane_head[:, None] == jnp.arange(LANES)[None, :]).astype(BF16)
    return tri, sel, e, e.T


def rw_prep(u, prm):
    t = u.shape[0]
    tm = RW_TM
    nt = t // tm
    hb = tm // SUBLANES
    tri, sel, e, et = _rw_consts()
    full = lambda shp: pl.BlockSpec(shp, lambda i: (0,) * len(shp))
    in_specs = [
        pl.BlockSpec((tm, RW_COLS_PAD), lambda i: (i, 0)),
        pl.BlockSpec((SUBLANES, RW_COLS_PAD), lambda i: (jnp.maximum(i * hb - 1, 0), 0)),
        pl.BlockSpec((SUBLANES, RW_COLS_PAD), lambda i: (jnp.minimum((i + 1) * hb, nt * hb - 1), 0)),
        full((SUBLANES, RW_COLS_PAD)),
        full((2, RW_W)), full((2, RW_W)),
        full((2, LORA_ALL, RW_W)), full((2, LORA_ALL, RW_W)),
        full((LORA_G, RW_W)),
        full((1, RW_W)), full((1, RW_W)), full((1, RW_W)),
        full((RW_W, LANES)), full((LANES, RW_W)),
        full((2, tm, tm)), full((SUBLANES, tm)),
    ]
    tok = lambda dt: jax.ShapeDtypeStruct((t, RW_W), dt)
    tok2 = lambda dt: jax.ShapeDtypeStruct((2, t, RW_W), dt)
    spec1 = pl.BlockSpec((tm, RW_W), lambda i: (i, 0))
    spec2 = pl.BlockSpec((2, tm, RW_W), lambda i: (0, i, 0))
    chan2 = jax.ShapeDtypeStruct((2, RW_W, t), BF16)
    spec2t = pl.BlockSpec((2, RW_W, tm), lambda i: (0, 0, i))
    out_shape = [tok(BF16), tok(F32), tok(F32), tok2(BF16), tok2(BF16), chan2, chan2,
                 jax.ShapeDtypeStruct((2, nt * SUBLANES, RW_W), F32)]
    out_specs = [spec1, spec1, spec1, spec2, spec2, spec2t, spec2t,
                 pl.BlockSpec((2, SUBLANES, RW_W), lambda i: (0, i, 0))]
    return pl.pallas_call(
        _rw_prep_body,
        grid=(nt,),
        in_specs=in_specs,
        out_specs=out_specs,
        out_shape=out_shape,
        compiler_params=_cparams(1),
        name="rw_prep",
    )(u, u, u, prm["taps"], prm["w0"], prm["a0"], prm["w2"], prm["a2"], prm["g2"], prm["kks"], prm["ka"],
      prm["rk"], e, et, tri, sel)


RW_PAIRS = RW_HEADS // 2
SCAN_G = 8
SC_T = RW_TM


def _bmm(a, b):
    return lax.dot_general(a, b, (((2,), (1,)), ((0,), (0,))), preferred_element_type=F32)


def _rw_scan_body(rt_ref, kkt_ref, v_ref, kbt_ref, bbt_ref, gam_ref, h0_ref, y_ref, hfin_ref, h_scr, *, reverse):
    c = pl.program_id(1)
    nc = pl.num_programs(1)

    @pl.when(c == 0)
    def _():
        h_scr[...] = h0_ref[...]

    gn = SCAN_G
    n = SC_T
    row = lax.broadcasted_iota(jnp.int32, (n, n), 0)
    col = lax.broadcasted_iota(jnp.int32, (n, n), 1)
    same = (row // CHUNK) == (col // CHUNK)
    earlier = (col > row) if reverse else (col < row)
    strict = same & earlier
    incl = same & (earlier | (col == row))
    eye = (row == col).astype(F32)
    level_masks = []
    b = 1
    while b < CHUNK:
        level_masks.append(((row // (2 * b)) == (col // (2 * b))) & ((row // b) != (col // b)))
        b *= 2
    head_a = (col < RW_HEAD).astype(BF16)
    head_b = (col >= RW_HEAD).astype(BF16)

    lanes = lambda g: slice(g * n, (g + 1) * n)
    rt_p = [rt_ref[0, :, lanes(g)] for g in range(gn)]
    kk_p = [kkt_ref[0, :, lanes(g)] for g in range(gn)]
    v_p = [v_ref[:, lanes(g)] for g in range(gn)]
    kbt = kbt_ref[0].reshape(gn, n, n)
    bbt = bbt_ref[0].reshape(gn, n, n)

    lhs = jnp.stack([jnp.concatenate([rt_p[g] * head_a, kk_p[g] * head_a, rt_p[g] * head_b, kk_p[g] * head_b],
                                     axis=0) for g in range(gn)])
    ak = _bmm(lhs, kbt)
    ab = _bmm(lhs, bbt)
    per_head = lambda x, i: jnp.concatenate([x[:, i * n:(i + 1) * n], x[:, (i + 2) * n:(i + 3) * n]], axis=0)
    ark = jnp.where(incl, per_head(ak, 0), 0.0)
    lk = jnp.where(strict, per_head(ak, 1), 0.0)
    arb = jnp.where(incl, per_head(ab, 0), 0.0)
    lb = jnp.where(strict, per_head(ab, 1), 0.0)

    x = eye - jnp.where(level_masks[0], lb, 0.0)
    for mask in level_masks[1:]:
        xb = x.astype(BF16)
        off = jnp.where(mask, lb, 0.0).astype(BF16)
        x = x - _bmm(xb, _bmm(off, xb).astype(BF16))
    tb = x.astype(BF16)

    kkm = jnp.stack([kk_p[g] * head_a for g in range(gn)] + [kk_p[g] * head_b for g in range(gn)])
    vm = jnp.stack([v_p[g] * head_a for g in range(gn)] + [v_p[g] * head_b for g in range(gn)])
    wk = _bmm(tb, kkm)
    u0 = _bmm(tb, _bmm(lk.astype(BF16), vm).astype(BF16))
    arbb = arb.astype(BF16)
    qw = _bmm(arbb, wk.astype(BF16))
    y0 = _bmm(jnp.concatenate([ark.astype(BF16), -arbb], axis=2),
              jnp.concatenate([vm, u0.astype(BF16)], axis=1))
    pair = lambda t: t[:gn] + t[gn:]
    wk_p = pair(wk).astype(BF16)
    u0_p = pair(u0).astype(BF16)
    qhat = (jnp.stack(rt_p).astype(F32) - pair(qw)).astype(BF16)
    y0_p = pair(y0)
    vu = jnp.concatenate([jnp.stack(v_p), u0_p], axis=1)

    h = h_scr[...]
    chunks = range(n // CHUNK)
    for j in (reversed(chunks) if reverse else chunks):
        in_chunk = ((col // CHUNK) == j).astype(BF16)
        kj = kbt * in_chunk
        bj = bbt * in_chunk
        hloc = jnp.where(same, _bmm(jnp.concatenate([kj, -bj], axis=2), vu), 0.0)
        wb = jnp.where(same, _bmm(bj, wk_p), 0.0).astype(BF16)
        gamma = jnp.stack([jnp.broadcast_to(gam_ref[0, j:j + 1, lanes(g)], (n, n)).T for g in range(gn)])
        hb = h.astype(BF16)
        rows = slice(j * CHUNK, (j + 1) * CHUNK)
        yj = _bmm(qhat[:, rows], hb) + y0_p[:, rows]
        for g in range(gn):
            y_ref[rows, lanes(g)] = yj[g]
        h = gamma * (h - _bmm(wb, hb) + hloc)
    h_scr[...] = h

    @pl.when(c == nc - 1)
    def _():
        hfin_ref[...] = h


def rw_scan(rt, kkt, v, kbar_t, bbar_t, gam, h0, d):
    t = v.shape[0]
    nsc = t // SC_T
    reverse = d == 1
    gw = SCAN_G * LANES
    tix = (lambda c: nsc - 1 - c) if reverse else (lambda c: c)
    tm_spec = pl.BlockSpec((1, SC_T, gw), lambda p, c: (d, tix(c), p))
    cm_spec = pl.BlockSpec((1, gw, SC_T), lambda p, c: (d, p, tix(c)))
    st_spec = pl.BlockSpec((SCAN_G, LANES, LANES), lambda p, c: (p, 0, 0))
    return pl.pallas_call(
        functools.partial(_rw_scan_body, reverse=reverse),
        grid=(RW_PAIRS // SCAN_G, nsc),
        in_specs=[tm_spec, tm_spec, pl.BlockSpec((SC_T, gw), lambda p, c: (tix(c), p)), cm_spec, cm_spec,
                  pl.BlockSpec((1, SUBLANES, gw), lambda p, c: (d, tix(c), p)), st_spec],
        out_specs=[pl.BlockSpec((SC_T, gw), lambda p, c: (tix(c), p)), st_spec],
        out_shape=[jax.ShapeDtypeStruct((t, RW_W), F32),
                   jax.ShapeDtypeStruct((RW_PAIRS, LANES, LANES), F32)],
        scratch_shapes=[pltpu.VMEM((SCAN_G, LANES, LANES), F32)],
        compiler_params=_cparams(2),
        name="rw_scan_rev" if reverse else "rw_scan_fwd",
    )(rt, kkt, v, kbar_t, bbar_t, gam, h0)


def _rw_post_body(yf_ref, yb_ref, bv_ref, g_ref, lg_ref, lb_ref, e_ref, et_ref, o_ref):
    y = yf_ref[...] + yb_ref[...]
    mu = _seg_sum_bcast(y, e_ref, et_ref) * (1.0 / RW_HEAD)
    dlt = y - mu
    var = _seg_sum_bcast(dlt * dlt, e_ref, et_ref) * (1.0 / RW_HEAD)
    yn = dlt * lax.rsqrt(var + GN_EPS)
    o_ref[...] = ((yn * lg_ref[...] + lb_ref[...] + bv_ref[...]) * g_ref[...]).astype(o_ref.dtype)


def rw_post(yf, yb, bv, g, lnx_g, lnx_b):
    t = yf.shape[0]
    tm = 256
    _, _, e, et = _rw_consts()
    tokspec = pl.BlockSpec((tm, RW_W), lambda i: (i, 0))
    row = pl.BlockSpec((1, RW_W), lambda i: (0, 0))
    return pl.pallas_call(
        _rw_post_body,
        grid=(t // tm,),
        in_specs=[tokspec, tokspec, tokspec, tokspec, row, row,
                  pl.BlockSpec((RW_W, LANES), lambda i: (0, 0)), pl.BlockSpec((LANES, RW_W), lambda i: (0, 0))],
        out_specs=tokspec,
        out_shape=jax.ShapeDtypeStruct((t, RW_W), BF16),
        compiler_params=_cparams(1),
        name="rw_post",
    )(yf, yb, bv, g, lnx_g, lnx_b, e, et)


def rwkv_group(u, prm, h0_f, h0_b, with_output):
    v, g, bv, rt, kkt, kbar_t, bbar_t, gam = rw_prep(u, prm)
    outs = []
    finals = []
    for d, h0 in ((0, h0_f), (1, h0_b)):
        y, hfin = rw_scan(rt, kkt, v, kbar_t, bbar_t, gam, h0, d)
        outs.append(y)
        finals.append(hfin)
    if not with_output:
        return None, finals[0], finals[1]
    out = rw_post(outs[0], outs[1], bv, g, prm["lnx_g"], prm["lnx_b"])
    return out, finals[0], finals[1]


NA_RQ = 4


def _na_body(q_ref, k_ref, v_ref, kc_ref, vc_ref, bt_ref, o_ref, *, n_rows):
    rb = pl.program_id(1)
    scale = NA_HEAD ** -0.5
    kc = kc_ref[...]
    vc = vc_ref[...]
    for qi in range(NA_RQ):
        r = rb * NA_RQ + qi
        start = jnp.clip(r - NA_KH // 2, 0, n_rows - NA_KH)
        dr0 = start - r + NA_KH - 1
        tok0 = pl.multiple_of(start * GRID_W, GRID_W)
        q = q_ref[qi * GRID_W:(qi + 1) * GRID_W, :]
        kw = k_ref[pl.ds(tok0, NA_KH * GRID_W), :]
        vw = v_ref[pl.ds(tok0, NA_KH * GRID_W), :]
        s = _dot_nt(q, kw) * scale
        bias = jnp.concatenate([bt_ref[0, dr0 + 2 * m] for m in range(NA_KH // 2)], axis=1)
        s = s + bias
        sc = _dot_nt(q, kc) * scale
        mx = jnp.maximum(jnp.max(s, axis=-1, keepdims=True), jnp.max(sc, axis=-1, keepdims=True))
        p = jnp.exp(s - mx)
        pc = jnp.exp(sc - mx)
        den = jnp.sum(p, axis=-1, keepdims=True) + jnp.sum(pc, axis=-1, keepdims=True)
        o = (_dot(p.astype(BF16), vw) + _dot(pc.astype(BF16), vc)) / den
        o_ref[qi * GRID_W:(qi + 1) * GRID_W, :] = o.astype(o_ref.dtype)


def _na_bias_table(rpb):
    cols = jnp.arange(GRID_W)
    col_start = jnp.clip(cols - NA_KW // 2, 0, GRID_W - NA_KW)
    col_ok = (cols[None, :] >= col_start[:, None]) & (cols[None, :] < col_start[:, None] + NA_KW)
    dc = jnp.clip(cols[None, :] - cols[:, None] + NA_KW - 1, 0, 2 * NA_KW - 2)
    bt = rpb.astype(F32)[:, :, dc]
    bt = jnp.where(col_ok[None, None], bt, NEG_INF)
    return jnp.concatenate([bt[:, :-1], bt[:, 1:]], axis=-1)


def na_latent(att_l, att_c, rpb):
    s_len = att_l.shape[0]
    n_rows = s_len // GRID_W
    qb = NA_W // NA_HEAD
    bt = _na_bias_table(rpb)
    return pl.pallas_call(
        functools.partial(_na_body, n_rows=n_rows),
        grid=(NA_HEADS, n_rows // NA_RQ),
        in_specs=[
            pl.BlockSpec((NA_RQ * GRID_W, NA_HEAD), lambda h, r: (r, h)),
            pl.BlockSpec((s_len, NA_HEAD), lambda h, r: (0, qb + h)),
            pl.BlockSpec((s_len, NA_HEAD), lambda h, r: (0, 2 * qb + h)),
            pl.BlockSpec((CTX_LEN, NA_HEAD), lambda h, r: (0, qb + h)),
            pl.BlockSpec((CTX_LEN, NA_HEAD), lambda h, r: (0, 2 * qb + h)),
            pl.BlockSpec((1, 2 * NA_KH - 2, GRID_W, 2 * GRID_W), lambda h, r: (h, 0, 0, 0)),
        ],
        out_specs=pl.BlockSpec((NA_RQ * GRID_W, NA_HEAD), lambda h, r: (r, h)),
        out_shape=jax.ShapeDtypeStruct((s_len, NA_W), BF16),
        compiler_params=_cparams(2),
        name="na_latent",
    )(att_l, att_l, att_l, att_c, att_c, bt)


def _na_ctx_body(q_ref, k_ref, v_ref, o_ref):
    s = _dot_nt(q_ref[...], k_ref[...]) * (NA_HEAD ** -0.5)
    mx = jnp.max(s, axis=-1, keepdims=True)
    p = jnp.exp(s - mx)
    den = jnp.sum(p, axis=-1, keepdims=True)
    o_ref[...] = (_dot(p.astype(BF16), v_ref[...]) / den).astype(o_ref.dtype)


def na_ctx(att_c):
    qb = NA_W // NA_HEAD
    blk = lambda off: pl.BlockSpec((CTX_LEN, NA_HEAD), lambda h: (0, off + h))
    return pl.pallas_call(
        _na_ctx_body,
        grid=(NA_HEADS,),
        in_specs=[blk(0), blk(qb), blk(2 * qb)],
        out_specs=blk(0),
        out_shape=jax.ShapeDtypeStruct((CTX_LEN, NA_W), BF16),
        compiler_params=_cparams(1),
        name="na_ctx",
    )(att_c, att_c, att_c)


def _rope_body(x_ref, c_ref, s1_ref, s2_ref, o_ref):
    x = x_ref[...].astype(F32)
    quarter = SW_HEAD // 4
    x_up = pltpu.roll(x, LANES - quarter, 1)
    x_dn = pltpu.roll(x, quarter, 1)
    o_ref[...] = (x * c_ref[...] + x_up * s1_ref[...] + x_dn * s2_ref[...]).astype(o_ref.dtype)


def _rope_tables(s_len):
    t = jnp.arange(s_len)
    row = (t // GRID_W).astype(F32)
    col = (t % GRID_W).astype(F32)
    half = SW_HEAD // 2
    inv = ROPE_BASE ** (-jnp.arange(0, half, 2, dtype=F32) / half)
    ang_r = row[:, None] * inv[None, :]
    ang_c = col[:, None] * inv[None, :]
    cos = jnp.concatenate([jnp.cos(ang_r)] * 2 + [jnp.cos(ang_c)] * 2, axis=-1)
    sin_r, sin_c = jnp.sin(ang_r), jnp.sin(ang_c)
    zero = jnp.zeros_like(sin_r)
    s_up = jnp.concatenate([-sin_r, zero, -sin_c, zero], axis=-1)
    s_dn = jnp.concatenate([zero, sin_r, zero, sin_c], axis=-1)
    tile2 = lambda a: jnp.concatenate([a, a], axis=-1)
    return tile2(cos), tile2(s_up), tile2(s_dn)


def rope_qk(att_l):
    s_len = att_l.shape[0]
    tm = min(2048, s_len)
    nblk = (SW_W + SW_KV * SW_HEAD) // LANES
    off = NA_COLS // LANES
    cos, s_up, s_dn = _rope_tables(s_len)
    tab = pl.BlockSpec((tm, LANES), lambda i, j: (i, 0))
    return pl.pallas_call(
        _rope_body,
        grid=(s_len // tm, nblk),
        in_specs=[pl.BlockSpec((tm, LANES), lambda i, j: (i, off + j)), tab, tab, tab],
        out_specs=pl.BlockSpec((tm, LANES), lambda i, j: (i, j)),
        out_shape=jax.ShapeDtypeStruct((s_len, nblk * LANES), BF16),
        compiler_params=_cparams(2),
        name="rope_qk",
    )(att_l, cos, s_up, s_dn)


def _sw_softmax_out(q8, s_parts, v_parts, sink_col):
    mx = sink_col
    for s in s_parts:
        mx = jnp.maximum(mx, jnp.max(s, axis=-1, keepdims=True))
    den = jnp.exp(sink_col - mx)
    o = None
    for s, v in zip(s_parts, v_parts):
        p = jnp.exp(s - mx)
        den = den + jnp.sum(p, axis=-1, keepdims=True)
        pv = _dot(p.astype(BF16), v)
        o = pv if o is None else o + pv
    return o / den


def _sink_column(sink_ref, kh, rows_per_head):
    n = SW_GROUP * rows_per_head
    grp = lax.broadcasted_iota(jnp.int32, (n, 1), 0) // rows_per_head
    col = jnp.zeros((n, 1), F32)
    for g in range(SW_GROUP):
        col = jnp.where(grp == g, sink_ref[kh * SW_GROUP + g], col)
    return col


def _sw_body(sink_ref, q_ref, k_ref, v_ref, kc_ref, vc_ref, o_ref, *, s_len):
    n = pl.program_id(0)
    scale = SW_HEAD ** -0.5
    win = 3 * SW_BLOCK
    start = pl.multiple_of(jnp.clip((n - 1) * SW_BLOCK, 0, s_len - win), SW_BLOCK)
    k2 = k_ref[pl.ds(start, win), :]
    v2 = v_ref[pl.ds(start, win), :]
    qpos = n * SW_BLOCK + lax.broadcasted_iota(jnp.int32, (SW_BLOCK, win), 0)
    kpos = start + lax.broadcasted_iota(jnp.int32, (SW_BLOCK, win), 1)
    ok = jnp.abs(kpos - qpos) <= SW_WIN
    for kh in range(SW_KV):
        lanes = slice(kh * SW_HEAD, (kh + 1) * SW_HEAD)
        q8 = jnp.concatenate(
            [q_ref[:, (kh * SW_GROUP + g) * SW_HEAD:(kh * SW_GROUP + g + 1) * SW_HEAD] for g in range(SW_GROUP)],
            axis=0)
        s_loc = _dot_nt(q8, k2[:, lanes]) * scale
        s_loc = jnp.where(ok[None], s_loc.reshape(SW_GROUP, SW_BLOCK, win), NEG_INF).reshape(
            SW_GROUP * SW_BLOCK, win)
        s_ctx = _dot_nt(q8, kc_ref[:, lanes]) * scale
        o = _sw_softmax_out(q8, [s_loc, s_ctx], [v2[:, lanes], vc_ref[:, lanes]],
                            _sink_column(sink_ref, kh, SW_BLOCK))
        for g in range(SW_GROUP):
            c0 = (kh * SW_GROUP + g) * SW_HEAD
            o_ref[:, c0:c0 + SW_HEAD] = o[g * SW_BLOCK:(g + 1) * SW_BLOCK].astype(o_ref.dtype)


def swa_latent(qk_rot, att_l, att_c, sink):
    s_len = qk_rot.shape[0]
    kcol = SW_W // LANES
    ck = (NA_COLS + SW_W) // LANES
    return pl.pallas_call(
        functools.partial(_sw_body, s_len=s_len),
        grid=(s_len // SW_BLOCK,),
        in_specs=[
            pl.BlockSpec(memory_space=pltpu.SMEM),
            pl.BlockSpec((SW_BLOCK, SW_W), lambda n: (n, 0)),
            pl.BlockSpec((s_len, LANES), lambda n: (0, kcol)),
            pl.BlockSpec((s_len, LANES), lambda n: (0, ck + 1)),
            pl.BlockSpec((CTX_LEN, LANES), lambda n: (0, ck)),
            pl.BlockSpec((CTX_LEN, LANES), lambda n: (0, ck + 1)),
        ],
        out_specs=pl.BlockSpec((SW_BLOCK, SW_W), lambda n: (n, 0)),
        out_shape=jax.ShapeDtypeStruct((s_len, SW_W), BF16),
        compiler_params=_cparams(1),
        name="swa_latent",
    )(sink, qk_rot, qk_rot, att_l, att_c, att_c)


def _sw_ctx_body(sink_ref, q_ref, k_ref, v_ref, o_ref):
    scale = SW_HEAD ** -0.5
    for kh in range(SW_KV):
        lanes = slice(kh * SW_HEAD, (kh + 1) * SW_HEAD)
        q8 = jnp.concatenate(
            [q_ref[:, (kh * SW_GROUP + g) * SW_HEAD:(kh * SW_GROUP + g + 1) * SW_HEAD] for g in range(SW_GROUP)],
            axis=0)
        s = _dot_nt(q8, k_ref[:, lanes]) * scale
        o = _sw_softmax_out(q8, [s], [v_ref[:, lanes]], _sink_column(sink_ref, kh, CTX_LEN))
        for g in range(SW_GROUP):
            c0 = (kh * SW_GROUP + g) * SW_HEAD
            o_ref[:, c0:c0 + SW_HEAD] = o[g * CTX_LEN:(g + 1) * CTX_LEN].astype(o_ref.dtype)


def swa_ctx(att_c, sink):
    qo = NA_COLS // SW_W
    ck = (NA_COLS + SW_W) // LANES
    return pl.pallas_call(
        _sw_ctx_body,
        grid=(1,),
        in_specs=[
            pl.BlockSpec(memory_space=pltpu.SMEM),
            pl.BlockSpec((CTX_LEN, SW_W), lambda n: (0, qo)),
            pl.BlockSpec((CTX_LEN, LANES), lambda n: (0, ck)),
            pl.BlockSpec((CTX_LEN, LANES), lambda n: (0, ck + 1)),
        ],
        out_specs=pl.BlockSpec((CTX_LEN, SW_W), lambda n: (0, 0)),
        out_shape=jax.ShapeDtypeStruct((CTX_LEN, SW_W), BF16),
        compiler_params=_cparams(1),
        name="swa_ctx",
    )(sink, att_c, att_c, att_c)


def _router_body(x_ref, g_ref, sc_ref, sh_ref, wr_ref, rb_ref, h_ref, pick_ref, hp_ref):
    x = x_ref[...]
    tm = x.shape[0]
    ms = jnp.mean(x * x, axis=-1, keepdims=True)
    h = (x * lax.rsqrt(ms + NORM_EPS) * g_ref[...]) * (1.0 + sc_ref[...]) + sh_ref[...]
    h_ref[...] = h.astype(h_ref.dtype)

    hh, hl = _split2(h)
    wh, wl = _split2(wr_ref[...])
    logits = _dot(hh, wh) + _dot(hl, wh) + _dot(hh, wl)
    lane = lax.broadcasted_iota(jnp.int32, (tm, LANES), 1)
    lane_f = lane.astype(F32)
    valid = lane < N_EXPERTS
    neg = -jnp.inf
    big = float(2 * LANES)
    scores = _sigmoid(logits)
    biased = jnp.where(valid, scores + rb_ref[...], neg)
    grp = lane // EXPERTS_PER_GROUP

    def first_argmax(vals):
        m = jnp.max(vals, axis=-1, keepdims=True)
        idx = jnp.min(jnp.where(vals == m, lane_f, big), axis=-1, keepdims=True)
        return m, idx

    gscore = jnp.full((tm, LANES), neg, F32)
    for gi in range(N_GROUPS):
        vg = jnp.where(grp == gi, biased, neg)
        m1, i1 = first_argmax(vg)
        m2 = jnp.max(jnp.where(lane_f == i1, neg, vg), axis=-1, keepdims=True)
        gscore = jnp.where(lane == gi * EXPERTS_PER_GROUP, m1 + m2, gscore)
    keep = jnp.zeros((tm, LANES), jnp.bool_)
    for _ in range(TOPK_GROUPS):
        _, gi = first_argmax(gscore)
        keep = keep | (grp.astype(F32) == jnp.floor(gi * (1.0 / EXPERTS_PER_GROUP)))
        gscore = jnp.where(lane_f == gi, neg, gscore)
    cur = jnp.where(valid, jnp.where(keep, biased, NEG_INF), neg)
    sel = jnp.zeros((tm, LANES), jnp.bool_)
    picks = jnp.zeros((tm, LANES), F32)
    for k in range(TOP_K):
        _, ei = first_argmax(cur)
        hit = lane_f == ei
        sel = sel | hit
        cur = jnp.where(hit, neg, cur)
        picks = jnp.where(lane == k, ei, picks)
        picks = jnp.where(lane == TOP_K + k, jnp.sum(jnp.where(hit, scores, 0.0), axis=-1, keepdims=True), picks)
    w = jnp.where(sel, scores, 0.0)
    norm = ROUTE_SCALE / jnp.sum(w, axis=-1, keepdims=True)
    pick_ref[...] = jnp.where((lane >= TOP_K) & (lane < 2 * TOP_K), picks * norm, picks)

    bits = pltpu.bitcast(h.astype(BF16).astype(F32), jnp.uint32)
    half = D_MODEL // 2
    words = (bits[:, half:] & jnp.uint32(0xFFFF0000)) | (bits[:, :half] >> 16)
    for a in range(TOK_ROWS):
        hp_ref[pl.ds(a, tm, stride=TOK_ROWS), :] = words[:, a * LANES:(a + 1) * LANES]


def router(x, g, sc, sh, w_router_pad, r_bias_pad):
    t = x.shape[0]
    tm = 256
    row = pl.BlockSpec((1, D_MODEL), lambda i: (0, 0))
    return pl.pallas_call(
        _router_body,
        grid=(t // tm,),
        in_specs=[pl.BlockSpec((tm, D_MODEL), lambda i: (i, 0)), row, row, row,
                  pl.BlockSpec((D_MODEL, LANES), lambda i: (0, 0)), pl.BlockSpec((1, LANES), lambda i: (0, 0))],
        out_specs=[pl.BlockSpec((tm, D_MODEL), lambda i: (i, 0)), pl.BlockSpec((tm, LANES), lambda i: (i, 0)),
                   pl.BlockSpec((tm * TOK_ROWS, LANES), lambda i: (i, 0))],
        out_shape=[jax.ShapeDtypeStruct((t, D_MODEL), BF16), jax.ShapeDtypeStruct((t, LANES), F32),
                   jax.ShapeDtypeStruct((t * TOK_ROWS, LANES), jnp.uint32)],
        compiler_params=_cparams(1),
        name="router",
    )(x, g, sc, sh, w_router_pad, r_bias_pad)


MOE_TM = 512
MOE_TM_SMALL = 64
PACK_W = D_MODEL // 2
TOK_ROWS = PACK_W // LANES
TOK_PITCH = TOK_ROWS + 4
MOE_DOWN_COLS = 2 * LANES


def _unpack_bf16_pairs(words):
    lo = pltpu.bitcast(words << 16, F32)
    hi = pltpu.bitcast(words & jnp.uint32(0xFFFF0000), F32)
    return lo, hi


def _pack_bf16_pairs(lo, hi):
    lo_bits = pltpu.bitcast(lo.astype(BF16).astype(F32), jnp.uint32) >> 16
    hi_bits = pltpu.bitcast(hi.astype(BF16).astype(F32), jnp.uint32) & jnp.uint32(0xFFFF0000)
    return hi_bits | lo_bits


def _routed_body(te_ref, nv_ref, src_ref, dst_ref, hp_hbm, wg_ref, wu_ref, wd_ref, yk_hbm, xbuf, ybuf, gsem, ssem,
                 *, tm, n_real_rows):
    s = pl.program_id(0)
    nv = nv_ref[0]

    tile_rows = tm * TOK_ROWS

    def gather_wait(slot):
        pltpu.make_async_copy(hp_hbm.at[pl.ds(0, tile_rows)], xbuf.at[slot, pl.ds(0, tile_rows)],
                              gsem.at[slot]).wait()

    def scatter_wait(slot):
        pltpu.make_async_copy(ybuf.at[slot, pl.ds(0, tile_rows)], yk_hbm.at[pl.ds(0, tile_rows)],
                              ssem.at[slot]).wait()

    @pl.when(s == 0)
    def _():
        ybuf[1, pl.ds(0, tile_rows), :] = jnp.zeros((tile_rows, LANES), jnp.uint32)
        fill = pltpu.make_async_copy(ybuf.at[1, pl.ds(0, tile_rows)],
                                     yk_hbm.at[pl.ds(n_real_rows * TOK_ROWS, tile_rows)], ssem.at[1])
        fill.start()
        fill.wait()

    @pl.when(s < nv)
    def _():
        slot = s % 2
        for r in range(tm):
            pltpu.make_async_copy(hp_hbm.at[pl.ds(src_ref[0, 0, r], TOK_ROWS)],
                                  xbuf.at[slot, pl.ds(r * TOK_PITCH, TOK_ROWS)], gsem.at[slot]).start(priority=r % 2)

    @pl.when((s >= 1) & (s <= nv))
    def _():
        t = s - 1
        slot = t % 2
        gather_wait(slot)
        hg = jnp.zeros((tm, D_EXPERT), F32)
        hu = jnp.zeros((tm, D_EXPERT), F32)
        for a in range(TOK_ROWS):
            x_lo, x_hi = _unpack_bf16_pairs(xbuf[slot, pl.ds(a, tm, stride=TOK_PITCH), :])
            x_a = jnp.concatenate([x_lo.astype(BF16), x_hi.astype(BF16)], axis=1)
            lo_rows = slice(a * LANES, (a + 1) * LANES)
            hi_rows = slice(PACK_W + a * LANES, PACK_W + (a + 1) * LANES)
            hg = hg + _dot(x_a, jnp.concatenate([wg_ref[0, 0, lo_rows], wg_ref[0, 0, hi_rows]], axis=0).astype(BF16))
            hu = hu + _dot(x_a, jnp.concatenate([wu_ref[0, 0, lo_rows], wu_ref[0, 0, hi_rows]], axis=0).astype(BF16))
        act = ((hg * _sigmoid(hg)) * hu).astype(BF16)
        for c in range(PACK_W // MOE_DOWN_COLS):
            c0 = c * MOE_DOWN_COLS
            y_lo = _dot(act, wd_ref[0, 0, :, c0:c0 + MOE_DOWN_COLS].astype(BF16))
            y_hi = _dot(act, wd_ref[0, 0, :, PACK_W + c0:PACK_W + c0 + MOE_DOWN_COLS].astype(BF16))
            words = _pack_bf16_pairs(y_lo, y_hi)
            for j in range(MOE_DOWN_COLS // LANES):
                a = c * (MOE_DOWN_COLS // LANES) + j
                ybuf[slot, pl.ds(a, tm, stride=TOK_PITCH), :] = words[:, j * LANES:(j + 1) * LANES]

        @pl.when(t >= 1)
        def _():
            scatter_wait(1 - slot)

        for r in range(tm):
            pltpu.make_async_copy(ybuf.at[slot, pl.ds(r * TOK_PITCH, TOK_ROWS)],
                                  yk_hbm.at[pl.ds(dst_ref[0, 0, r], TOK_ROWS)], ssem.at[slot]).start(priority=r % 2)

        @pl.when(s == nv)
        def _():
            scatter_wait(slot)


def _dispatch_plan(picks, t, tm):
    n_pairs = TOP_K * t
    n_tiles = n_pairs // tm + N_EXPERTS
    e_flat = picks[:, :TOP_K].astype(jnp.int32).reshape(-1)
    pair = jnp.arange(n_pairs, dtype=jnp.int32)
    dst = (pair % TOP_K) * t + pair // TOP_K
    experts = jnp.arange(N_EXPERTS, dtype=jnp.int32)
    counts = jnp.sum((e_flat[:, None] == experts[None, :]).astype(jnp.int32), axis=0)
    fill = (-counts) % tm
    filler_e = jnp.repeat(experts, tm)
    filler_j = jnp.tile(jnp.arange(tm, dtype=jnp.int32), N_EXPERTS)
    filler_key = jnp.where(filler_j < jnp.repeat(fill, tm), 2 * filler_e + 1, 2 * N_EXPERTS)
    dst_bits = (n_pairs + tm - 1).bit_length()
    assert 2 * N_EXPERTS < (1 << (31 - dst_bits))
    keys = jnp.concatenate([2 * e_flat, filler_key]) * (1 << dst_bits) + jnp.concatenate([dst, n_pairs + filler_j])
    dst_sorted = lax.sort(keys) & ((1 << dst_bits) - 1)
    src_sorted = jnp.where(dst_sorted < n_pairs, dst_sorted % t, 0)
    tile_end = jnp.cumsum((counts + fill) // tm)
    n_valid = tile_end[-1].astype(jnp.int32)
    ti = jnp.minimum(jnp.arange(n_tiles, dtype=jnp.int32), n_valid - 1)
    tile_e = jnp.sum((tile_end[None, :] <= ti[:, None]).astype(jnp.int32), axis=1)
    return (tile_e.astype(jnp.int32), n_valid.reshape(1), (src_sorted * TOK_ROWS).reshape(n_tiles, 1, tm),
            (dst_sorted * TOK_ROWS).reshape(n_tiles, 1, tm))


def moe_routed(hp, picks, layer, we_gate, we_up, we_down):
    t = hp.shape[0] // TOK_ROWS
    tm = MOE_TM if TOP_K * t >= N_EXPERTS * MOE_TM else MOE_TM_SMALL
    tile_e, n_valid, src, dst = _dispatch_plan(picks, t, tm)
    n_tiles = tile_e.shape[0]
    last = n_tiles - 1
    cur = lambda s: jnp.minimum(s, last)
    prev = lambda s: jnp.clip(s - 1, 0, last)
    smem = lambda f: pl.BlockSpec((1, 1, tm), lambda s, te, nv: (f(s), 0, 0), memory_space=pltpu.SMEM)
    wspec = lambda shp: pl.BlockSpec((1, 1) + shp, lambda s, te, nv: (layer, te[prev(s)], 0, 0))
    grid_spec = pltpu.PrefetchScalarGridSpec(
        num_scalar_prefetch=2,
        grid=(n_tiles + 1,),
        in_specs=[
            smem(cur), smem(prev),
            pl.BlockSpec(memory_space=pl.ANY),
            wspec((D_MODEL, D_EXPERT)), wspec((D_MODEL, D_EXPERT)), wspec((D_EXPERT, D_MODEL)),
        ],
        out_specs=pl.BlockSpec(memory_space=pl.ANY),
        scratch_shapes=[pltpu.VMEM((2, tm * TOK_PITCH, LANES), jnp.uint32),
                        pltpu.VMEM((2, tm * TOK_PITCH, LANES), jnp.uint32),
                        pltpu.SemaphoreType.DMA((2,)), pltpu.SemaphoreType.DMA((2,))],
    )
    return pl.pallas_call(
        functools.partial(_routed_body, tm=tm, n_real_rows=TOP_K * t),
        grid_spec=grid_spec,
        out_shape=jax.ShapeDtypeStruct(((TOP_K * t + tm) * TOK_ROWS, LANES), jnp.uint32),
        compiler_params=_cparams(1),
        name="moe_routed",
    )(tile_e, n_valid, src, dst, hp, we_gate, we_up, we_down)


COMBINE_TM = 128


def _combine_body(h_ref, pick_ref, *rest):
    yk_refs = rest[:TOP_K]
    wg_ref, wu_ref, wd_ref, x_ref, ga_ref, o_ref = rest[TOP_K:]
    h = h_ref[...]
    hg = _dot(h, wg_ref[...])
    hu = _dot(h, wu_ref[...])
    act = ((hg * _sigmoid(hg)) * hu).astype(BF16)
    picks = pick_ref[...]
    tm = h.shape[0]
    shared = _dot(act, wd_ref[...])
    for a in range(TOK_ROWS):
        acc_lo = shared[:, a * LANES:(a + 1) * LANES]
        acc_hi = shared[:, PACK_W + a * LANES:PACK_W + (a + 1) * LANES]
        for k in range(TOP_K):
            lo, hi = _unpack_bf16_pairs(yk_refs[k][pl.ds(a, tm, stride=TOK_ROWS), :])
            w_k = picks[:, TOP_K + k:TOP_K + k + 1]
            acc_lo = acc_lo + w_k * lo
            acc_hi = acc_hi + w_k * hi
        for base, acc in ((0, acc_lo), (PACK_W, acc_hi)):
            cols = slice(base + a * LANES, base + (a + 1) * LANES)
            o_ref[:, cols] = x_ref[:, cols] + ga_ref[:, cols] * acc


def moe_combine(h, picks, yk, ws_gate, ws_up, ws_down, x, ga):
    t = h.shape[0]
    tm = COMBINE_TM
    nb = t // tm
    tok = lambda cols: pl.BlockSpec((tm, cols), lambda i: (i, 0))
    full = lambda a: pl.BlockSpec(a.shape, lambda i: (0,) * a.ndim)
    slot = lambda k: pl.BlockSpec((tm * TOK_ROWS, LANES), lambda i: (k * nb + i, 0))
    return pl.pallas_call(
        _combine_body,
        grid=(nb,),
        in_specs=[tok(D_MODEL), tok(LANES)] + [slot(k) for k in range(TOP_K)] + [
            full(ws_gate), full(ws_up), full(ws_down), tok(D_MODEL), pl.BlockSpec((1, D_MODEL), lambda i: (0, 0))],
        out_specs=tok(D_MODEL),
        out_shape=jax.ShapeDtypeStruct((t, D_MODEL), F32),
        compiler_params=_cparams(1),
        name="moe_combine",
    )(h, picks, *([yk] * TOP_K), ws_gate, ws_up, ws_down, x, ga)


def _pad_cols(a, n):
    return jnp.pad(a, [(0, 0)] * (a.ndim - 1) + [(0, n - a.shape[-1])])


def _rw_params(i, rw_conv, rw_w0, rw_w2, rw_a0, rw_a2, rw_g2, rw_kk, rw_ka, rw_rk, rw_lnx_g, rw_lnx_b):
    w2 = jnp.zeros((2, LORA_ALL, RW_W), F32)
    a2 = jnp.zeros((2, LORA_ALL, RW_W), F32)
    for d in range(2):
        w2 = w2.at[d, d * LORA_W:(d + 1) * LORA_W].set(rw_w2[i, d])
        a2 = a2.at[d, 2 * LORA_W + d * LORA_A:2 * LORA_W + (d + 1) * LORA_A].set(rw_a2[i, d])
    taps = jnp.pad(rw_conv[i], ((0, SUBLANES - 3), (0, RW_COLS_PAD - RW_COLS)))
    return dict(
        taps=taps, w0=rw_w0[i], a0=rw_a0[i], w2=w2.astype(BF16), a2=a2.astype(BF16), g2=rw_g2[i].astype(BF16),
        kks=rw_kk[i][None], ka=rw_ka[i][None], rk=rw_rk[i].reshape(1, RW_W),
        lnx_g=rw_lnx_g[i][None], lnx_b=rw_lnx_b[i][None])


def kernel(x, c, ctx, c_ctx, w_ada, b_ada, norm1_g, norm2_g, w_in, rw_conv, rw_w0, rw_w2, rw_a0, rw_a2, rw_g2,
           rw_kk, rw_ka, rw_rk, rw_lnx_g, rw_lnx_b, na_rpb, sw_sink, w_out, w_router, router_bias, we_gate,
           we_up, we_down, ws_gate, ws_up, ws_down, final_g):
    assert x.shape[0] == 1 and x.shape[2] == D_MODEL and ctx.shape[1] == CTX_LEN
    xl = x[0]
    xc = ctx[0]
    cond8 = jnp.zeros((SUBLANES, D_MODEL), F32).at[0].set(c[0]).at[1].set(c_ctx)
    mods = ada_mod(cond8, w_ada, b_ada)
    w_in_bf16 = w_in.astype(BF16)

    for i in range(DEPTH):
        ctx_needed = i < DEPTH - 1
        mod_l = [m[None] for m in mods[i, 0].reshape(6, D_MODEL)]
        mod_c = [m[None] for m in mods[i, 1].reshape(6, D_MODEL)]
        w_rw = [(w_in_bf16, i, 0)]
        w_att = [(w_in_bf16[i][:, RW_COLS:][None], 0, 0)]
        w_o_parts = [(w_out, i, 0), (w_out, i, RW_W // NA_W), (w_out, i, (RW_W + NA_W) // SW_W)]
        rw_prm = _rw_params(i, rw_conv, rw_w0, rw_w2, rw_a0, rw_a2, rw_g2, rw_kk, rw_ka, rw_rk, rw_lnx_g,
                            rw_lnx_b)
        n1 = norm1_g[i][None]
        n2 = norm2_g[i][None]

        hl = norm_mod(xl, n1, mod_l[1], mod_l[0])
        hc = norm_mod(xc, n1, mod_c[1], mod_c[0])
        url = matmul([hl], w_rw, RW_COLS_PAD, 768, F32, name="in_proj_rw")
        urc = matmul([hc], w_rw, RW_COLS_PAD, 768, F32, name="in_proj_rw_ctx")
        ual = matmul([hl], w_att, ATT_COLS, 256, BF16, name="in_proj_att")
        uac = matmul([hc], w_att, ATT_COLS, 256, BF16, name="in_proj_att_ctx")

        z = jnp.zeros((RW_PAIRS, LANES, LANES), F32)
        rwc, s_f, s_b = rwkv_group(urc, rw_prm, z, z, ctx_needed)
        rwl, _, _ = rwkv_group(url, rw_prm, s_f, s_b, True)
        nal = na_latent(ual, uac, na_rpb[i])
        swl = swa_latent(rope_qk(ual), ual, uac, sw_sink[i])
        xl = matmul([rwl, nal, swl], w_o_parts, D_MODEL, 512, F32, residual=(xl, mod_l[2]), name="out_proj")

        wr = _pad_cols(w_router[i], LANES)
        rb = _pad_cols(router_bias[i][None], LANES)
        shared = (ws_gate[i].astype(BF16), ws_up[i].astype(BF16), ws_down[i].astype(BF16))

        def moe(xt, mod):
            h2, picks, hp = router(xt, n2, mod[4], mod[3], wr, rb)
            yk = moe_routed(hp, picks, i, we_gate, we_up, we_down)
            return moe_combine(h2, picks, yk, *shared, xt, mod[5])

        xl = moe(xl, mod_l)
        if ctx_needed:
            nac = na_ctx(uac)
            swc = swa_ctx(uac, sw_sink[i])
            xc = matmul([rwc, nac, swc], w_o_parts, D_MODEL, 512, F32, residual=(xc, mod_c[2]), name="out_proj_ctx")
            xc = moe(xc, mod_c)
    return final_norm(xl, final_g[None])[None]
```

```python
import functools

import jax
import jax.numpy as jnp
from jax import lax
from jax.experimental import pallas as pl
from jax.experimental.pallas import tpu as pltpu

F32 = jnp.float32
BF16 = jnp.bfloat16

D_MODEL = 4096
DEPTH = 2
GRID_W = 64
CTX_LEN = 256
NORM_EPS = 1e-6
NEG_INF = -1e30

RW_HEAD = 64
RW_W = D_MODEL // 2
RW_HEADS = RW_W // RW_HEAD
LORA_W = 96
LORA_A = 96
LORA_G = 256
GN_EPS = 64e-5
RW_COLS = 3 * RW_W + LORA_G + 2 * LORA_W + 2 * LORA_A
LORA_OFF = 3 * RW_W + LORA_G
LORA_ALL = 2 * LORA_W + 2 * LORA_A

NA_HEAD = 128
NA_W = D_MODEL // 4
NA_HEADS = NA_W // NA_HEAD
NA_KH = 8
NA_KW = 16
NA_COLS = 3 * NA_W

SW_HEAD = 64
SW_W = D_MODEL - RW_W - NA_W
SW_HEADS = SW_W // SW_HEAD
SW_KV = SW_HEADS // 8
SW_GROUP = SW_HEADS // SW_KV
SW_WIN = 128
SW_BLOCK = 128
ROPE_BASE = 10000.0
SW_COLS = SW_W + 2 * SW_KV * SW_HEAD
ATT_COLS = NA_COLS + SW_COLS

N_EXPERTS = 64
N_GROUPS = 8
EXPERTS_PER_GROUP = N_EXPERTS // N_GROUPS
TOPK_GROUPS = 4
TOP_K = 8
D_EXPERT = 256
ROUTE_SCALE = 2.5

LANES = 128
SUBLANES = 8
RW_COLS_PAD = 6912
CHUNK = 64
VMEM_LIMIT = 56 * 1024 * 1024


def _cparams(n_axes, vmem=VMEM_LIMIT):
    return pltpu.CompilerParams(dimension_semantics=("arbitrary",) * n_axes, vmem_limit_bytes=vmem)


def _dot(a, b):
    return jnp.dot(a, b, preferred_element_type=F32)


def _dot_nt(a, b):
    return lax.dot_general(a, b, (((1,), (1,)), ((), ())), preferred_element_type=F32)


def _split2(x):
    hi = x.astype(BF16)
    lo = (x - hi.astype(F32)).astype(BF16)
    return hi, lo


def _split3(x):
    hi = x.astype(BF16)
    r1 = x - hi.astype(F32)
    mid = r1.astype(BF16)
    lo = (r1 - mid.astype(F32)).astype(BF16)
    return hi, mid, lo


def _sigmoid(x):
    return 1.0 / (1.0 + jnp.exp(-x))


def _ada_body(s_ref, w_ref, b_ref, o_ref):
    c = s_ref[...]
    s = (c * _sigmoid(c)).astype(BF16)
    o_ref[0] = _dot(s, w_ref[0].astype(BF16)) + b_ref[0]


def ada_mod(cond8, w_ada, b_ada):
    bn = 512
    n = 6 * D_MODEL
    return pl.pallas_call(
        _ada_body,
        grid=(DEPTH, n // bn),
        in_specs=[
            pl.BlockSpec((SUBLANES, D_MODEL), lambda l, j: (0, 0)),
            pl.BlockSpec((1, D_MODEL, bn), lambda l, j: (l, 0, j)),
            pl.BlockSpec((1, 1, bn), lambda l, j: (l, 0, j)),
        ],
        out_specs=pl.BlockSpec((1, SUBLANES, bn), lambda l, j: (l, 0, j)),
        out_shape=jax.ShapeDtypeStruct((DEPTH, SUBLANES, n), F32),
        compiler_params=_cparams(2),
        name="ada_mod",
    )(cond8, w_ada, b_ada.reshape(DEPTH, 1, n))


def _norm_mod_body(x_ref, g_ref, sc_ref, sh_ref, o_ref):
    x = x_ref[...]
    ms = jnp.mean(x * x, axis=-1, keepdims=True)
    y = x * lax.rsqrt(ms + NORM_EPS) * g_ref[...]
    o_ref[...] = (y * (1.0 + sc_ref[...]) + sh_ref[...]).astype(o_ref.dtype)


def norm_mod(x, g, sc, sh, out_dtype=BF16):
    t = x.shape[0]
    tm = 256
    row = pl.BlockSpec((1, D_MODEL), lambda i: (0, 0))
    return pl.pallas_call(
        _norm_mod_body,
        grid=(t // tm,),
        in_specs=[pl.BlockSpec((tm, D_MODEL), lambda i: (i, 0)), row, row, row],
        out_specs=pl.BlockSpec((tm, D_MODEL), lambda i: (i, 0)),
        out_shape=jax.ShapeDtypeStruct((t, D_MODEL), out_dtype),
        compiler_params=_cparams(1),
        name="norm_mod",
    )(x, g, sc, sh)


def _final_norm_body(x_ref, g_ref, o_ref):
    x = x_ref[...]
    ms = jnp.mean(x * x, axis=-1, keepdims=True)
    o_ref[...] = x * lax.rsqrt(ms + NORM_EPS) * g_ref[...]


def final_norm(x, g):
    t = x.shape[0]
    tm = 256
    return pl.pallas_call(
        _final_norm_body,
        grid=(t // tm,),
        in_specs=[pl.BlockSpec((tm, D_MODEL), lambda i: (i, 0)), pl.BlockSpec((1, D_MODEL), lambda i: (0, 0))],
        out_specs=pl.BlockSpec((tm, D_MODEL), lambda i: (i, 0)),
        out_shape=jax.ShapeDtypeStruct((t, D_MODEL), F32),
        compiler_params=_cparams(1),
        name="final_norm",
    )(x, g)


def _mm_body(*refs, n_in, residual):
    a_refs = refs[:n_in]
    w_refs = refs[n_in:2 * n_in]
    rest = refs[2 * n_in:]
    acc = _dot(a_refs[0][...], w_refs[0][0].astype(BF16))
    for a_ref, w_ref in zip(a_refs[1:], w_refs[1:]):
        acc = acc + _dot(a_ref[...], w_ref[0].astype(BF16))
    if residual:
        x_ref, ga_ref, o_ref = rest
        o_ref[...] = x_ref[...] + ga_ref[...] * acc
    else:
        (o_ref,) = rest
        o_ref[...] = acc.astype(o_ref.dtype)


def matmul(a_list, w_list, n, bn, out_dtype, residual=None, name="matmul"):
    m = a_list[0].shape[0]
    bm = 1024 if m % 1024 == 0 else 256
    in_specs = [pl.BlockSpec((bm, a.shape[1]), lambda i, j: (i, 0)) for a in a_list]
    for a, (_, layer, row_blk) in zip(a_list, w_list):
        in_specs.append(pl.BlockSpec((1, a.shape[1], bn), lambda i, j, layer=layer, row_blk=row_blk: (layer, row_blk, j)))
    args = list(a_list) + [w for w, _, _ in w_list]
    if residual is not None:
        in_specs += [pl.BlockSpec((bm, bn), lambda i, j: (i, j)), pl.BlockSpec((1, bn), lambda i, j: (0, j))]
        args += list(residual)
    return pl.pallas_call(
        functools.partial(_mm_body, n_in=len(a_list), residual=residual is not None),
        grid=(m // bm, n // bn),
        in_specs=in_specs,
        out_specs=pl.BlockSpec((bm, bn), lambda i, j: (i, j)),
        out_shape=jax.ShapeDtypeStruct((m, n), out_dtype),
        compiler_params=_cparams(2),
        name=name,
    )(*args)


RW_TM = 128


def _seg_sum_bcast(x, e_ref, et_ref):
    hi, lo = _split2(x)
    s = _dot(hi, e_ref[...]) + _dot(lo, e_ref[...])
    shi, slo = _split2(s)
    return _dot(shi, et_ref[...]) + _dot(slo, et_ref[...])


def _rw_prep_body(u_ref, up_ref, un_ref, taps_ref, w0_ref, a0_ref, w2_ref, a2_ref, g2_ref, kks_ref, ka_ref,
                  rk_ref, e_ref, et_ref, tri_ref, sel_ref,
                  v_ref, g_ref, bv_ref, rt_ref, kkt_ref, kbar_ref, bbar_ref, gam_ref):
    i = pl.program_id(0)
    nt = pl.num_programs(0)
    tm = RW_TM
    rows = lax.broadcasted_iota(jnp.int32, (tm, 1), 0)
    has_prev = (i > 0).astype(F32)
    has_next = (i < nt - 1).astype(F32)

    def conv(c0, c1):
        u = u_ref[:, c0:c1]
        prev_row = up_ref[SUBLANES - 1:SUBLANES, c0:c1] * has_prev
        next_row = un_ref[0:1, c0:c1] * has_next
        u_prev = jnp.where(rows == 0, prev_row, pltpu.roll(u, 1, 0))
        u_next = jnp.where(rows == tm - 1, next_row, pltpu.roll(u, tm - 1, 0))
        return (u_prev * taps_ref[0:1, c0:c1] + u * taps_ref[1:2, c0:c1] + u_next * taps_ref[2:3, c0:c1])

    r = conv(0, RW_W)
    k = conv(RW_W, 2 * RW_W)
    v = conv(2 * RW_W, 3 * RW_W)
    gd = conv(3 * RW_W, LORA_OFF)
    lo_in = conv(LORA_OFF, LORA_OFF + LORA_ALL)

    v_ref[...] = v.astype(v_ref.dtype)
    g_ref[...] = _dot(_sigmoid(gd).astype(BF16), g2_ref[...])

    kk = k * kks_ref[...]
    ssq = _seg_sum_bcast(kk * kk, e_ref, et_ref)
    kkn = kk / jnp.maximum(jnp.sqrt(ssq), 1e-12)

    tanh_lo = jnp.tanh(lo_in).astype(BF16)
    raw_lo = lo_in.astype(BF16)
    bonus = jnp.zeros((tm, RW_W), F32)
    for d in range(2):
        w_pre = w0_ref[d:d + 1, :] + _dot(tanh_lo, w2_ref[d])
        z = -w_pre
        softplus = jnp.maximum(z, 0.0) + jnp.log(1.0 + jnp.exp(-jnp.abs(z)))
        w_log = -softplus - 0.5
        lw = -jnp.exp(w_log)
        a = _sigmoid(a0_ref[d:d + 1, :] + _dot(raw_lo, a2_ref[d]))
        kd = k * (1.0 + (a - 1.0) * ka_ref[...])
        b = kkn * a
        bonus = bonus + r * kd * rk_ref[...]
        l1, l2, l3 = _split3(lw)
        tri = tri_ref[d]
        sel = sel_ref[...]
        cum = _dot(tri, l1) + _dot(tri, l2) + _dot(tri, l3)
        tot = _dot(sel, l1) + _dot(sel, l2) + _dot(sel, l3)
        e_inv = jnp.exp(-cum)
        rt_ref[d] = (r * jnp.exp(cum)).astype(BF16)
        kkt_ref[d] = (kkn * jnp.exp(cum - lw)).astype(BF16)
        kbar_ref[d] = (kd * e_inv).T.astype(BF16)
        bbar_ref[d] = (b * e_inv).T.astype(BF16)
        gam_ref[d] = jnp.exp(tot)
    bv_ref[...] = _seg_sum_bcast(bonus, e_ref, et_ref) * v


def _rw_consts():
    tm = RW_TM
    t = jnp.arange(tm)
    same = (t[:, None] // CHUNK) == (t[None, :] // CHUNK)
    tri_f = same & (t[None, :] <= t[:, None])
    tri_b = same & (t[None, :] >= t[:, None])
    tri = jnp.stack([tri_f, tri_b]).astype(BF16)
    sel = ((t[None, :] // CHUNK) == jnp.arange(SUBLANES)[:, None]).astype(BF16)
    lane_head = jnp.arange(RW_W) // RW_HEAD
    e = (lane_head[:, None] == jnp.arange(LANES)[None, :]).astype(BF16)
    return tri, sel, e, e.T


def rw_prep(u, prm):
    t = u.shape[0]
    tm = RW_TM
    nt = t // tm
    hb = tm // SUBLANES
    tri, sel, e, et = _rw_consts()
    full = lambda shp: pl.BlockSpec(shp, lambda i: (0,) * len(shp))
    in_specs = [
        pl.BlockSpec((tm, RW_COLS_PAD), lambda i: (i, 0)),
        pl.BlockSpec((SUBLANES, RW_COLS_PAD), lambda i: (jnp.maximum(i * hb - 1, 0), 0)),
        pl.BlockSpec((SUBLANES, RW_COLS_PAD), lambda i: (jnp.minimum((i + 1) * hb, nt * hb - 1), 0)),
        full((SUBLANES, RW_COLS_PAD)),
        full((2, RW_W)), full((2, RW_W)),
        full((2, LORA_ALL, RW_W)), full((2, LORA_ALL, RW_W)),
        full((LORA_G, RW_W)),
        full((1, RW_W)), full((1, RW_W)), full((1, RW_W)),
        full((RW_W, LANES)), full((LANES, RW_W)),
        full((2, tm, tm)), full((SUBLANES, tm)),
    ]
    tok = lambda dt: jax.ShapeDtypeStruct((t, RW_W), dt)
    tok2 = lambda dt: jax.ShapeDtypeStruct((2, t, RW_W), dt)
    spec1 = pl.BlockSpec((tm, RW_W), lambda i: (i, 0))
    spec2 = pl.BlockSpec((2, tm, RW_W), lambda i: (0, i, 0))
    chan2 = jax.ShapeDtypeStruct((2, RW_W, t), BF16)
    spec2t = pl.BlockSpec((2, RW_W, tm), lambda i: (0, 0, i))
    out_shape = [tok(BF16), tok(F32), tok(F32), tok2(BF16), tok2(BF16), chan2, chan2,
                 jax.ShapeDtypeStruct((2, nt * SUBLANES, RW_W), F32)]
    out_specs = [spec1, spec1, spec1, spec2, spec2, spec2t, spec2t,
                 pl.BlockSpec((2, SUBLANES, RW_W), lambda i: (0, i, 0))]
    return pl.pallas_call(
        _rw_prep_body,
        grid=(nt,),
        in_specs=in_specs,
        out_specs=out_specs,
        out_shape=out_shape,
        compiler_params=_cparams(1),
        name="rw_prep",
    )(u, u, u, prm["taps"], prm["w0"], prm["a0"], prm["w2"], prm["a2"], prm["g2"], prm["kks"], prm["ka"],
      prm["rk"], e, et, tri, sel)


RW_PAIRS = RW_HEADS // 2
SCAN_G = 8
SC_T = RW_TM


def _bmm(a, b):
    return lax.dot_general(a, b, (((2,), (1,)), ((0,), (0,))), preferred_element_type=F32)


def _rw_scan_body(rt_ref, kkt_ref, v_ref, kbt_ref, bbt_ref, gam_ref, h0_ref, y_ref, hfin_ref, h_scr, *, reverse):
    c = pl.program_id(1)
    nc = pl.num_programs(1)

    @pl.when(c == 0)
    def _():
        h_scr[...] = h0_ref[...]

    gn = SCAN_G
    n = SC_T
    row = lax.broadcasted_iota(jnp.int32, (n, n), 0)
    col = lax.broadcasted_iota(jnp.int32, (n, n), 1)
    same = (row // CHUNK) == (col // CHUNK)
    earlier = (col > row) if reverse else (col < row)
    strict = same & earlier
    incl = same & (earlier | (col == row))
    eye = (row == col).astype(F32)
    level_masks = []
    b = 1
    while b < CHUNK:
        level_masks.append(((row // (2 * b)) == (col // (2 * b))) & ((row // b) != (col // b)))
        b *= 2
    head_a = (col < RW_HEAD).astype(BF16)
    head_b = (col >= RW_HEAD).astype(BF16)

    lanes = lambda g: slice(g * n, (g + 1) * n)
    rt_p = [rt_ref[0, :, lanes(g)] for g in range(gn)]
    kk_p = [kkt_ref[0, :, lanes(g)] for g in range(gn)]
    v_p = [v_ref[:, lanes(g)] for g in range(gn)]
    kbt = kbt_ref[0].reshape(gn, n, n)
    bbt = bbt_ref[0].reshape(gn, n, n)

    lhs = jnp.stack([jnp.concatenate([rt_p[g] * head_a, kk_p[g] * head_a, rt_p[g] * head_b, kk_p[g] * head_b],
                                     axis=0) for g in range(gn)])
    ak = _bmm(lhs, kbt)
    ab = _bmm(lhs, bbt)
    per_head = lambda x, i: jnp.concatenate([x[:, i * n:(i + 1) * n], x[:, (i + 2) * n:(i + 3) * n]], axis=0)
    ark = jnp.where(incl, per_head(ak, 0), 0.0)
    lk = jnp.where(strict, per_head(ak, 1), 0.0)
    arb = jnp.where(incl, per_head(ab, 0), 0.0)
    lb = jnp.where(strict, per_head(ab, 1), 0.0)

    x = eye - jnp.where(level_masks[0], lb, 0.0)
    for mask in level_masks[1:]:
        xb = x.astype(BF16)
        off = jnp.where(mask, lb, 0.0).astype(BF16)
        x = x - _bmm(xb, _bmm(off, xb).astype(BF16))
    tb = x.astype(BF16)

    kkm = jnp.stack([kk_p[g] * head_a for g in range(gn)] + [kk_p[g] * head_b for g in range(gn)])
    vm = jnp.stack([v_p[g] * head_a for g in range(gn)] + [v_p[g] * head_b for g in range(gn)])
    wk = _bmm(tb, kkm)
    u0 = _bmm(tb, _bmm(lk.astype(BF16), vm).astype(BF16))
    arbb = arb.astype(BF16)
    qw = _bmm(arbb, wk.astype(BF16))
    y0 = _bmm(jnp.concatenate([ark.astype(BF16), -arbb], axis=2),
              jnp.concatenate([vm, u0.astype(BF16)], axis=1))
    pair = lambda t: t[:gn] + t[gn:]
    wk_p = pair(wk).astype(BF16)
    u0_p = pair(u0).astype(BF16)
    qhat = (jnp.stack(rt_p).astype(F32) - pair(qw)).astype(BF16)
    y0_p = pair(y0)
    vu = jnp.concatenate([jnp.stack(v_p), u0_p], axis=1)

    h = h_scr[...]
    chunks = range(n // CHUNK)
    for j in (reversed(chunks) if reverse else chunks):
        in_chunk = ((col // CHUNK) == j).astype(BF16)
        kj = kbt * in_chunk
        bj = bbt * in_chunk
        hloc = jnp.where(same, _bmm(jnp.concatenate([kj, -bj], axis=2), vu), 0.0)
        wb = jnp.where(same, _bmm(bj, wk_p), 0.0).astype(BF16)
        gamma = jnp.stack([jnp.broadcast_to(gam_ref[0, j:j + 1, lanes(g)], (n, n)).T for g in range(gn)])
        hb = h.astype(BF16)
        rows = slice(j * CHUNK, (j + 1) * CHUNK)
        yj = _bmm(qhat[:, rows], hb) + y0_p[:, rows]
        for g in range(gn):
            y_ref[rows, lanes(g)] = yj[g]
        h = gamma * (h - _bmm(wb, hb) + hloc)
    h_scr[...] = h

    @pl.when(c == nc - 1)
    def _():
        hfin_ref[...] = h


def rw_scan(rt, kkt, v, kbar_t, bbar_t, gam, h0, d):
    t = v.shape[0]
    nsc = t // SC_T
    reverse = d == 1
    gw = SCAN_G * LANES
    tix = (lambda c: nsc - 1 - c) if reverse else (lambda c: c)
    tm_spec = pl.BlockSpec((1, SC_T, gw), lambda p, c: (d, tix(c), p))
    cm_spec = pl.BlockSpec((1, gw, SC_T), lambda p, c: (d, p, tix(c)))
    st_spec = pl.BlockSpec((SCAN_G, LANES, LANES), lambda p, c: (p, 0, 0))
    return pl.pallas_call(
        functools.partial(_rw_scan_body, reverse=reverse),
        grid=(RW_PAIRS // SCAN_G, nsc),
        in_specs=[tm_spec, tm_spec, pl.BlockSpec((SC_T, gw), lambda p, c: (tix(c), p)), cm_spec, cm_spec,
                  pl.BlockSpec((1, SUBLANES, gw), lambda p, c: (d, tix(c), p)), st_spec],
        out_specs=[pl.BlockSpec((SC_T, gw), lambda p, c: (tix(c), p)), st_spec],
        out_shape=[jax.ShapeDtypeStruct((t, RW_W), F32),
                   jax.ShapeDtypeStruct((RW_PAIRS, LANES, LANES), F32)],
        scratch_shapes=[pltpu.VMEM((SCAN_G, LANES, LANES), F32)],
        compiler_params=_cparams(2),
        name="rw_scan_rev" if reverse else "rw_scan_fwd",
    )(rt, kkt, v, kbar_t, bbar_t, gam, h0)


def _rw_post_body(yf_ref, yb_ref, bv_ref, g_ref, lg_ref, lb_ref, e_ref, et_ref, o_ref):
    y = yf_ref[...] + yb_ref[...]
    mu = _seg_sum_bcast(y, e_ref, et_ref) * (1.0 / RW_HEAD)
    dlt = y - mu
    var = _seg_sum_bcast(dlt * dlt, e_ref, et_ref) * (1.0 / RW_HEAD)
    yn = dlt * lax.rsqrt(var + GN_EPS)
    o_ref[...] = ((yn * lg_ref[...] + lb_ref[...] + bv_ref[...]) * g_ref[...]).astype(o_ref.dtype)


def rw_post(yf, yb, bv, g, lnx_g, lnx_b):
    t = yf.shape[0]
    tm = 256
    _, _, e, et = _rw_consts()
    tokspec = pl.BlockSpec((tm, RW_W), lambda i: (i, 0))
    row = pl.BlockSpec((1, RW_W), lambda i: (0, 0))
    return pl.pallas_call(
        _rw_post_body,
        grid=(t // tm,),
        in_specs=[tokspec, tokspec, tokspec, tokspec, row, row,
                  pl.BlockSpec((RW_W, LANES), lambda i: (0, 0)), pl.BlockSpec((LANES, RW_W), lambda i: (0, 0))],
        out_specs=tokspec,
        out_shape=jax.ShapeDtypeStruct((t, RW_W), BF16),
        compiler_params=_cparams(1),
        name="rw_post",
    )(yf, yb, bv, g, lnx_g, lnx_b, e, et)


def rwkv_group(u, prm, h0_f, h0_b, with_output):
    v, g, bv, rt, kkt, kbar_t, bbar_t, gam = rw_prep(u, prm)
    outs = []
    finals = []
    for d, h0 in ((0, h0_f), (1, h0_b)):
        y, hfin = rw_scan(rt, kkt, v, kbar_t, bbar_t, gam, h0, d)
        outs.append(y)
        finals.append(hfin)
    if not with_output:
        return None, finals[0], finals[1]
    out = rw_post(outs[0], outs[1], bv, g, prm["lnx_g"], prm["lnx_b"])
    return out, finals[0], finals[1]


NA_RQ = 4


def _na_body(q_ref, k_ref, v_ref, kc_ref, vc_ref, bt_ref, o_ref, *, n_rows):
    rb = pl.program_id(1)
    scale = NA_HEAD ** -0.5
    kc = kc_ref[...]
    vc = vc_ref[...]
    for qi in range(NA_RQ):
        r = rb * NA_RQ + qi
        start = jnp.clip(r - NA_KH // 2, 0, n_rows - NA_KH)
        dr0 = start - r + NA_KH - 1
        tok0 = pl.multiple_of(start * GRID_W, GRID_W)
        q = q_ref[qi * GRID_W:(qi + 1) * GRID_W, :]
        kw = k_ref[pl.ds(tok0, NA_KH * GRID_W), :]
        vw = v_ref[pl.ds(tok0, NA_KH * GRID_W), :]
        s = _dot_nt(q, kw) * scale
        bias = jnp.concatenate([bt_ref[0, dr0 + 2 * m] for m in range(NA_KH // 2)], axis=1)
        s = s + bias
        sc = _dot_nt(q, kc) * scale
        mx = jnp.maximum(jnp.max(s, axis=-1, keepdims=True), jnp.max(sc, axis=-1, keepdims=True))
        p = jnp.exp(s - mx)
        pc = jnp.exp(sc - mx)
        den = jnp.sum(p, axis=-1, keepdims=True) + jnp.sum(pc, axis=-1, keepdims=True)
        o = (_dot(p.astype(BF16), vw) + _dot(pc.astype(BF16), vc)) / den
        o_ref[qi * GRID_W:(qi + 1) * GRID_W, :] = o.astype(o_ref.dtype)


def _na_bias_table(rpb):
    cols = jnp.arange(GRID_W)
    col_start = jnp.clip(cols - NA_KW // 2, 0, GRID_W - NA_KW)
    col_ok = (cols[None, :] >= col_start[:, None]) & (cols[None, :] < col_start[:, None] + NA_KW)
    dc = jnp.clip(cols[None, :] - cols[:, None] + NA_KW - 1, 0, 2 * NA_KW - 2)
    bt = rpb.astype(F32)[:, :, dc]
    bt = jnp.where(col_ok[None, None], bt, NEG_INF)
    return jnp.concatenate([bt[:, :-1], bt[:, 1:]], axis=-1)


def na_latent(att_l, att_c, rpb):
    s_len = att_l.shape[0]
    n_rows = s_len // GRID_W
    qb = NA_W // NA_HEAD
    bt = _na_bias_table(rpb)
    return pl.pallas_call(
        functools.partial(_na_body, n_rows=n_rows),
        grid=(NA_HEADS, n_rows // NA_RQ),
        in_specs=[
            pl.BlockSpec((NA_RQ * GRID_W, NA_HEAD), lambda h, r: (r, h)),
            pl.BlockSpec((s_len, NA_HEAD), lambda h, r: (0, qb + h)),
            pl.BlockSpec((s_len, NA_HEAD), lambda h, r: (0, 2 * qb + h)),
            pl.BlockSpec((CTX_LEN, NA_HEAD), lambda h, r: (0, qb + h)),
            pl.BlockSpec((CTX_LEN, NA_HEAD), lambda h, r: (0, 2 * qb + h)),
            pl.BlockSpec((1, 2 * NA_KH - 2, GRID_W, 2 * GRID_W), lambda h, r: (h, 0, 0, 0)),
        ],
        out_specs=pl.BlockSpec((NA_RQ * GRID_W, NA_HEAD), lambda h, r: (r, h)),
        out_shape=jax.ShapeDtypeStruct((s_len, NA_W), BF16),
        compiler_params=_cparams(2),
        name="na_latent",
    )(att_l, att_l, att_l, att_c, att_c, bt)


def _na_ctx_body(q_ref, k_ref, v_ref, o_ref):
    s = _dot_nt(q_ref[...], k_ref[...]) * (NA_HEAD ** -0.5)
    mx = jnp.max(s, axis=-1, keepdims=True)
    p = jnp.exp(s - mx)
    den = jnp.sum(p, axis=-1, keepdims=True)
    o_ref[...] = (_dot(p.astype(BF16), v_ref[...]) / den).astype(o_ref.dtype)


def na_ctx(att_c):
    qb = NA_W // NA_HEAD
    blk = lambda off: pl.BlockSpec((CTX_LEN, NA_HEAD), lambda h: (0, off + h))
    return pl.pallas_call(
        _na_ctx_body,
        grid=(NA_HEADS,),
        in_specs=[blk(0), blk(qb), blk(2 * qb)],
        out_specs=blk(0),
        out_shape=jax.ShapeDtypeStruct((CTX_LEN, NA_W), BF16),
        compiler_params=_cparams(1),
        name="na_ctx",
    )(att_c, att_c, att_c)


def _rope_body(x_ref, c_ref, s1_ref, s2_ref, o_ref):
    x = x_ref[...].astype(F32)
    quarter = SW_HEAD // 4
    x_up = pltpu.roll(x, LANES - quarter, 1)
    x_dn = pltpu.roll(x, quarter, 1)
    o_ref[...] = (x * c_ref[...] + x_up * s1_ref[...] + x_dn * s2_ref[...]).astype(o_ref.dtype)


def _rope_tables(s_len):
    t = jnp.arange(s_len)
    row = (t // GRID_W).astype(F32)
    col = (t % GRID_W).astype(F32)
    half = SW_HEAD // 2
    inv = ROPE_BASE ** (-jnp.arange(0, half, 2, dtype=F32) / half)
    ang_r = row[:, None] * inv[None, :]
    ang_c = col[:, None] * inv[None, :]
    cos = jnp.concatenate([jnp.cos(ang_r)] * 2 + [jnp.cos(ang_c)] * 2, axis=-1)
    sin_r, sin_c = jnp.sin(ang_r), jnp.sin(ang_c)
    zero = jnp.zeros_like(sin_r)
    s_up = jnp.concatenate([-sin_r, zero, -sin_c, zero], axis=-1)
    s_dn = jnp.concatenate([zero, sin_r, zero, sin_c], axis=-1)
    tile2 = lambda a: jnp.concatenate([a, a], axis=-1)
    return tile2(cos), tile2(s_up), tile2(s_dn)


def rope_qk(att_l):
    s_len = att_l.shape[0]
    tm = min(2048, s_len)
    nblk = (SW_W + SW_KV * SW_HEAD) // LANES
    off = NA_COLS // LANES
    cos, s_up, s_dn = _rope_tables(s_len)
    tab = pl.BlockSpec((tm, LANES), lambda i, j: (i, 0))
    return pl.pallas_call(
        _rope_body,
        grid=(s_len // tm, nblk),
        in_specs=[pl.BlockSpec((tm, LANES), lambda i, j: (i, off + j)), tab, tab, tab],
        out_specs=pl.BlockSpec((tm, LANES), lambda i, j: (i, j)),
        out_shape=jax.ShapeDtypeStruct((s_len, nblk * LANES), BF16),
        compiler_params=_cparams(2),
        name="rope_qk",
    )(att_l, cos, s_up, s_dn)


def _sw_softmax_out(q8, s_parts, v_parts, sink_col):
    mx = sink_col
    for s in s_parts:
        mx = jnp.maximum(mx, jnp.max(s, axis=-1, keepdims=True))
    den = jnp.exp(sink_col - mx)
    o = None
    for s, v in zip(s_parts, v_parts):
        p = jnp.exp(s - mx)
        den = den + jnp.sum(p, axis=-1, keepdims=True)
        pv = _dot(p.astype(BF16), v)
        o = pv if o is None else o + pv
    return o / den


def _sink_column(sink_ref, kh, rows_per_head):
    n = SW_GROUP * rows_per_head
    grp = lax.broadcasted_iota(jnp.int32, (n, 1), 0) // rows_per_head
    col = jnp.zeros((n, 1), F32)
    for g in range(SW_GROUP):
        col = jnp.where(grp == g, sink_ref[kh * SW_GROUP + g], col)
    return col


def _sw_body(sink_ref, q_ref, k_ref, v_ref, kc_ref, vc_ref, o_ref, *, s_len):
    n = pl.program_id(0)
    scale = SW_HEAD ** -0.5
    win = 3 * SW_BLOCK
    start = pl.multiple_of(jnp.clip((n - 1) * SW_BLOCK, 0, s_len - win), SW_BLOCK)
    k2 = k_ref[pl.ds(start, win), :]
    v2 = v_ref[pl.ds(start, win), :]
    qpos = n * SW_BLOCK + lax.broadcasted_iota(jnp.int32, (SW_BLOCK, win), 0)
    kpos = start + lax.broadcasted_iota(jnp.int32, (SW_BLOCK, win), 1)
    ok = jnp.abs(kpos - qpos) <= SW_WIN
    for kh in range(SW_KV):
        lanes = slice(kh * SW_HEAD, (kh + 1) * SW_HEAD)
        q8 = jnp.concatenate(
            [q_ref[:, (kh * SW_GROUP + g) * SW_HEAD:(kh * SW_GROUP + g + 1) * SW_HEAD] for g in range(SW_GROUP)],
            axis=0)
        s_loc = _dot_nt(q8, k2[:, lanes]) * scale
        s_loc = jnp.where(ok[None], s_loc.reshape(SW_GROUP, SW_BLOCK, win), NEG_INF).reshape(
            SW_GROUP * SW_BLOCK, win)
        s_ctx = _dot_nt(q8, kc_ref[:, lanes]) * scale
        o = _sw_softmax_out(q8, [s_loc, s_ctx], [v2[:, lanes], vc_ref[:, lanes]],
                            _sink_column(sink_ref, kh, SW_BLOCK))
        for g in range(SW_GROUP):
            c0 = (kh * SW_GROUP + g) * SW_HEAD
            o_ref[:, c0:c0 + SW_HEAD] = o[g * SW_BLOCK:(g + 1) * SW_BLOCK].astype(o_ref.dtype)


def swa_latent(qk_rot, att_l, att_c, sink):
    s_len = qk_rot.shape[0]
    kcol = SW_W // LANES
    ck = (NA_COLS + SW_W) // LANES
    return pl.pallas_call(
        functools.partial(_sw_body, s_len=s_len),
        grid=(s_len // SW_BLOCK,),
        in_specs=[
            pl.BlockSpec(memory_space=pltpu.SMEM),
            pl.BlockSpec((SW_BLOCK, SW_W), lambda n: (n, 0)),
            pl.BlockSpec((s_len, LANES), lambda n: (0, kcol)),
            pl.BlockSpec((s_len, LANES), lambda n: (0, ck + 1)),
            pl.BlockSpec((CTX_LEN, LANES), lambda n: (0, ck)),
            pl.BlockSpec((CTX_LEN, LANES), lambda n: (0, ck + 1)),
        ],
        out_specs=pl.BlockSpec((SW_BLOCK, SW_W), lambda n: (n, 0)),
        out_shape=jax.ShapeDtypeStruct((s_len, SW_W), BF16),
        compiler_params=_cparams(1),
        name="swa_latent",
    )(sink, qk_rot, qk_rot, att_l, att_c, att_c)


def _sw_ctx_body(sink_ref, q_ref, k_ref, v_ref, o_ref):
    scale = SW_HEAD ** -0.5
    for kh in range(SW_KV):
        lanes = slice(kh * SW_HEAD, (kh + 1) * SW_HEAD)
        q8 = jnp.concatenate(
            [q_ref[:, (kh * SW_GROUP + g) * SW_HEAD:(kh * SW_GROUP + g + 1) * SW_HEAD] for g in range(SW_GROUP)],
            axis=0)
        s = _dot_nt(q8, k_ref[:, lanes]) * scale
        o = _sw_softmax_out(q8, [s], [v_ref[:, lanes]], _sink_column(sink_ref, kh, CTX_LEN))
        for g in range(SW_GROUP):
            c0 = (kh * SW_GROUP + g) * SW_HEAD
            o_ref[:, c0:c0 + SW_HEAD] = o[g * CTX_LEN:(g + 1) * CTX_LEN].astype(o_ref.dtype)


def swa_ctx(att_c, sink):
    qo = NA_COLS // SW_W
    ck = (NA_COLS + SW_W) // LANES
    return pl.pallas_call(
        _sw_ctx_body,
        grid=(1,),
        in_specs=[
            pl.BlockSpec(memory_space=pltpu.SMEM),
            pl.BlockSpec((CTX_LEN, SW_W), lambda n: (0, qo)),
            pl.BlockSpec((CTX_LEN, LANES), lambda n: (0, ck)),
            pl.BlockSpec((CTX_LEN, LANES), lambda n: (0, ck + 1)),
        ],
        out_specs=pl.BlockSpec((CTX_LEN, SW_W), lambda n: (0, 0)),
        out_shape=jax.ShapeDtypeStruct((CTX_LEN, SW_W), BF16),
        compiler_params=_cparams(1),
        name="swa_ctx",
    )(sink, att_c, att_c, att_c)


def _router_body(x_ref, g_ref, sc_ref, sh_ref, wr_ref, rb_ref, h_ref, pick_ref, hp_ref):
    x = x_ref[...]
    tm = x.shape[0]
    ms = jnp.mean(x * x, axis=-1, keepdims=True)
    h = (x * lax.rsqrt(ms + NORM_EPS) * g_ref[...]) * (1.0 + sc_ref[...]) + sh_ref[...]
    h_ref[...] = h.astype(h_ref.dtype)

    hh, hl = _split2(h)
    wh, wl = _split2(wr_ref[...])
    logits = _dot(hh, wh) + _dot(hl, wh) + _dot(hh, wl)
    lane = lax.broadcasted_iota(jnp.int32, (tm, LANES), 1)
    lane_f = lane.astype(F32)
    valid = lane < N_EXPERTS
    neg = -jnp.inf
    big = float(2 * LANES)
    scores = _sigmoid(logits)
    biased = jnp.where(valid, scores + rb_ref[...], neg)
    grp = lane // EXPERTS_PER_GROUP

    def first_argmax(vals):
        m = jnp.max(vals, axis=-1, keepdims=True)
        idx = jnp.min(jnp.where(vals == m, lane_f, big), axis=-1, keepdims=True)
        return m, idx

    gscore = jnp.full((tm, LANES), neg, F32)
    for gi in range(N_GROUPS):
        vg = jnp.where(grp == gi, biased, neg)
        m1, i1 = first_argmax(vg)
        m2 = jnp.max(jnp.where(lane_f == i1, neg, vg), axis=-1, keepdims=True)
        gscore = jnp.where(lane == gi * EXPERTS_PER_GROUP, m1 + m2, gscore)
    keep = jnp.zeros((tm, LANES), jnp.bool_)
    for _ in range(TOPK_GROUPS):
        _, gi = first_argmax(gscore)
        keep = keep | (grp.astype(F32) == jnp.floor(gi * (1.0 / EXPERTS_PER_GROUP)))
        gscore = jnp.where(lane_f == gi, neg, gscore)
    cur = jnp.where(valid, jnp.where(keep, biased, NEG_INF), neg)
    sel = jnp.zeros((tm, LANES), jnp.bool_)
    picks = jnp.zeros((tm, LANES), F32)
    for k in range(TOP_K):
        _, ei = first_argmax(cur)
        hit = lane_f == ei
        sel = sel | hit
        cur = jnp.where(hit, neg, cur)
        picks = jnp.where(lane == k, ei, picks)
        picks = jnp.where(lane == TOP_K + k, jnp.sum(jnp.where(hit, scores, 0.0), axis=-1, keepdims=True), picks)
    w = jnp.where(sel, scores, 0.0)
    norm = ROUTE_SCALE / jnp.sum(w, axis=-1, keepdims=True)
    pick_ref[...] = jnp.where((lane >= TOP_K) & (lane < 2 * TOP_K), picks * norm, picks)

    bits = pltpu.bitcast(h.astype(BF16).astype(F32), jnp.uint32)
    half = D_MODEL // 2
    words = (bits[:, half:] & jnp.uint32(0xFFFF0000)) | (bits[:, :half] >> 16)
    for a in range(TOK_ROWS):
        hp_ref[pl.ds(a, tm, stride=TOK_ROWS), :] = words[:, a * LANES:(a + 1) * LANES]


def router(x, g, sc, sh, w_router_pad, r_bias_pad):
    t = x.shape[0]
    tm = 256
    row = pl.BlockSpec((1, D_MODEL), lambda i: (0, 0))
    return pl.pallas_call(
        _router_body,
        grid=(t // tm,),
        in_specs=[pl.BlockSpec((tm, D_MODEL), lambda i: (i, 0)), row, row, row,
                  pl.BlockSpec((D_MODEL, LANES), lambda i: (0, 0)), pl.BlockSpec((1, LANES), lambda i: (0, 0))],
        out_specs=[pl.BlockSpec((tm, D_MODEL), lambda i: (i, 0)), pl.BlockSpec((tm, LANES), lambda i: (i, 0)),
                   pl.BlockSpec((tm * TOK_ROWS, LANES), lambda i: (i, 0))],
        out_shape=[jax.ShapeDtypeStruct((t, D_MODEL), BF16), jax.ShapeDtypeStruct((t, LANES), F32),
                   jax.ShapeDtypeStruct((t * TOK_ROWS, LANES), jnp.uint32)],
        compiler_params=_cparams(1),
        name="router",
    )(x, g, sc, sh, w_router_pad, r_bias_pad)


MOE_TM = 512
MOE_TM_SMALL = 64
PACK_W = D_MODEL // 2
TOK_ROWS = PACK_W // LANES
TOK_PITCH = TOK_ROWS + 4
MOE_DOWN_COLS = 2 * LANES


def _unpack_bf16_pairs(words):
    lo = pltpu.bitcast(words << 16, F32)
    hi = pltpu.bitcast(words & jnp.uint32(0xFFFF0000), F32)
    return lo, hi


def _pack_bf16_pairs(lo, hi):
    lo_bits = pltpu.bitcast(lo.astype(BF16).astype(F32), jnp.uint32) >> 16
    hi_bits = pltpu.bitcast(hi.astype(BF16).astype(F32), jnp.uint32) & jnp.uint32(0xFFFF0000)
    return hi_bits | lo_bits


def _routed_body(te_ref, nv_ref, src_ref, dst_ref, hp_hbm, wg_ref, wu_ref, wd_ref, yk_hbm, xbuf, ybuf, gsem, ssem,
                 *, tm, n_real_rows):
    s = pl.program_id(0)
    nv = nv_ref[0]

    tile_rows = tm * TOK_ROWS

    def gather_wait(slot):
        pltpu.make_async_copy(hp_hbm.at[pl.ds(0, tile_rows)], xbuf.at[slot, pl.ds(0, tile_rows)],
                              gsem.at[slot]).wait()

    def scatter_wait(slot):
        pltpu.make_async_copy(ybuf.at[slot, pl.ds(0, tile_rows)], yk_hbm.at[pl.ds(0, tile_rows)],
                              ssem.at[slot]).wait()

    def start_gathers(slot, rows):
        for r in rows:
            pltpu.make_async_copy(hp_hbm.at[pl.ds(src_ref[0, 0, r], TOK_ROWS)],
                                  xbuf.at[slot, pl.ds(r * TOK_PITCH, TOK_ROWS)], gsem.at[slot]).start(priority=r % 2)

    @pl.when(s == 0)
    def _():
        ybuf[1, pl.ds(0, tile_rows), :] = jnp.zeros((tile_rows, LANES), jnp.uint32)
        fill = pltpu.make_async_copy(ybuf.at[1, pl.ds(0, tile_rows)],
                                     yk_hbm.at[pl.ds(n_real_rows * TOK_ROWS, tile_rows)], ssem.at[1])
        fill.start()
        fill.wait()
        start_gathers(0, range(tm))

    @pl.when((s >= 1) & (s <= nv))
    def _():
        t = s - 1
        slot = t % 2
        gather_wait(slot)
        hg = jnp.zeros((tm, D_EXPERT), F32)
        hu = jnp.zeros((tm, D_EXPERT), F32)
        rows_per_chunk = tm // TOK_ROWS
        for a in range(TOK_ROWS):
            x_lo, x_hi = _unpack_bf16_pairs(xbuf[slot, pl.ds(a, tm, stride=TOK_PITCH), :])
            x_a = jnp.concatenate([x_lo.astype(BF16), x_hi.astype(BF16)], axis=1)
            lo_rows = slice(a * LANES, (a + 1) * LANES)
            hi_rows = slice(PACK_W + a * LANES, PACK_W + (a + 1) * LANES)
            hg = hg + _dot(x_a, jnp.concatenate([wg_ref[0, 0, lo_rows], wg_ref[0, 0, hi_rows]], axis=0).astype(BF16))
            hu = hu + _dot(x_a, jnp.concatenate([wu_ref[0, 0, lo_rows], wu_ref[0, 0, hi_rows]], axis=0).astype(BF16))
            start_gathers(1 - slot, range(a * rows_per_chunk, (a + 1) * rows_per_chunk))
        act = ((hg * _sigmoid(hg)) * hu).astype(BF16)
        for c in range(PACK_W // MOE_DOWN_COLS):
            c0 = c * MOE_DOWN_COLS
            y_lo = _dot(act, wd_ref[0, 0, :, c0:c0 + MOE_DOWN_COLS].astype(BF16))
            y_hi = _dot(act, wd_ref[0, 0, :, PACK_W + c0:PACK_W + c0 + MOE_DOWN_COLS].astype(BF16))
            words = _pack_bf16_pairs(y_lo, y_hi)
            for j in range(MOE_DOWN_COLS // LANES):
                a = c * (MOE_DOWN_COLS // LANES) + j
                ybuf[slot, pl.ds(a, tm, stride=TOK_PITCH), :] = words[:, j * LANES:(j + 1) * LANES]

        @pl.when(t >= 1)
        def _():
            scatter_wait(1 - slot)

        for r in range(tm):
            pltpu.make_async_copy(ybuf.at[slot, pl.ds(r * TOK_PITCH, TOK_ROWS)],
                                  yk_hbm.at[pl.ds(dst_ref[0, 0, r], TOK_ROWS)], ssem.at[slot]).start(priority=r % 2)

        @pl.when(s == nv)
        def _():
            scatter_wait(slot)
            gather_wait(1 - slot)


def _dispatch_plan(picks, t, tm):
    n_pairs = TOP_K * t
    n_tiles = n_pairs // tm + N_EXPERTS
    e_flat = picks[:, :TOP_K].astype(jnp.int32).reshape(-1)
    pair = jnp.arange(n_pairs, dtype=jnp.int32)
    dst = (pair % TOP_K) * t + pair // TOP_K
    experts = jnp.arange(N_EXPERTS, dtype=jnp.int32)
    counts = jnp.sum((e_flat[:, None] == experts[None, :]).astype(jnp.int32), axis=0)
    fill = (-counts) % tm
    filler_e = jnp.repeat(experts, tm)
    filler_j = jnp.tile(jnp.arange(tm, dtype=jnp.int32), N_EXPERTS)
    filler_key = jnp.where(filler_j < jnp.repeat(fill, tm), 2 * filler_e + 1, 2 * N_EXPERTS)
    dst_bits = (n_pairs + tm - 1).bit_length()
    assert 2 * N_EXPERTS < (1 << (31 - dst_bits))
    keys = jnp.concatenate([2 * e_flat, filler_key]) * (1 << dst_bits) + jnp.concatenate([dst, n_pairs + filler_j])
    dst_sorted = lax.sort(keys) & ((1 << dst_bits) - 1)
    src_sorted = jnp.where(dst_sorted < n_pairs, dst_sorted % t, 0)
    tile_end = jnp.cumsum((counts + fill) // tm)
    n_valid = tile_end[-1].astype(jnp.int32)
    ti = jnp.minimum(jnp.arange(n_tiles, dtype=jnp.int32), n_valid - 1)
    tile_e = jnp.sum((tile_end[None, :] <= ti[:, None]).astype(jnp.int32), axis=1)
    return (tile_e.astype(jnp.int32), n_valid.reshape(1), (src_sorted * TOK_ROWS).reshape(n_tiles, 1, tm),
            (dst_sorted * TOK_ROWS).reshape(n_tiles, 1, tm))


def moe_routed(hp, picks, layer, we_gate, we_up, we_down):
    t = hp.shape[0] // TOK_ROWS
    tm = MOE_TM if TOP_K * t >= N_EXPERTS * MOE_TM else MOE_TM_SMALL
    tile_e, n_valid, src, dst = _dispatch_plan(picks, t, tm)
    n_tiles = tile_e.shape[0]
    last = n_tiles - 1
    cur = lambda s: jnp.minimum(s, last)
    prev = lambda s: jnp.clip(s - 1, 0, last)
    smem = lambda f: pl.BlockSpec((1, 1, tm), lambda s, te, nv: (f(s), 0, 0), memory_space=pltpu.SMEM)
    wspec = lambda shp: pl.BlockSpec((1, 1) + shp, lambda s, te, nv: (layer, te[prev(s)], 0, 0))
    grid_spec = pltpu.PrefetchScalarGridSpec(
        num_scalar_prefetch=2,
        grid=(n_tiles + 1,),
        in_specs=[
            smem(cur), smem(prev),
            pl.BlockSpec(memory_space=pl.ANY),
            wspec((D_MODEL, D_EXPERT)), wspec((D_MODEL, D_EXPERT)), wspec((D_EXPERT, D_MODEL)),
        ],
        out_specs=pl.BlockSpec(memory_space=pl.ANY),
        scratch_shapes=[pltpu.VMEM((2, tm * TOK_PITCH, LANES), jnp.uint32),
                        pltpu.VMEM((2, tm * TOK_PITCH, LANES), jnp.uint32),
                        pltpu.SemaphoreType.DMA((2,)), pltpu.SemaphoreType.DMA((2,))],
    )
    return pl.pallas_call(
        functools.partial(_routed_body, tm=tm, n_real_rows=TOP_K * t),
        grid_spec=grid_spec,
        out_shape=jax.ShapeDtypeStruct(((TOP_K * t + tm) * TOK_ROWS, LANES), jnp.uint32),
        compiler_params=_cparams(1),
        name="moe_routed",
    )(tile_e, n_valid, src, dst, hp, we_gate, we_up, we_down)


COMBINE_TM = 128


def _combine_body(h_ref, pick_ref, *rest):
    yk_refs = rest[:TOP_K]
    wg_ref, wu_ref, wd_ref, x_ref, ga_ref, o_ref = rest[TOP_K:]
    h = h_ref[...]
    hg = _dot(h, wg_ref[...])
    hu = _dot(h, wu_ref[...])
    act = ((hg * _sigmoid(hg)) * hu).astype(BF16)
    picks = pick_ref[...]
    tm = h.shape[0]
    shared = _dot(act, wd_ref[...])
    for a in range(TOK_ROWS):
        acc_lo = shared[:, a * LANES:(a + 1) * LANES]
        acc_hi = shared[:, PACK_W + a * LANES:PACK_W + (a + 1) * LANES]
        for k in range(TOP_K):
            lo, hi = _unpack_bf16_pairs(yk_refs[k][pl.ds(a, tm, stride=TOK_ROWS), :])
            w_k = picks[:, TOP_K + k:TOP_K + k + 1]
            acc_lo = acc_lo + w_k * lo
            acc_hi = acc_hi + w_k * hi
        for base, acc in ((0, acc_lo), (PACK_W, acc_hi)):
            cols = slice(base + a * LANES, base + (a + 1) * LANES)
            o_ref[:, cols] = x_ref[:, cols] + ga_ref[:, cols] * acc


def moe_combine(h, picks, yk, ws_gate, ws_up, ws_down, x, ga):
    t = h.shape[0]
    tm = COMBINE_TM
    nb = t // tm
    tok = lambda cols: pl.BlockSpec((tm, cols), lambda i: (i, 0))
    full = lambda a: pl.BlockSpec(a.shape, lambda i: (0,) * a.ndim)
    slot = lambda k: pl.BlockSpec((tm * TOK_ROWS, LANES), lambda i: (k * nb + i, 0))
    return pl.pallas_call(
        _combine_body,
        grid=(nb,),
        in_specs=[tok(D_MODEL), tok(LANES)] + [slot(k) for k in range(TOP_K)] + [
            full(ws_gate), full(ws_up), full(ws_down), tok(D_MODEL), pl.BlockSpec((1, D_MODEL), lambda i: (0, 0))],
        out_specs=tok(D_MODEL),
        out_shape=jax.ShapeDtypeStruct((t, D_MODEL), F32),
        compiler_params=_cparams(1),
        name="moe_combine",
    )(h, picks, *([yk] * TOP_K), ws_gate, ws_up, ws_down, x, ga)


def _pad_cols(a, n):
    return jnp.pad(a, [(0, 0)] * (a.ndim - 1) + [(0, n - a.shape[-1])])


def _rw_params(i, rw_conv, rw_w0, rw_w2, rw_a0, rw_a2, rw_g2, rw_kk, rw_ka, rw_rk, rw_lnx_g, rw_lnx_b):
    w2 = jnp.zeros((2, LORA_ALL, RW_W), F32)
    a2 = jnp.zeros((2, LORA_ALL, RW_W), F32)
    for d in range(2):
        w2 = w2.at[d, d * LORA_W:(d + 1) * LORA_W].set(rw_w2[i, d])
        a2 = a2.at[d, 2 * LORA_W + d * LORA_A:2 * LORA_W + (d + 1) * LORA_A].set(rw_a2[i, d])
    taps = jnp.pad(rw_conv[i], ((0, SUBLANES - 3), (0, RW_COLS_PAD - RW_COLS)))
    return dict(
        taps=taps, w0=rw_w0[i], a0=rw_a0[i], w2=w2.astype(BF16), a2=a2.astype(BF16), g2=rw_g2[i].astype(BF16),
        kks=rw_kk[i][None], ka=rw_ka[i][None], rk=rw_rk[i].reshape(1, RW_W),
        lnx_g=rw_lnx_g[i][None], lnx_b=rw_lnx_b[i][None])


def kernel(x, c, ctx, c_ctx, w_ada, b_ada, norm1_g, norm2_g, w_in, rw_conv, rw_w0, rw_w2, rw_a0, rw_a2, rw_g2,
           rw_kk, rw_ka, rw_rk, rw_lnx_g, rw_lnx_b, na_rpb, sw_sink, w_out, w_router, router_bias, we_gate,
           we_up, we_down, ws_gate, ws_up, ws_down, final_g):
    assert x.shape[0] == 1 and x.shape[2] == D_MODEL and ctx.shape[1] == CTX_LEN
    xl = x[0]
    xc = ctx[0]
    cond8 = jnp.zeros((SUBLANES, D_MODEL), F32).at[0].set(c[0]).at[1].set(c_ctx)
    mods = ada_mod(cond8, w_ada, b_ada)
    w_in_bf16 = w_in.astype(BF16)

    for i in range(DEPTH):
        ctx_needed = i < DEPTH - 1
        mod_l = [m[None] for m in mods[i, 0].reshape(6, D_MODEL)]
        mod_c = [m[None] for m in mods[i, 1].reshape(6, D_MODEL)]
        w_rw = [(w_in_bf16, i, 0)]
        w_att = [(w_in_bf16[i][:, RW_COLS:][None], 0, 0)]
        w_o_parts = [(w_out, i, 0), (w_out, i, RW_W // NA_W), (w_out, i, (RW_W + NA_W) // SW_W)]
        rw_prm = _rw_params(i, rw_conv, rw_w0, rw_w2, rw_a0, rw_a2, rw_g2, rw_kk, rw_ka, rw_rk, rw_lnx_g,
                            rw_lnx_b)
        n1 = norm1_g[i][None]
        n2 = norm2_g[i][None]

        hl = norm_mod(xl, n1, mod_l[1], mod_l[0])
        hc = norm_mod(xc, n1, mod_c[1], mod_c[0])
        url = matmul([hl], w_rw, RW_COLS_PAD, 768, F32, name="in_proj_rw")
        urc = matmul([hc], w_rw, RW_COLS_PAD, 768, F32, name="in_proj_rw_ctx")
        ual = matmul([hl], w_att, ATT_COLS, 256, BF16, name="in_proj_att")
        uac = matmul([hc], w_att, ATT_COLS, 256, BF16, name="in_proj_att_ctx")

        z = jnp.zeros((RW_PAIRS, LANES, LANES), F32)
        rwc, s_f, s_b = rwkv_group(urc, rw_prm, z, z, ctx_needed)
        rwl, _, _ = rwkv_group(url, rw_prm, s_f, s_b, True)
        nal = na_latent(ual, uac, na_rpb[i])
        swl = swa_latent(rope_qk(ual), ual, uac, sw_sink[i])
        xl = matmul([rwl, nal, swl], w_o_parts, D_MODEL, 512, F32, residual=(xl, mod_l[2]), name="out_proj")

        wr = _pad_cols(w_router[i], LANES)
        rb = _pad_cols(router_bias[i][None], LANES)
        shared = (ws_gate[i].astype(BF16), ws_up[i].astype(BF16), ws_down[i].astype(BF16))

        def moe(xt, mod):
            h2, picks, hp = router(xt, n2, mod[4], mod[3], wr, rb)
            yk = moe_routed(hp, picks, i, we_gate, we_up, we_down)
            return moe_combine(h2, picks, yk, *shared, xt, mod[5])

        xl = moe(xl, mod_l)
        if ctx_needed:
            nac = na_ctx(uac)
            swc = swa_ctx(uac, sw_sink[i])
            xc = matmul([rwc, nac, swc], w_o_parts, D_MODEL, 512, F32, residual=(xc, mod_c[2]), name="out_proj_ctx")
            xc = moe(xc, mod_c)
    return final_norm(xl, final_g[None])[None]
```

```python
import functools

import jax
import jax.numpy as jnp
from jax import lax
from jax.experimental import pallas as pl
from jax.experimental.pallas import tpu as pltpu

F32 = jnp.float32
BF16 = jnp.bfloat16

D_MODEL = 4096
DEPTH = 2
GRID_W = 64
CTX_LEN = 256
NORM_EPS = 1e-6
NEG_INF = -1e30

RW_HEAD = 64
RW_W = D_MODEL // 2
RW_HEADS = RW_W // RW_HEAD
LORA_W = 96
LORA_A = 96
LORA_G = 256
GN_EPS = 64e-5
RW_COLS = 3 * RW_W + LORA_G + 2 * LORA_W + 2 * LORA_A
LORA_OFF = 3 * RW_W + LORA_G
LORA_ALL = 2 * LORA_W + 2 * LORA_A

NA_HEAD = 128
NA_W = D_MODEL // 4
NA_HEADS = NA_W // NA_HEAD
NA_KH = 8
NA_KW = 16
NA_COLS = 3 * NA_W

SW_HEAD = 64
SW_W = D_MODEL - RW_W - NA_W
SW_HEADS = SW_W // SW_HEAD
SW_KV = SW_HEADS // 8
SW_GROUP = SW_HEADS // SW_KV
SW_WIN = 128
SW_BLOCK = 128
ROPE_BASE = 10000.0
SW_COLS = SW_W + 2 * SW_KV * SW_HEAD
ATT_COLS = NA_COLS + SW_COLS

N_EXPERTS = 64
N_GROUPS = 8
EXPERTS_PER_GROUP = N_EXPERTS // N_GROUPS
TOPK_GROUPS = 4
TOP_K = 8
D_EXPERT = 256
ROUTE_SCALE = 2.5

LANES = 128
SUBLANES = 8
RW_COLS_PAD = 6912
CHUNK = 64
VMEM_LIMIT = 56 * 1024 * 1024


def _cparams(n_axes, vmem=VMEM_LIMIT):
    return pltpu.CompilerParams(dimension_semantics=("arbitrary",) * n_axes, vmem_limit_bytes=vmem)


def _dot(a, b):
    return jnp.dot(a, b, preferred_element_type=F32)


def _dot_nt(a, b):
    return lax.dot_general(a, b, (((1,), (1,)), ((), ())), preferred_element_type=F32)


def _split2(x):
    hi = x.astype(BF16)
    lo = (x - hi.astype(F32)).astype(BF16)
    return hi, lo


def _split3(x):
    hi = x.astype(BF16)
    r1 = x - hi.astype(F32)
    mid = r1.astype(BF16)
    lo = (r1 - mid.astype(F32)).astype(BF16)
    return hi, mid, lo


def _sigmoid(x):
    return 1.0 / (1.0 + jnp.exp(-x))


def _ada_body(s_ref, w_ref, b_ref, o_ref):
    c = s_ref[...]
    s = (c * _sigmoid(c)).astype(BF16)
    o_ref[0] = _dot(s, w_ref[0].astype(BF16)) + b_ref[0]


def ada_mod(cond8, w_ada, b_ada):
    bn = 512
    n = 6 * D_MODEL
    return pl.pallas_call(
        _ada_body,
        grid=(DEPTH, n // bn),
        in_specs=[
            pl.BlockSpec((SUBLANES, D_MODEL), lambda l, j: (0, 0)),
            pl.BlockSpec((1, D_MODEL, bn), lambda l, j: (l, 0, j)),
            pl.BlockSpec((1, 1, bn), lambda l, j: (l, 0, j)),
        ],
        out_specs=pl.BlockSpec((1, SUBLANES, bn), lambda l, j: (l, 0, j)),
        out_shape=jax.ShapeDtypeStruct((DEPTH, SUBLANES, n), F32),
        compiler_params=_cparams(2),
        name="ada_mod",
    )(cond8, w_ada, b_ada.reshape(DEPTH, 1, n))


def _norm_mod_body(x_ref, g_ref, sc_ref, sh_ref, o_ref):
    x = x_ref[...]
    ms = jnp.mean(x * x, axis=-1, keepdims=True)
    y = x * lax.rsqrt(ms + NORM_EPS) * g_ref[...]
    o_ref[...] = (y * (1.0 + sc_ref[...]) + sh_ref[...]).astype(o_ref.dtype)


def norm_mod(x, g, sc, sh, out_dtype=BF16):
    t = x.shape[0]
    tm = 256
    row = pl.BlockSpec((1, D_MODEL), lambda i: (0, 0))
    return pl.pallas_call(
        _norm_mod_body,
        grid=(t // tm,),
        in_specs=[pl.BlockSpec((tm, D_MODEL), lambda i: (i, 0)), row, row, row],
        out_specs=pl.BlockSpec((tm, D_MODEL), lambda i: (i, 0)),
        out_shape=jax.ShapeDtypeStruct((t, D_MODEL), out_dtype),
        compiler_params=_cparams(1),
        name="norm_mod",
    )(x, g, sc, sh)


def _final_norm_body(x_ref, g_ref, o_ref):
    x = x_ref[...]
    ms = jnp.mean(x * x, axis=-1, keepdims=True)
    o_ref[...] = x * lax.rsqrt(ms + NORM_EPS) * g_ref[...]


def final_norm(x, g):
    t = x.shape[0]
    tm = 256
    return pl.pallas_call(
        _final_norm_body,
        grid=(t // tm,),
        in_specs=[pl.BlockSpec((tm, D_MODEL), lambda i: (i, 0)), pl.BlockSpec((1, D_MODEL), lambda i: (0, 0))],
        out_specs=pl.BlockSpec((tm, D_MODEL), lambda i: (i, 0)),
        out_shape=jax.ShapeDtypeStruct((t, D_MODEL), F32),
        compiler_params=_cparams(1),
        name="final_norm",
    )(x, g)


def _mm_body(*refs, n_in, residual):
    a_refs = refs[:n_in]
    w_refs = refs[n_in:2 * n_in]
    rest = refs[2 * n_in:]
    acc = _dot(a_refs[0][...], w_refs[0][0].astype(BF16))
    for a_ref, w_ref in zip(a_refs[1:], w_refs[1:]):
        acc = acc + _dot(a_ref[...], w_ref[0].astype(BF16))
    if residual:
        x_ref, ga_ref, o_ref = rest
        o_ref[...] = x_ref[...] + ga_ref[...] * acc
    else:
        (o_ref,) = rest
        o_ref[...] = acc.astype(o_ref.dtype)


def matmul(a_list, w_list, n, bn, out_dtype, residual=None, name="matmul"):
    m = a_list[0].shape[0]
    bm = 1024 if m % 1024 == 0 else 256
    in_specs = [pl.BlockSpec((bm, a.shape[1]), lambda i, j: (i, 0)) for a in a_list]
    for a, (_, layer, row_blk) in zip(a_list, w_list):
        in_specs.append(pl.BlockSpec((1, a.shape[1], bn), lambda i, j, layer=layer, row_blk=row_blk: (layer, row_blk, j)))
    args = list(a_list) + [w for w, _, _ in w_list]
    if residual is not None:
        in_specs += [pl.BlockSpec((bm, bn), lambda i, j: (i, j)), pl.BlockSpec((1, bn), lambda i, j: (0, j))]
        args += list(residual)
    return pl.pallas_call(
        functools.partial(_mm_body, n_in=len(a_list), residual=residual is not None),
        grid=(m // bm, n // bn),
        in_specs=in_specs,
        out_specs=pl.BlockSpec((bm, bn), lambda i, j: (i, j)),
        out_shape=jax.ShapeDtypeStruct((m, n), out_dtype),
        compiler_params=_cparams(2),
        name=name,
    )(*args)


RW_TM = 128


def _seg_sum_bcast(x, e_ref, et_ref):
    hi, lo = _split2(x)
    s = _dot(hi, e_ref[...]) + _dot(lo, e_ref[...])
    shi, slo = _split2(s)
    return _dot(shi, et_ref[...]) + _dot(slo, et_ref[...])


def _rw_prep_body(u_ref, up_ref, un_ref, taps_ref, w0_ref, a0_ref, w2_ref, a2_ref, g2_ref, kks_ref, ka_ref,
                  rk_ref, e_ref, et_ref, tri_ref, sel_ref,
                  v_ref, g_ref, bv_ref, rt_ref, kkt_ref, kbar_ref, bbar_ref, gam_ref):
    i = pl.program_id(0)
    nt = pl.num_programs(0)
    tm = RW_TM
    rows = lax.broadcasted_iota(jnp.int32, (tm, 1), 0)
    has_prev = (i > 0).astype(F32)
    has_next = (i < nt - 1).astype(F32)

    def conv(c0, c1):
        u = u_ref[:, c0:c1]
        prev_row = up_ref[SUBLANES - 1:SUBLANES, c0:c1] * has_prev
        next_row = un_ref[0:1, c0:c1] * has_next
        u_prev = jnp.where(rows == 0, prev_row, pltpu.roll(u, 1, 0))
        u_next = jnp.where(rows == tm - 1, next_row, pltpu.roll(u, tm - 1, 0))
        return (u_prev * taps_ref[0:1, c0:c1] + u * taps_ref[1:2, c0:c1] + u_next * taps_ref[2:3, c0:c1])

    r = conv(0, RW_W)
    k = conv(RW_W, 2 * RW_W)
    v = conv(2 * RW_W, 3 * RW_W)
    gd = conv(3 * RW_W, LORA_OFF)
    lo_in = conv(LORA_OFF, LORA_OFF + LORA_ALL)

    v_ref[...] = v.astype(v_ref.dtype)
    g_ref[...] = _dot(_sigmoid(gd).astype(BF16), g2_ref[...])

    kk = k * kks_ref[...]
    ssq = _seg_sum_bcast(kk * kk, e_ref, et_ref)
    kkn = kk / jnp.maximum(jnp.sqrt(ssq), 1e-12)

    tanh_lo = jnp.tanh(lo_in).astype(BF16)
    raw_lo = lo_in.astype(BF16)
    bonus = jnp.zeros((tm, RW_W), F32)
    for d in range(2):
        w_pre = w0_ref[d:d + 1, :] + _dot(tanh_lo, w2_ref[d])
        z = -w_pre
        softplus = jnp.maximum(z, 0.0) + jnp.log(1.0 + jnp.exp(-jnp.abs(z)))
        w_log = -softplus - 0.5
        lw = -jnp.exp(w_log)
        a = _sigmoid(a0_ref[d:d + 1, :] + _dot(raw_lo, a2_ref[d]))
        kd = k * (1.0 + (a - 1.0) * ka_ref[...])
        b = kkn * a
        bonus = bonus + r * kd * rk_ref[...]
        l1, l2, l3 = _split3(lw)
        tri = tri_ref[d]
        sel = sel_ref[...]
        cum = _dot(tri, l1) + _dot(tri, l2) + _dot(tri, l3)
        tot = _dot(sel, l1) + _dot(sel, l2) + _dot(sel, l3)
        e_inv = jnp.exp(-cum)
        rt_ref[d] = (r * jnp.exp(cum)).astype(BF16)
        kkt_ref[d] = (kkn * jnp.exp(cum - lw)).astype(BF16)
        kbar_ref[d] = (kd * e_inv).T.astype(BF16)
        bbar_ref[d] = (b * e_inv).T.astype(BF16)
        gam_ref[d] = jnp.exp(tot)
    bv_ref[...] = _seg_sum_bcast(bonus, e_ref, et_ref) * v


def _rw_consts():
    tm = RW_TM
    t = jnp.arange(tm)
    same = (t[:, None] // CHUNK) == (t[None, :] // CHUNK)
    tri_f = same & (t[None, :] <= t[:, None])
    tri_b = same & (t[None, :] >= t[:, None])
    tri = jnp.stack([tri_f, tri_b]).astype(BF16)
    sel = ((t[None, :] // CHUNK) == jnp.arange(SUBLANES)[:, None]).astype(BF16)
    lane_head = jnp.arange(RW_W) // RW_HEAD
    e = (lane_head[:, None] == jnp.arange(LANES)[None, :]).astype(BF16)
    return tri, sel, e, e.T


def rw_prep(u, prm):
    t = u.shape[0]
    tm = RW_TM
    nt = t // tm
    hb = tm // SUBLANES
    tri, sel, e, et = _rw_consts()
    full = lambda shp: pl.BlockSpec(shp, lambda i: (0,) * len(shp))
    in_specs = [
        pl.BlockSpec((tm, RW_COLS_PAD), lambda i: (i, 0)),
        pl.BlockSpec((SUBLANES, RW_COLS_PAD), lambda i: (jnp.maximum(i * hb - 1, 0), 0)),
        pl.BlockSpec((SUBLANES, RW_COLS_PAD), lambda i: (jnp.minimum((i + 1) * hb, nt * hb - 1), 0)),
        full((SUBLANES, RW_COLS_PAD)),
        full((2, RW_W)), full((2, RW_W)),
        full((2, LORA_ALL, RW_W)), full((2, LORA_ALL, RW_W)),
        full((LORA_G, RW_W)),
        full((1, RW_W)), full((1, RW_W)), full((1, RW_W)),
        full((RW_W, LANES)), full((LANES, RW_W)),
        full((2, tm, tm)), full((SUBLANES, tm)),
    ]
    tok = lambda dt: jax.ShapeDtypeStruct((t, RW_W), dt)
    tok2 = lambda dt: jax.ShapeDtypeStruct((2, t, RW_W), dt)
    spec1 = pl.BlockSpec((tm, RW_W), lambda i: (i, 0))
    spec2 = pl.BlockSpec((2, tm, RW_W), lambda i: (0, i, 0))
    chan2 = jax.ShapeDtypeStruct((2, RW_W, t), BF16)
    spec2t = pl.BlockSpec((2, RW_W, tm), lambda i: (0, 0, i))
    out_shape = [tok(BF16), tok(F32), tok(F32), tok2(BF16), tok2(BF16), chan2, chan2,
                 jax.ShapeDtypeStruct((2, nt * SUBLANES, RW_W), F32)]
    out_specs = [spec1, spec1, spec1, spec2, spec2, spec2t, spec2t,
                 pl.BlockSpec((2, SUBLANES, RW_W), lambda i: (0, i, 0))]
    return pl.pallas_call(
        _rw_prep_body,
        grid=(nt,),
        in_specs=in_specs,
        out_specs=out_specs,
        out_shape=out_shape,
        compiler_params=_cparams(1),
        name="rw_prep",
    )(u, u, u, prm["taps"], prm["w0"], prm["a0"], prm["w2"], prm["a2"], prm["g2"], prm["kks"], prm["ka"],
      prm["rk"], e, et, tri, sel)


RW_PAIRS = RW_HEADS // 2
SCAN_G = 8
SC_T = RW_TM


def _bmm(a, b):
    return lax.dot_general(a, b, (((2,), (1,)), ((0,), (0,))), preferred_element_type=F32)


def _rw_scan_body(rt_ref, kkt_ref, v_ref, kbt_ref, bbt_ref, gam_ref, h0_ref, y_ref, hfin_ref, h_scr, *, reverse):
    c = pl.program_id(1)
    nc = pl.num_programs(1)

    @pl.when(c == 0)
    def _():
        h_scr[...] = h0_ref[...]

    gn = SCAN_G
    n = SC_T
    row = lax.broadcasted_iota(jnp.int32, (n, n), 0)
    col = lax.broadcasted_iota(jnp.int32, (n, n), 1)
    same = (row // CHUNK) == (col // CHUNK)
    earlier = (col > row) if reverse else (col < row)
    strict = same & earlier
    incl = same & (earlier | (col == row))
    eye = (row == col).astype(F32)
    level_masks = []
    b = 1
    while b < CHUNK:
        level_masks.append(((row // (2 * b)) == (col // (2 * b))) & ((row // b) != (col // b)))
        b *= 2
    head_a = (col < RW_HEAD).astype(BF16)
    head_b = (col >= RW_HEAD).astype(BF16)

    lanes = lambda g: slice(g * n, (g + 1) * n)
    rt_p = [rt_ref[0, :, lanes(g)] for g in range(gn)]
    kk_p = [kkt_ref[0, :, lanes(g)] for g in range(gn)]
    v_p = [v_ref[:, lanes(g)] for g in range(gn)]
    kbt = kbt_ref[0].reshape(gn, n, n)
    bbt = bbt_ref[0].reshape(gn, n, n)

    lhs = jnp.stack([jnp.concatenate([rt_p[g] * head_a, kk_p[g] * head_a, rt_p[g] * head_b, kk_p[g] * head_b],
                                     axis=0) for g in range(gn)])
    ak = _bmm(lhs, kbt)
    ab = _bmm(lhs, bbt)
    per_head = lambda x, i: jnp.concatenate([x[:, i * n:(i + 1) * n], x[:, (i + 2) * n:(i + 3) * n]], axis=0)
    ark = jnp.where(incl, per_head(ak, 0), 0.0)
    lk = jnp.where(strict, per_head(ak, 1), 0.0)
    arb = jnp.where(incl, per_head(ab, 0), 0.0)
    lb = jnp.where(strict, per_head(ab, 1), 0.0)

    x = eye - jnp.where(level_masks[0], lb, 0.0)
    for mask in level_masks[1:]:
        xb = x.astype(BF16)
        off = jnp.where(mask, lb, 0.0).astype(BF16)
        x = x - _bmm(xb, _bmm(off, xb).astype(BF16))
    tb = x.astype(BF16)

    kkm = jnp.stack([kk_p[g] * head_a for g in range(gn)] + [kk_p[g] * head_b for g in range(gn)])
    vm = jnp.stack([v_p[g] * head_a for g in range(gn)] + [v_p[g] * head_b for g in range(gn)])
    wk = _bmm(tb, kkm)
    u0 = _bmm(tb, _bmm(lk.astype(BF16), vm).astype(BF16))
    arbb = arb.astype(BF16)
    qw = _bmm(arbb, wk.astype(BF16))
    y0 = _bmm(jnp.concatenate([ark.astype(BF16), -arbb], axis=2),
              jnp.concatenate([vm, u0.astype(BF16)], axis=1))
    pair = lambda t: t[:gn] + t[gn:]
    wk_p = pair(wk).astype(BF16)
    u0_p = pair(u0).astype(BF16)
    qhat = (jnp.stack(rt_p).astype(F32) - pair(qw)).astype(BF16)
    y0_p = pair(y0)
    vu = jnp.concatenate([jnp.stack(v_p), u0_p], axis=1)

    h = h_scr[...]
    chunks = range(n // CHUNK)
    for j in (reversed(chunks) if reverse else chunks):
        in_chunk = ((col // CHUNK) == j).astype(BF16)
        kj = kbt * in_chunk
        bj = bbt * in_chunk
        hloc = jnp.where(same, _bmm(jnp.concatenate([kj, -bj], axis=2), vu), 0.0)
        wb = jnp.where(same, _bmm(bj, wk_p), 0.0).astype(BF16)
        gamma = jnp.stack([jnp.broadcast_to(gam_ref[0, j:j + 1, lanes(g)], (n, n)).T for g in range(gn)])
        hb = h.astype(BF16)
        rows = slice(j * CHUNK, (j + 1) * CHUNK)
        yj = _bmm(qhat[:, rows], hb) + y0_p[:, rows]
        for g in range(gn):
            y_ref[rows, lanes(g)] = yj[g]
        h = gamma * (h - _bmm(wb, hb) + hloc)
    h_scr[...] = h

    @pl.when(c == nc - 1)
    def _():
        hfin_ref[...] = h


def rw_scan(rt, kkt, v, kbar_t, bbar_t, gam, h0, d):
    t = v.shape[0]
    nsc = t // SC_T
    reverse = d == 1
    gw = SCAN_G * LANES
    tix = (lambda c: nsc - 1 - c) if reverse else (lambda c: c)
    tm_spec = pl.BlockSpec((1, SC_T, gw), lambda p, c: (d, tix(c), p))
    cm_spec = pl.BlockSpec((1, gw, SC_T), lambda p, c: (d, p, tix(c)))
    st_spec = pl.BlockSpec((SCAN_G, LANES, LANES), lambda p, c: (p, 0, 0))
    return pl.pallas_call(
        functools.partial(_rw_scan_body, reverse=reverse),
        grid=(RW_PAIRS // SCAN_G, nsc),
        in_specs=[tm_spec, tm_spec, pl.BlockSpec((SC_T, gw), lambda p, c: (tix(c), p)), cm_spec, cm_spec,
                  pl.BlockSpec((1, SUBLANES, gw), lambda p, c: (d, tix(c), p)), st_spec],
        out_specs=[pl.BlockSpec((SC_T, gw), lambda p, c: (tix(c), p)), st_spec],
        out_shape=[jax.ShapeDtypeStruct((t, RW_W), F32),
                   jax.ShapeDtypeStruct((RW_PAIRS, LANES, LANES), F32)],
        scratch_shapes=[pltpu.VMEM((SCAN_G, LANES, LANES), F32)],
        compiler_params=_cparams(2),
        name="rw_scan_rev" if reverse else "rw_scan_fwd",
    )(rt, kkt, v, kbar_t, bbar_t, gam, h0)


def _rw_post_body(yf_ref, yb_ref, bv_ref, g_ref, lg_ref, lb_ref, e_ref, et_ref, o_ref):
    y = yf_ref[...] + yb_ref[...]
    mu = _seg_sum_bcast(y, e_ref, et_ref) * (1.0 / RW_HEAD)
    dlt = y - mu
    var = _seg_sum_bcast(dlt * dlt, e_ref, et_ref) * (1.0 / RW_HEAD)
    yn = dlt * lax.rsqrt(var + GN_EPS)
    o_ref[...] = ((yn * lg_ref[...] + lb_ref[...] + bv_ref[...]) * g_ref[...]).astype(o_ref.dtype)


def rw_post(yf, yb, bv, g, lnx_g, lnx_b):
    t = yf.shape[0]
    tm = 256
    _, _, e, et = _rw_consts()
    tokspec = pl.BlockSpec((tm, RW_W), lambda i: (i, 0))
    row = pl.BlockSpec((1, RW_W), lambda i: (0, 0))
    return pl.pallas_call(
        _rw_post_body,
        grid=(t // tm,),
        in_specs=[tokspec, tokspec, tokspec, tokspec, row, row,
                  pl.BlockSpec((RW_W, LANES), lambda i: (0, 0)), pl.BlockSpec((LANES, RW_W), lambda i: (0, 0))],
        out_specs=tokspec,
        out_shape=jax.ShapeDtypeStruct((t, RW_W), BF16),
        compiler_params=_cparams(1),
        name="rw_post",
    )(yf, yb, bv, g, lnx_g, lnx_b, e, et)


def rwkv_group(u, prm, h0_f, h0_b, with_output):
    v, g, bv, rt, kkt, kbar_t, bbar_t, gam = rw_prep(u, prm)
    outs = []
    finals = []
    for d, h0 in ((0, h0_f), (1, h0_b)):
        y, hfin = rw_scan(rt, kkt, v, kbar_t, bbar_t, gam, h0, d)
        outs.append(y)
        finals.append(hfin)
    if not with_output:
        return None, finals[0], finals[1]
    out = rw_post(outs[0], outs[1], bv, g, prm["lnx_g"], prm["lnx_b"])
    return out, finals[0], finals[1]


NA_RQ = 4


def _na_body(q_ref, k_ref, v_ref, kc_ref, vc_ref, bt_ref, o_ref, *, n_rows):
    rb = pl.program_id(1)
    scale = NA_HEAD ** -0.5
    kws, vws, biases = [], [], []
    for qi in range(NA_RQ):
        r = rb * NA_RQ + qi
        start = jnp.clip(r - NA_KH // 2, 0, n_rows - NA_KH)
        dr0 = start - r + NA_KH - 1
        tok0 = pl.multiple_of(start * GRID_W, GRID_W)
        kws.append(k_ref[pl.ds(tok0, NA_KH * GRID_W), :])
        vws.append(v_ref[pl.ds(tok0, NA_KH * GRID_W), :])
        biases.append(jnp.concatenate([bt_ref[0, dr0 + 2 * m] for m in range(NA_KH // 2)], axis=1))
    q_all = q_ref[...]
    q3 = q_all.reshape(NA_RQ, GRID_W, NA_HEAD)
    s = lax.dot_general(q3, jnp.stack(kws), (((2,), (2,)), ((0,), (0,))),
                        preferred_element_type=F32) * scale + jnp.stack(biases)
    sc = (_dot_nt(q_all, kc_ref[...]) * scale).reshape(NA_RQ, GRID_W, CTX_LEN)
    mx = jnp.maximum(jnp.max(s, axis=-1, keepdims=True), jnp.max(sc, axis=-1, keepdims=True))
    p = jnp.exp(s - mx)
    pc = jnp.exp(sc - mx)
    den = jnp.sum(p, axis=-1, keepdims=True) + jnp.sum(pc, axis=-1, keepdims=True)
    o_ctx = _dot(pc.reshape(NA_RQ * GRID_W, CTX_LEN).astype(BF16), vc_ref[...]).reshape(NA_RQ, GRID_W, NA_HEAD)
    o = (_bmm(p.astype(BF16), jnp.stack(vws)) + o_ctx) / den
    o_ref[...] = o.reshape(NA_RQ * GRID_W, NA_HEAD).astype(o_ref.dtype)


def _na_bias_table(rpb):
    cols = jnp.arange(GRID_W)
    col_start = jnp.clip(cols - NA_KW // 2, 0, GRID_W - NA_KW)
    col_ok = (cols[None, :] >= col_start[:, None]) & (cols[None, :] < col_start[:, None] + NA_KW)
    dc = jnp.clip(cols[None, :] - cols[:, None] + NA_KW - 1, 0, 2 * NA_KW - 2)
    bt = rpb.astype(F32)[:, :, dc]
    bt = jnp.where(col_ok[None, None], bt, NEG_INF)
    return jnp.concatenate([bt[:, :-1], bt[:, 1:]], axis=-1)


def na_latent(att_l, att_c, rpb):
    s_len = att_l.shape[0]
    n_rows = s_len // GRID_W
    qb = NA_W // NA_HEAD
    bt = _na_bias_table(rpb)
    return pl.pallas_call(
        functools.partial(_na_body, n_rows=n_rows),
        grid=(NA_HEADS, n_rows // NA_RQ),
        in_specs=[
            pl.BlockSpec((NA_RQ * GRID_W, NA_HEAD), lambda h, r: (r, h)),
            pl.BlockSpec((s_len, NA_HEAD), lambda h, r: (0, qb + h)),
            pl.BlockSpec((s_len, NA_HEAD), lambda h, r: (0, 2 * qb + h)),
            pl.BlockSpec((CTX_LEN, NA_HEAD), lambda h, r: (0, qb + h)),
            pl.BlockSpec((CTX_LEN, NA_HEAD), lambda h, r: (0, 2 * qb + h)),
            pl.BlockSpec((1, 2 * NA_KH - 2, GRID_W, 2 * GRID_W), lambda h, r: (h, 0, 0, 0)),
        ],
        out_specs=pl.BlockSpec((NA_RQ * GRID_W, NA_HEAD), lambda h, r: (r, h)),
        out_shape=jax.ShapeDtypeStruct((s_len, NA_W), BF16),
        compiler_params=_cparams(2),
        name="na_latent",
    )(att_l, att_l, att_l, att_c, att_c, bt)


def _na_ctx_body(q_ref, k_ref, v_ref, o_ref):
    s = _dot_nt(q_ref[...], k_ref[...]) * (NA_HEAD ** -0.5)
    mx = jnp.max(s, axis=-1, keepdims=True)
    p = jnp.exp(s - mx)
    den = jnp.sum(p, axis=-1, keepdims=True)
    o_ref[...] = (_dot(p.astype(BF16), v_ref[...]) / den).astype(o_ref.dtype)


def na_ctx(att_c):
    qb = NA_W // NA_HEAD
    blk = lambda off: pl.BlockSpec((CTX_LEN, NA_HEAD), lambda h: (0, off + h))
    return pl.pallas_call(
        _na_ctx_body,
        grid=(NA_HEADS,),
        in_specs=[blk(0), blk(qb), blk(2 * qb)],
        out_specs=blk(0),
        out_shape=jax.ShapeDtypeStruct((CTX_LEN, NA_W), BF16),
        compiler_params=_cparams(1),
        name="na_ctx",
    )(att_c, att_c, att_c)


def _rope_body(x_ref, c_ref, s1_ref, s2_ref, o_ref):
    x = x_ref[...].astype(F32)
    quarter = SW_HEAD // 4
    x_up = pltpu.roll(x, LANES - quarter, 1)
    x_dn = pltpu.roll(x, quarter, 1)
    o_ref[...] = (x * c_ref[...] + x_up * s1_ref[...] + x_dn * s2_ref[...]).astype(o_ref.dtype)


def _rope_tables(s_len):
    t = jnp.arange(s_len)
    row = (t // GRID_W).astype(F32)
    col = (t % GRID_W).astype(F32)
    half = SW_HEAD // 2
    inv = ROPE_BASE ** (-jnp.arange(0, half, 2, dtype=F32) / half)
    ang_r = row[:, None] * inv[None, :]
    ang_c = col[:, None] * inv[None, :]
    cos = jnp.concatenate([jnp.cos(ang_r)] * 2 + [jnp.cos(ang_c)] * 2, axis=-1)
    sin_r, sin_c = jnp.sin(ang_r), jnp.sin(ang_c)
    zero = jnp.zeros_like(sin_r)
    s_up = jnp.concatenate([-sin_r, zero, -sin_c, zero], axis=-1)
    s_dn = jnp.concatenate([zero, sin_r, zero, sin_c], axis=-1)
    tile2 = lambda a: jnp.concatenate([a, a], axis=-1)
    return tile2(cos), tile2(s_up), tile2(s_dn)


def rope_qk(att_l):
    s_len = att_l.shape[0]
    tm = min(2048, s_len)
    nblk = (SW_W + SW_KV * SW_HEAD) // LANES
    off = NA_COLS // LANES
    cos, s_up, s_dn = _rope_tables(s_len)
    tab = pl.BlockSpec((tm, LANES), lambda i, j: (i, 0))
    return pl.pallas_call(
        _rope_body,
        grid=(s_len // tm, nblk),
        in_specs=[pl.BlockSpec((tm, LANES), lambda i, j: (i, off + j)), tab, tab, tab],
        out_specs=pl.BlockSpec((tm, LANES), lambda i, j: (i, j)),
        out_shape=jax.ShapeDtypeStruct((s_len, nblk * LANES), BF16),
        compiler_params=_cparams(2),
        name="rope_qk",
    )(att_l, cos, s_up, s_dn)


def _sw_softmax_out(q8, s_parts, v_parts, sink_col):
    mx = sink_col
    for s in s_parts:
        mx = jnp.maximum(mx, jnp.max(s, axis=-1, keepdims=True))
    den = jnp.exp(sink_col - mx)
    o = None
    for s, v in zip(s_parts, v_parts):
        p = jnp.exp(s - mx)
        den = den + jnp.sum(p, axis=-1, keepdims=True)
        pv = _dot(p.astype(BF16), v)
        o = pv if o is None else o + pv
    return o / den


def _sink_column(sink_ref, kh, rows_per_head):
    n = SW_GROUP * rows_per_head
    grp = lax.broadcasted_iota(jnp.int32, (n, 1), 0) // rows_per_head
    col = jnp.zeros((n, 1), F32)
    for g in range(SW_GROUP):
        col = jnp.where(grp == g, sink_ref[kh * SW_GROUP + g], col)
    return col


def _sw_body(sink_ref, q_ref, k_ref, v_ref, kc_ref, vc_ref, o_ref, *, s_len):
    n = pl.program_id(0)
    scale = SW_HEAD ** -0.5
    win = 3 * SW_BLOCK
    start = pl.multiple_of(jnp.clip((n - 1) * SW_BLOCK, 0, s_len - win), SW_BLOCK)
    k2 = k_ref[pl.ds(start, win), :]
    v2 = v_ref[pl.ds(start, win), :]
    qpos = n * SW_BLOCK + lax.broadcasted_iota(jnp.int32, (SW_BLOCK, win), 0)
    kpos = start + lax.broadcasted_iota(jnp.int32, (SW_BLOCK, win), 1)
    ok = jnp.abs(kpos - qpos) <= SW_WIN
    for kh in range(SW_KV):
        lanes = slice(kh * SW_HEAD, (kh + 1) * SW_HEAD)
        q8 = jnp.concatenate(
            [q_ref[:, (kh * SW_GROUP + g) * SW_HEAD:(kh * SW_GROUP + g + 1) * SW_HEAD] for g in range(SW_GROUP)],
            axis=0)
        s_loc = _dot_nt(q8, k2[:, lanes]) * scale
        s_loc = jnp.where(ok[None], s_loc.reshape(SW_GROUP, SW_BLOCK, win), NEG_INF).reshape(
            SW_GROUP * SW_BLOCK, win)
        s_ctx = _dot_nt(q8, kc_ref[:, lanes]) * scale
        o = _sw_softmax_out(q8, [s_loc, s_ctx], [v2[:, lanes], vc_ref[:, lanes]],
                            _sink_column(sink_ref, kh, SW_BLOCK))
        for g in range(SW_GROUP):
            c0 = (kh * SW_GROUP + g) * SW_HEAD
            o_ref[:, c0:c0 + SW_HEAD] = o[g * SW_BLOCK:(g + 1) * SW_BLOCK].astype(o_ref.dtype)


def swa_latent(qk_rot, att_l, att_c, sink):
    s_len = qk_rot.shape[0]
    kcol = SW_W // LANES
    ck = (NA_COLS + SW_W) // LANES
    return pl.pallas_call(
        functools.partial(_sw_body, s_len=s_len),
        grid=(s_len // SW_BLOCK,),
        in_specs=[
            pl.BlockSpec(memory_space=pltpu.SMEM),
            pl.BlockSpec((SW_BLOCK, SW_W), lambda n: (n, 0)),
            pl.BlockSpec((s_len, LANES), lambda n: (0, kcol)),
            pl.BlockSpec((s_len, LANES), lambda n: (0, ck + 1)),
            pl.BlockSpec((CTX_LEN, LANES), lambda n: (0, ck)),
            pl.BlockSpec((CTX_LEN, LANES), lambda n: (0, ck + 1)),
        ],
        out_specs=pl.BlockSpec((SW_BLOCK, SW_W), lambda n: (n, 0)),
        out_shape=jax.ShapeDtypeStruct((s_len, SW_W), BF16),
        compiler_params=_cparams(1),
        name="swa_latent",
    )(sink, qk_rot, qk_rot, att_l, att_c, att_c)


def _sw_ctx_body(sink_ref, q_ref, k_ref, v_ref, o_ref):
    scale = SW_HEAD ** -0.5
    for kh in range(SW_KV):
        lanes = slice(kh * SW_HEAD, (kh + 1) * SW_HEAD)
        q8 = jnp.concatenate(
            [q_ref[:, (kh * SW_GROUP + g) * SW_HEAD:(kh * SW_GROUP + g + 1) * SW_HEAD] for g in range(SW_GROUP)],
            axis=0)
        s = _dot_nt(q8, k_ref[:, lanes]) * scale
        o = _sw_softmax_out(q8, [s], [v_ref[:, lanes]], _sink_column(sink_ref, kh, CTX_LEN))
        for g in range(SW_GROUP):
            c0 = (kh * SW_GROUP + g) * SW_HEAD
            o_ref[:, c0:c0 + SW_HEAD] = o[g * CTX_LEN:(g + 1) * CTX_LEN].astype(o_ref.dtype)


def swa_ctx(att_c, sink):
    qo = NA_COLS // SW_W
    ck = (NA_COLS + SW_W) // LANES
    return pl.pallas_call(
        _sw_ctx_body,
        grid=(1,),
        in_specs=[
            pl.BlockSpec(memory_space=pltpu.SMEM),
            pl.BlockSpec((CTX_LEN, SW_W), lambda n: (0, qo)),
            pl.BlockSpec((CTX_LEN, LANES), lambda n: (0, ck)),
            pl.BlockSpec((CTX_LEN, LANES), lambda n: (0, ck + 1)),
        ],
        out_specs=pl.BlockSpec((CTX_LEN, SW_W), lambda n: (0, 0)),
        out_shape=jax.ShapeDtypeStruct((CTX_LEN, SW_W), BF16),
        compiler_params=_cparams(1),
        name="swa_ctx",
    )(sink, att_c, att_c, att_c)


def _router_body(x_ref, g_ref, sc_ref, sh_ref, wr_ref, rb_ref, h_ref, pick_ref, hp_ref):
    x = x_ref[...]
    tm = x.shape[0]
    ms = jnp.mean(x * x, axis=-1, keepdims=True)
    h = (x * lax.rsqrt(ms + NORM_EPS) * g_ref[...]) * (1.0 + sc_ref[...]) + sh_ref[...]
    h_ref[...] = h.astype(h_ref.dtype)

    hh, hl = _split2(h)
    wh, wl = _split2(wr_ref[...])
    logits = _dot(hh, wh) + _dot(hl, wh) + _dot(hh, wl)
    lane = lax.broadcasted_iota(jnp.int32, (tm, LANES), 1)
    lane_f = lane.astype(F32)
    valid = lane < N_EXPERTS
    neg = -jnp.inf
    big = float(2 * LANES)
    scores = _sigmoid(logits)
    biased = jnp.where(valid, scores + rb_ref[...], neg)
    grp = lane // EXPERTS_PER_GROUP

    def first_argmax(vals):
        m = jnp.max(vals, axis=-1, keepdims=True)
        idx = jnp.min(jnp.where(vals == m, lane_f, big), axis=-1, keepdims=True)
        return m, idx

    gscore = jnp.full((tm, LANES), neg, F32)
    for gi in range(N_GROUPS):
        vg = jnp.where(grp == gi, biased, neg)
        m1, i1 = first_argmax(vg)
        m2 = jnp.max(jnp.where(lane_f == i1, neg, vg), axis=-1, keepdims=True)
        gscore = jnp.where(lane == gi * EXPERTS_PER_GROUP, m1 + m2, gscore)
    keep = jnp.zeros((tm, LANES), jnp.bool_)
    for _ in range(TOPK_GROUPS):
        _, gi = first_argmax(gscore)
        keep = keep | (grp.astype(F32) == jnp.floor(gi * (1.0 / EXPERTS_PER_GROUP)))
        gscore = jnp.where(lane_f == gi, neg, gscore)
    cur = jnp.where(valid, jnp.where(keep, biased, NEG_INF), neg)
    sel = jnp.zeros((tm, LANES), jnp.bool_)
    picks = jnp.zeros((tm, LANES), F32)
    for k in range(TOP_K):
        _, ei = first_argmax(cur)
        hit = lane_f == ei
        sel = sel | hit
        cur = jnp.where(hit, neg, cur)
        picks = jnp.where(lane == k, ei, picks)
        picks = jnp.where(lane == TOP_K + k, jnp.sum(jnp.where(hit, scores, 0.0), axis=-1, keepdims=True), picks)
    w = jnp.where(sel, scores, 0.0)
    norm = ROUTE_SCALE / jnp.sum(w, axis=-1, keepdims=True)
    pick_ref[...] = jnp.where((lane >= TOP_K) & (lane < 2 * TOP_K), picks * norm, picks)

    bits = pltpu.bitcast(h.astype(BF16).astype(F32), jnp.uint32)
    half = D_MODEL // 2
    words = (bits[:, half:] & jnp.uint32(0xFFFF0000)) | (bits[:, :half] >> 16)
    for a in range(TOK_ROWS):
        hp_ref[pl.ds(a, tm, stride=TOK_ROWS), :] = words[:, a * LANES:(a + 1) * LANES]


def router(x, g, sc, sh, w_router_pad, r_bias_pad):
    t = x.shape[0]
    tm = 256
    row = pl.BlockSpec((1, D_MODEL), lambda i: (0, 0))
    return pl.pallas_call(
        _router_body,
        grid=(t // tm,),
        in_specs=[pl.BlockSpec((tm, D_MODEL), lambda i: (i, 0)), row, row, row,
                  pl.BlockSpec((D_MODEL, LANES), lambda i: (0, 0)), pl.BlockSpec((1, LANES), lambda i: (0, 0))],
        out_specs=[pl.BlockSpec((tm, D_MODEL), lambda i: (i, 0)), pl.BlockSpec((tm, LANES), lambda i: (i, 0)),
                   pl.BlockSpec((tm * TOK_ROWS, LANES), lambda i: (i, 0))],
        out_shape=[jax.ShapeDtypeStruct((t, D_MODEL), BF16), jax.ShapeDtypeStruct((t, LANES), F32),
                   jax.ShapeDtypeStruct((t * TOK_ROWS, LANES), jnp.uint32)],
        compiler_params=_cparams(1),
        name="router",
    )(x, g, sc, sh, w_router_pad, r_bias_pad)


MOE_TM = 512
MOE_TM_SMALL = 64
PACK_W = D_MODEL // 2
TOK_ROWS = PACK_W // LANES
TOK_PITCH = TOK_ROWS + 4
MOE_DOWN_COLS = 2 * LANES


def _unpack_bf16_pairs(words):
    lo = pltpu.bitcast(words << 16, F32)
    hi = pltpu.bitcast(words & jnp.uint32(0xFFFF0000), F32)
    return lo, hi


def _pack_bf16_pairs(lo, hi):
    lo_bits = pltpu.bitcast(lo.astype(BF16).astype(F32), jnp.uint32) >> 16
    hi_bits = pltpu.bitcast(hi.astype(BF16).astype(F32), jnp.uint32) & jnp.uint32(0xFFFF0000)
    return hi_bits | lo_bits


def _routed_body(te_ref, nv_ref, src_ref, dst_ref, hp_hbm, wg_ref, wu_ref, wd_ref, yk_hbm, xbuf, ybuf, gsem, ssem,
                 *, tm, n_real_rows):
    s = pl.program_id(0)
    nv = nv_ref[0]

    tile_rows = tm * TOK_ROWS

    def gather_wait(slot):
        pltpu.make_async_copy(hp_hbm.at[pl.ds(0, tile_rows)], xbuf.at[slot, pl.ds(0, tile_rows)],
                              gsem.at[slot]).wait()

    def scatter_wait(slot):
        pltpu.make_async_copy(ybuf.at[slot, pl.ds(0, tile_rows)], yk_hbm.at[pl.ds(0, tile_rows)],
                              ssem.at[slot]).wait()

    def start_gathers(slot, rows):
        for r in rows:
            pltpu.make_async_copy(hp_hbm.at[pl.ds(src_ref[0, 0, r], TOK_ROWS)],
                                  xbuf.at[slot, pl.ds(r * TOK_PITCH, TOK_ROWS)], gsem.at[slot]).start(priority=r % 2)

    @pl.when(s == 0)
    def _():
        ybuf[1, pl.ds(0, tile_rows), :] = jnp.zeros((tile_rows, LANES), jnp.uint32)
        fill = pltpu.make_async_copy(ybuf.at[1, pl.ds(0, tile_rows)],
                                     yk_hbm.at[pl.ds(n_real_rows * TOK_ROWS, tile_rows)], ssem.at[1])
        fill.start()
        fill.wait()
        start_gathers(0, range(tm))

    @pl.when((s >= 1) & (s <= nv))
    def _():
        t = s - 1
        slot = t % 2
        gather_wait(slot)
        hg = jnp.zeros((tm, D_EXPERT), F32)
        hu = jnp.zeros((tm, D_EXPERT), F32)
        rows_per_chunk = tm // TOK_ROWS
        for a in range(TOK_ROWS):
            x_lo, x_hi = _unpack_bf16_pairs(xbuf[slot, pl.ds(a, tm, stride=TOK_PITCH), :])
            x_a = jnp.concatenate([x_lo.astype(BF16), x_hi.astype(BF16)], axis=1)
            lo_rows = slice(a * LANES, (a + 1) * LANES)
            hi_rows = slice(PACK_W + a * LANES, PACK_W + (a + 1) * LANES)
            hg = hg + _dot(x_a, jnp.concatenate([wg_ref[0, 0, lo_rows], wg_ref[0, 0, hi_rows]], axis=0).astype(BF16))
            hu = hu + _dot(x_a, jnp.concatenate([wu_ref[0, 0, lo_rows], wu_ref[0, 0, hi_rows]], axis=0).astype(BF16))
            start_gathers(1 - slot, range(a * rows_per_chunk, (a + 1) * rows_per_chunk))
        act = ((hg * _sigmoid(hg)) * hu).astype(BF16)
        for c in range(PACK_W // MOE_DOWN_COLS):
            c0 = c * MOE_DOWN_COLS
            y_lo = _dot(act, wd_ref[0, 0, :, c0:c0 + MOE_DOWN_COLS].astype(BF16))
            y_hi = _dot(act, wd_ref[0, 0, :, PACK_W + c0:PACK_W + c0 + MOE_DOWN_COLS].astype(BF16))
            words = _pack_bf16_pairs(y_lo, y_hi)
            for j in range(MOE_DOWN_COLS // LANES):
                a = c * (MOE_DOWN_COLS // LANES) + j
                ybuf[slot, pl.ds(a, tm, stride=TOK_PITCH), :] = words[:, j * LANES:(j + 1) * LANES]

        @pl.when(t >= 1)
        def _():
            scatter_wait(1 - slot)

        for r in range(tm):
            pltpu.make_async_copy(ybuf.at[slot, pl.ds(r * TOK_PITCH, TOK_ROWS)],
                                  yk_hbm.at[pl.ds(dst_ref[0, 0, r], TOK_ROWS)], ssem.at[slot]).start(priority=r % 2)

        @pl.when(s == nv)
        def _():
            scatter_wait(slot)
            gather_wait(1 - slot)


def _dispatch_plan(picks, t, tm):
    n_pairs = TOP_K * t
    n_tiles = n_pairs // tm + N_EXPERTS
    e_flat = picks[:, :TOP_K].astype(jnp.int32).reshape(-1)
    pair = jnp.arange(n_pairs, dtype=jnp.int32)
    dst = (pair % TOP_K) * t + pair // TOP_K
    experts = jnp.arange(N_EXPERTS, dtype=jnp.int32)
    counts = jnp.sum((e_flat[:, None] == experts[None, :]).astype(jnp.int32), axis=0)
    fill = (-counts) % tm
    filler_e = jnp.repeat(experts, tm)
    filler_j = jnp.tile(jnp.arange(tm, dtype=jnp.int32), N_EXPERTS)
    filler_key = jnp.where(filler_j < jnp.repeat(fill, tm), 2 * filler_e + 1, 2 * N_EXPERTS)
    dst_bits = (n_pairs + tm - 1).bit_length()
    assert 2 * N_EXPERTS < (1 << (31 - dst_bits))
    keys = jnp.concatenate([2 * e_flat, filler_key]) * (1 << dst_bits) + jnp.concatenate([dst, n_pairs + filler_j])
    dst_sorted = lax.sort(keys) & ((1 << dst_bits) - 1)
    src_sorted = jnp.where(dst_sorted < n_pairs, dst_sorted % t, 0)
    tile_end = jnp.cumsum((counts + fill) // tm)
    n_valid = tile_end[-1].astype(jnp.int32)
    ti = jnp.minimum(jnp.arange(n_tiles, dtype=jnp.int32), n_valid - 1)
    tile_e = jnp.sum((tile_end[None, :] <= ti[:, None]).astype(jnp.int32), axis=1)
    return (tile_e.astype(jnp.int32), n_valid.reshape(1), (src_sorted * TOK_ROWS).reshape(n_tiles, 1, tm),
            (dst_sorted * TOK_ROWS).reshape(n_tiles, 1, tm))


def moe_routed(hp, picks, layer, we_gate, we_up, we_down):
    t = hp.shape[0] // TOK_ROWS
    tm = MOE_TM if TOP_K * t >= N_EXPERTS * MOE_TM else MOE_TM_SMALL
    tile_e, n_valid, src, dst = _dispatch_plan(picks, t, tm)
    n_tiles = tile_e.shape[0]
    last = n_tiles - 1
    cur = lambda s: jnp.minimum(s, last)
    prev = lambda s: jnp.clip(s - 1, 0, last)
    smem = lambda f: pl.BlockSpec((1, 1, tm), lambda s, te, nv: (f(s), 0, 0), memory_space=pltpu.SMEM)
    wspec = lambda shp: pl.BlockSpec((1, 1) + shp, lambda s, te, nv: (layer, te[prev(s)], 0, 0))
    grid_spec = pltpu.PrefetchScalarGridSpec(
        num_scalar_prefetch=2,
        grid=(n_tiles + 1,),
        in_specs=[
            smem(cur), smem(prev),
            pl.BlockSpec(memory_space=pl.ANY),
            wspec((D_MODEL, D_EXPERT)), wspec((D_MODEL, D_EXPERT)), wspec((D_EXPERT, D_MODEL)),
        ],
        out_specs=pl.BlockSpec(memory_space=pl.ANY),
        scratch_shapes=[pltpu.VMEM((2, tm * TOK_PITCH, LANES), jnp.uint32),
                        pltpu.VMEM((2, tm * TOK_PITCH, LANES), jnp.uint32),
                        pltpu.SemaphoreType.DMA((2,)), pltpu.SemaphoreType.DMA((2,))],
    )
    return pl.pallas_call(
        functools.partial(_routed_body, tm=tm, n_real_rows=TOP_K * t),
        grid_spec=grid_spec,
        out_shape=jax.ShapeDtypeStruct(((TOP_K * t + tm) * TOK_ROWS, LANES), jnp.uint32),
        compiler_params=_cparams(1),
        name="moe_routed",
    )(tile_e, n_valid, src, dst, hp, we_gate, we_up, we_down)


COMBINE_TM = 128


def _combine_body(h_ref, pick_ref, *rest):
    yk_refs = rest[:TOP_K]
    wg_ref, wu_ref, wd_ref, x_ref, ga_ref, o_ref = rest[TOP_K:]
    h = h_ref[...]
    hg = _dot(h, wg_ref[...])
    hu = _dot(h, wu_ref[...])
    act = ((hg * _sigmoid(hg)) * hu).astype(BF16)
    picks = pick_ref[...]
    tm = h.shape[0]
    shared = _dot(act, wd_ref[...])
    for a in range(TOK_ROWS):
        acc_lo = shared[:, a * LANES:(a + 1) * LANES]
        acc_hi = shared[:, PACK_W + a * LANES:PACK_W + (a + 1) * LANES]
        for k in range(TOP_K):
            lo, hi = _unpack_bf16_pairs(yk_refs[k][pl.ds(a, tm, stride=TOK_ROWS), :])
            w_k = picks[:, TOP_K + k:TOP_K + k + 1]
            acc_lo = acc_lo + w_k * lo
            acc_hi = acc_hi + w_k * hi
        for base, acc in ((0, acc_lo), (PACK_W, acc_hi)):
            cols = slice(base + a * LANES, base + (a + 1) * LANES)
            o_ref[:, cols] = x_ref[:, cols] + ga_ref[:, cols] * acc


def moe_combine(h, picks, yk, ws_gate, ws_up, ws_down, x, ga):
    t = h.shape[0]
    tm = COMBINE_TM
    nb = t // tm
    tok = lambda cols: pl.BlockSpec((tm, cols), lambda i: (i, 0))
    full = lambda a: pl.BlockSpec(a.shape, lambda i: (0,) * a.ndim)
    slot = lambda k: pl.BlockSpec((tm * TOK_ROWS, LANES), lambda i: (k * nb + i, 0))
    return pl.pallas_call(
        _combine_body,
        grid=(nb,),
        in_specs=[tok(D_MODEL), tok(LANES)] + [slot(k) for k in range(TOP_K)] + [
            full(ws_gate), full(ws_up), full(ws_down), tok(D_MODEL), pl.BlockSpec((1, D_MODEL), lambda i: (0, 0))],
        out_specs=tok(D_MODEL),
        out_shape=jax.ShapeDtypeStruct((t, D_MODEL), F32),
        compiler_params=_cparams(1),
        name="moe_combine",
    )(h, picks, *([yk] * TOP_K), ws_gate, ws_up, ws_down, x, ga)


def _pad_cols(a, n):
    return jnp.pad(a, [(0, 0)] * (a.ndim - 1) + [(0, n - a.shape[-1])])


def _rw_params(i, rw_conv, rw_w0, rw_w2, rw_a0, rw_a2, rw_g2, rw_kk, rw_ka, rw_rk, rw_lnx_g, rw_lnx_b):
    w2 = jnp.zeros((2, LORA_ALL, RW_W), F32)
    a2 = jnp.zeros((2, LORA_ALL, RW_W), F32)
    for d in range(2):
        w2 = w2.at[d, d * LORA_W:(d + 1) * LORA_W].set(rw_w2[i, d])
        a2 = a2.at[d, 2 * LORA_W + d * LORA_A:2 * LORA_W + (d + 1) * LORA_A].set(rw_a2[i, d])
    taps = jnp.pad(rw_conv[i], ((0, SUBLANES - 3), (0, RW_COLS_PAD - RW_COLS)))
    return dict(
        taps=taps, w0=rw_w0[i], a0=rw_a0[i], w2=w2.astype(BF16), a2=a2.astype(BF16), g2=rw_g2[i].astype(BF16),
        kks=rw_kk[i][None], ka=rw_ka[i][None], rk=rw_rk[i].reshape(1, RW_W),
        lnx_g=rw_lnx_g[i][None], lnx_b=rw_lnx_b[i][None])


def kernel(x, c, ctx, c_ctx, w_ada, b_ada, norm1_g, norm2_g, w_in, rw_conv, rw_w0, rw_w2, rw_a0, rw_a2, rw_g2,
           rw_kk, rw_ka, rw_rk, rw_lnx_g, rw_lnx_b, na_rpb, sw_sink, w_out, w_router, router_bias, we_gate,
           we_up, we_down, ws_gate, ws_up, ws_down, final_g):
    assert x.shape[0] == 1 and x.shape[2] == D_MODEL and ctx.shape[1] == CTX_LEN
    xl = x[0]
    xc = ctx[0]
    cond8 = jnp.zeros((SUBLANES, D_MODEL), F32).at[0].set(c[0]).at[1].set(c_ctx)
    mods = ada_mod(cond8, w_ada, b_ada)
    w_in_bf16 = w_in.astype(BF16)

    for i in range(DEPTH):
        ctx_needed = i < DEPTH - 1
        mod_l = [m[None] for m in mods[i, 0].reshape(6, D_MODEL)]
        mod_c = [m[None] for m in mods[i, 1].reshape(6, D_MODEL)]
        w_rw = [(w_in_bf16, i, 0)]
        w_att = [(w_in_bf16[i][:, RW_COLS:][None], 0, 0)]
        w_o_parts = [(w_out, i, 0), (w_out, i, RW_W // NA_W), (w_out, i, (RW_W + NA_W) // SW_W)]
        rw_prm = _rw_params(i, rw_conv, rw_w0, rw_w2, rw_a0, rw_a2, rw_g2, rw_kk, rw_ka, rw_rk, rw_lnx_g,
                            rw_lnx_b)
        n1 = norm1_g[i][None]
        n2 = norm2_g[i][None]

        hl = norm_mod(xl, n1, mod_l[1], mod_l[0])
        hc = norm_mod(xc, n1, mod_c[1], mod_c[0])
        url = matmul([hl], w_rw, RW_COLS_PAD, 768, F32, name="in_proj_rw")
        urc = matmul([hc], w_rw, RW_COLS_PAD, 768, F32, name="in_proj_rw_ctx")
        ual = matmul([hl], w_att, ATT_COLS, 256, BF16, name="in_proj_att")
        uac = matmul([hc], w_att, ATT_COLS, 256, BF16, name="in_proj_att_ctx")

        z = jnp.zeros((RW_PAIRS, LANES, LANES), F32)
        rwc, s_f, s_b = rwkv_group(urc, rw_prm, z, z, ctx_needed)
        rwl, _, _ = rwkv_group(url, rw_prm, s_f, s_b, True)
        nal = na_latent(ual, uac, na_rpb[i])
        swl = swa_latent(rope_qk(ual), ual, uac, sw_sink[i])
        xl = matmul([rwl, nal, swl], w_o_parts, D_MODEL, 512, F32, residual=(xl, mod_l[2]), name="out_proj")

        wr = _pad_cols(w_router[i], LANES)
        rb = _pad_cols(router_bias[i][None], LANES)
        shared = (ws_gate[i].astype(BF16), ws_up[i].astype(BF16), ws_down[i].astype(BF16))

        def moe(xt, mod):
            h2, picks, hp = router(xt, n2, mod[4], mod[3], wr, rb)
            yk = moe_routed(hp, picks, i, we_gate, we_up, we_down)
            return moe_combine(h2, picks, yk, *shared, xt, mod[5])

        xl = moe(xl, mod_l)
        if ctx_needed:
            nac = na_ctx(uac)
            swc = swa_ctx(uac, sw_sink[i])
            xc = matmul([rwc, nac, swc], w_o_parts, D_MODEL, 512, F32, residual=(xc, mod_c[2]), name="out_proj_ctx")
            xc = moe(xc, mod_c)
    return final_norm(xl, final_g[None])[None]
```

```python
import functools

import jax
import jax.numpy as jnp
from jax import lax
from jax.experimental import pallas as pl
from jax.experimental.pallas import tpu as pltpu

F32 = jnp.float32
BF16 = jnp.bfloat16

D_MODEL = 4096
DEPTH = 2
GRID_W = 64
CTX_LEN = 256
NORM_EPS = 1e-6
NEG_INF = -1e30

RW_HEAD = 64
RW_W = D_MODEL // 2
RW_HEADS = RW_W // RW_HEAD
LORA_W = 96
LORA_A = 96
LORA_G = 256
GN_EPS = 64e-5
RW_COLS = 3 * RW_W + LORA_G + 2 * LORA_W + 2 * LORA_A
LORA_OFF = 3 * RW_W + LORA_G
LORA_ALL = 2 * LORA_W + 2 * LORA_A

NA_HEAD = 128
NA_W = D_MODEL // 4
NA_HEADS = NA_W // NA_HEAD
NA_KH = 8
NA_KW = 16
NA_COLS = 3 * NA_W

SW_HEAD = 64
SW_W = D_MODEL - RW_W - NA_W
SW_HEADS = SW_W // SW_HEAD
SW_KV = SW_HEADS // 8
SW_GROUP = SW_HEADS // SW_KV
SW_WIN = 128
SW_BLOCK = 128
ROPE_BASE = 10000.0
SW_COLS = SW_W + 2 * SW_KV * SW_HEAD
ATT_COLS = NA_COLS + SW_COLS

N_EXPERTS = 64
N_GROUPS = 8
EXPERTS_PER_GROUP = N_EXPERTS // N_GROUPS
TOPK_GROUPS = 4
TOP_K = 8
D_EXPERT = 256
ROUTE_SCALE = 2.5

LANES = 128
SUBLANES = 8
RW_COLS_PAD = 6912
CHUNK = 64
VMEM_LIMIT = 56 * 1024 * 1024


def _cparams(n_axes, vmem=VMEM_LIMIT):
    return pltpu.CompilerParams(dimension_semantics=("arbitrary",) * n_axes, vmem_limit_bytes=vmem)


def _dot(a, b):
    return jnp.dot(a, b, preferred_element_type=F32)


def _dot_nt(a, b):
    return lax.dot_general(a, b, (((1,), (1,)), ((), ())), preferred_element_type=F32)


def _split2(x):
    hi = x.astype(BF16)
    lo = (x - hi.astype(F32)).astype(BF16)
    return hi, lo


def _split3(x):
    hi = x.astype(BF16)
    r1 = x - hi.astype(F32)
    mid = r1.astype(BF16)
    lo = (r1 - mid.astype(F32)).astype(BF16)
    return hi, mid, lo


def _sigmoid(x):
    return 1.0 / (1.0 + jnp.exp(-x))


def _ada_body(s_ref, w_ref, b_ref, o_ref):
    c = s_ref[...]
    s = (c * _sigmoid(c)).astype(BF16)
    o_ref[0] = _dot(s, w_ref[0].astype(BF16)) + b_ref[0]


def ada_mod(cond8, w_ada, b_ada):
    bn = 512
    n = 6 * D_MODEL
    return pl.pallas_call(
        _ada_body,
        grid=(DEPTH, n // bn),
        in_specs=[
            pl.BlockSpec((SUBLANES, D_MODEL), lambda l, j: (0, 0)),
            pl.BlockSpec((1, D_MODEL, bn), lambda l, j: (l, 0, j)),
            pl.BlockSpec((1, 1, bn), lambda l, j: (l, 0, j)),
        ],
        out_specs=pl.BlockSpec((1, SUBLANES, bn), lambda l, j: (l, 0, j)),
        out_shape=jax.ShapeDtypeStruct((DEPTH, SUBLANES, n), F32),
        compiler_params=_cparams(2),
        name="ada_mod",
    )(cond8, w_ada, b_ada.reshape(DEPTH, 1, n))


def _norm_mod_body(x_ref, g_ref, sc_ref, sh_ref, o_ref):
    x = x_ref[...]
    ms = jnp.mean(x * x, axis=-1, keepdims=True)
    y = x * lax.rsqrt(ms + NORM_EPS) * g_ref[...]
    o_ref[...] = (y * (1.0 + sc_ref[...]) + sh_ref[...]).astype(o_ref.dtype)


def norm_mod(x, g, sc, sh, out_dtype=BF16):
    t = x.shape[0]
    tm = 256
    row = pl.BlockSpec((1, D_MODEL), lambda i: (0, 0))
    return pl.pallas_call(
        _norm_mod_body,
        grid=(t // tm,),
        in_specs=[pl.BlockSpec((tm, D_MODEL), lambda i: (i, 0)), row, row, row],
        out_specs=pl.BlockSpec((tm, D_MODEL), lambda i: (i, 0)),
        out_shape=jax.ShapeDtypeStruct((t, D_MODEL), out_dtype),
        compiler_params=_cparams(1),
        name="norm_mod",
    )(x, g, sc, sh)


def _final_norm_body(x_ref, g_ref, o_ref):
    x = x_ref[...]
    ms = jnp.mean(x * x, axis=-1, keepdims=True)
    o_ref[...] = x * lax.rsqrt(ms + NORM_EPS) * g_ref[...]


def final_norm(x, g):
    t = x.shape[0]
    tm = 256
    return pl.pallas_call(
        _final_norm_body,
        grid=(t // tm,),
        in_specs=[pl.BlockSpec((tm, D_MODEL), lambda i: (i, 0)), pl.BlockSpec((1, D_MODEL), lambda i: (0, 0))],
        out_specs=pl.BlockSpec((tm, D_MODEL), lambda i: (i, 0)),
        out_shape=jax.ShapeDtypeStruct((t, D_MODEL), F32),
        compiler_params=_cparams(1),
        name="final_norm",
    )(x, g)


def _mm_body(*refs, n_in, residual):
    a_refs = refs[:n_in]
    w_refs = refs[n_in:2 * n_in]
    rest = refs[2 * n_in:]
    acc = _dot(a_refs[0][...], w_refs[0][0].astype(BF16))
    for a_ref, w_ref in zip(a_refs[1:], w_refs[1:]):
        acc = acc + _dot(a_ref[...], w_ref[0].astype(BF16))
    if residual:
        x_ref, ga_ref, o_ref = rest
        o_ref[...] = x_ref[...] + ga_ref[...] * acc
    else:
        (o_ref,) = rest
        o_ref[...] = acc.astype(o_ref.dtype)


def matmul(a_list, w_list, n, bn, out_dtype, residual=None, name="matmul"):
    m = a_list[0].shape[0]
    bm = 1024 if m % 1024 == 0 else 256
    in_specs = [pl.BlockSpec((bm, a.shape[1]), lambda i, j: (i, 0)) for a in a_list]
    for a, (_, layer, row_blk) in zip(a_list, w_list):
        in_specs.append(pl.BlockSpec((1, a.shape[1], bn), lambda i, j, layer=layer, row_blk=row_blk: (layer, row_blk, j)))
    args = list(a_list) + [w for w, _, _ in w_list]
    if residual is not None:
        in_specs += [pl.BlockSpec((bm, bn), lambda i, j: (i, j)), pl.BlockSpec((1, bn), lambda i, j: (0, j))]
        args += list(residual)
    return pl.pallas_call(
        functools.partial(_mm_body, n_in=len(a_list), residual=residual is not None),
        grid=(m // bm, n // bn),
        in_specs=in_specs,
        out_specs=pl.BlockSpec((bm, bn), lambda i, j: (i, j)),
        out_shape=jax.ShapeDtypeStruct((m, n), out_dtype),
        compiler_params=_cparams(2),
        name=name,
    )(*args)


RW_TM = 128


def _seg_sum_bcast(x, e_ref, et_ref):
    hi, lo = _split2(x)
    s = _dot(hi, e_ref[...]) + _dot(lo, e_ref[...])
    shi, slo = _split2(s)
    return _dot(shi, et_ref[...]) + _dot(slo, et_ref[...])


def _rw_prep_body(u_ref, up_ref, un_ref, taps_ref, w0_ref, a0_ref, w2_ref, a2_ref, g2_ref, kks_ref, ka_ref,
                  rk_ref, e_ref, et_ref, tri_ref, sel_ref,
                  v_ref, g_ref, bv_ref, rt_ref, kkt_ref, kbar_ref, bbar_ref, gam_ref):
    i = pl.program_id(0)
    nt = pl.num_programs(0)
    tm = RW_TM
    rows = lax.broadcasted_iota(jnp.int32, (tm, 1), 0)
    has_prev = (i > 0).astype(F32)
    has_next = (i < nt - 1).astype(F32)

    def conv(c0, c1):
        u = u_ref[:, c0:c1]
        prev_row = up_ref[SUBLANES - 1:SUBLANES, c0:c1] * has_prev
        next_row = un_ref[0:1, c0:c1] * has_next
        u_prev = jnp.where(rows == 0, prev_row, pltpu.roll(u, 1, 0))
        u_next = jnp.where(rows == tm - 1, next_row, pltpu.roll(u, tm - 1, 0))
        return (u_prev * taps_ref[0:1, c0:c1] + u * taps_ref[1:2, c0:c1] + u_next * taps_ref[2:3, c0:c1])

    r = conv(0, RW_W)
    k = conv(RW_W, 2 * RW_W)
    v = conv(2 * RW_W, 3 * RW_W)
    gd = conv(3 * RW_W, LORA_OFF)
    lo_in = conv(LORA_OFF, LORA_OFF + LORA_ALL)

    v_ref[...] = v.astype(v_ref.dtype)
    g_ref[...] = _dot(_sigmoid(gd).astype(BF16), g2_ref[...])

    kk = k * kks_ref[...]
    ssq = _seg_sum_bcast(kk * kk, e_ref, et_ref)
    kkn = kk / jnp.maximum(jnp.sqrt(ssq), 1e-12)

    tanh_lo = jnp.tanh(lo_in).astype(BF16)
    raw_lo = lo_in.astype(BF16)
    bonus = jnp.zeros((tm, RW_W), F32)
    for d in range(2):
        w_pre = w0_ref[d:d + 1, :] + _dot(tanh_lo, w2_ref[d])
        z = -w_pre
        softplus = jnp.maximum(z, 0.0) + jnp.log(1.0 + jnp.exp(-jnp.abs(z)))
        w_log = -softplus - 0.5
        lw = -jnp.exp(w_log)
        a = _sigmoid(a0_ref[d:d + 1, :] + _dot(raw_lo, a2_ref[d]))
        kd = k * (1.0 + (a - 1.0) * ka_ref[...])
        b = kkn * a
        bonus = bonus + r * kd * rk_ref[...]
        l1, l2, l3 = _split3(lw)
        tri = tri_ref[d]
        sel = sel_ref[...]
        cum = _dot(tri, l1) + _dot(tri, l2) + _dot(tri, l3)
        tot = _dot(sel, l1) + _dot(sel, l2) + _dot(sel, l3)
        e_inv = jnp.exp(-cum)
        rt_ref[d] = (r * jnp.exp(cum)).astype(BF16)
        kkt_ref[d] = (kkn * jnp.exp(cum - lw)).astype(BF16)
        kbar_ref[d] = (kd * e_inv).T.astype(BF16)
        bbar_ref[d] = (b * e_inv).T.astype(BF16)
        gam_ref[d] = jnp.exp(tot)
    bv_ref[...] = _seg_sum_bcast(bonus, e_ref, et_ref) * v


def _rw_consts():
    tm = RW_TM
    t = jnp.arange(tm)
    same = (t[:, None] // CHUNK) == (t[None, :] // CHUNK)
    tri_f = same & (t[None, :] <= t[:, None])
    tri_b = same & (t[None, :] >= t[:, None])
    tri = jnp.stack([tri_f, tri_b]).astype(BF16)
    sel = ((t[None, :] // CHUNK) == jnp.arange(SUBLANES)[:, None]).astype(BF16)
    lane_head = jnp.arange(RW_W) // RW_HEAD
    e = (lane_head[:, None] == jnp.arange(LANES)[None, :]).astype(BF16)
    return tri, sel, e, e.T


def rw_prep(u, prm):
    t = u.shape[0]
    tm = RW_TM
    nt = t // tm
    hb = tm // SUBLANES
    tri, sel, e, et = _rw_consts()
    full = lambda shp: pl.BlockSpec(shp, lambda i: (0,) * len(shp))
    in_specs = [
        pl.BlockSpec((tm, RW_COLS_PAD), lambda i: (i, 0)),
        pl.BlockSpec((SUBLANES, RW_COLS_PAD), lambda i: (jnp.maximum(i * hb - 1, 0), 0)),
        pl.BlockSpec((SUBLANES, RW_COLS_PAD), lambda i: (jnp.minimum((i + 1) * hb, nt * hb - 1), 0)),
        full((SUBLANES, RW_COLS_PAD)),
        full((2, RW_W)), full((2, RW_W)),
        full((2, LORA_ALL, RW_W)), full((2, LORA_ALL, RW_W)),
        full((LORA_G, RW_W)),
        full((1, RW_W)), full((1, RW_W)), full((1, RW_W)),
        full((RW_W, LANES)), full((LANES, RW_W)),
        full((2, tm, tm)), full((SUBLANES, tm)),
    ]
    tok = lambda dt: jax.ShapeDtypeStruct((t, RW_W), dt)
    tok2 = lambda dt: jax.ShapeDtypeStruct((2, t, RW_W), dt)
    spec1 = pl.BlockSpec((tm, RW_W), lambda i: (i, 0))
    spec2 = pl.BlockSpec((2, tm, RW_W), lambda i: (0, i, 0))
    chan2 = jax.ShapeDtypeStruct((2, RW_W, t), BF16)
    spec2t = pl.BlockSpec((2, RW_W, tm), lambda i: (0, 0, i))
    out_shape = [tok(BF16), tok(F32), tok(F32), tok2(BF16), tok2(BF16), chan2, chan2,
                 jax.ShapeDtypeStruct((2, nt * SUBLANES, RW_W), F32)]
    out_specs = [spec1, spec1, spec1, spec2, spec2, spec2t, spec2t,
                 pl.BlockSpec((2, SUBLANES, RW_W), lambda i: (0, i, 0))]
    return pl.pallas_call(
        _rw_prep_body,
        grid=(nt,),
        in_specs=in_specs,
        out_specs=out_specs,
        out_shape=out_shape,
        compiler_params=_cparams(1),
        name="rw_prep",
    )(u, u, u, prm["taps"], prm["w0"], prm["a0"], prm["w2"], prm["a2"], prm["g2"], prm["kks"], prm["ka"],
      prm["rk"], e, et, tri, sel)


RW_PAIRS = RW_HEADS // 2
SCAN_G = 8
SC_T = RW_TM


def _bmm(a, b):
    return lax.dot_general(a, b, (((2,), (1,)), ((0,), (0,))), preferred_element_type=F32)


def _rw_scan_body(rt_ref, kkt_ref, v_ref, kbt_ref, bbt_ref, gam_ref, h0_ref, y_ref, hfin_ref, h_scr, *, reverse):
    c = pl.program_id(1)
    nc = pl.num_programs(1)

    @pl.when(c == 0)
    def _():
        h_scr[...] = h0_ref[...]

    gn = SCAN_G
    n = SC_T
    row = lax.broadcasted_iota(jnp.int32, (n, n), 0)
    col = lax.broadcasted_iota(jnp.int32, (n, n), 1)
    same = (row // CHUNK) == (col // CHUNK)
    earlier = (col > row) if reverse else (col < row)
    strict = same & earlier
    incl = same & (earlier | (col == row))
    eye = (row == col).astype(F32)
    level_masks = []
    b = 1
    while b < CHUNK:
        level_masks.append(((row // (2 * b)) == (col // (2 * b))) & ((row // b) != (col // b)))
        b *= 2
    head_a = (col < RW_HEAD).astype(BF16)
    head_b = (col >= RW_HEAD).astype(BF16)

    lanes = lambda g: slice(g * n, (g + 1) * n)
    rt_p = [rt_ref[0, :, lanes(g)] for g in range(gn)]
    kk_p = [kkt_ref[0, :, lanes(g)] for g in range(gn)]
    v_p = [v_ref[:, lanes(g)] for g in range(gn)]
    kbt = kbt_ref[0].reshape(gn, n, n)
    bbt = bbt_ref[0].reshape(gn, n, n)

    lhs = jnp.stack([jnp.concatenate([rt_p[g] * head_a, kk_p[g] * head_a, rt_p[g] * head_b, kk_p[g] * head_b],
                                     axis=0) for g in range(gn)])
    ak = _bmm(lhs, kbt)
    ab = _bmm(lhs, bbt)
    per_head = lambda x, i: jnp.concatenate([x[:, i * n:(i + 1) * n], x[:, (i + 2) * n:(i + 3) * n]], axis=0)
    ark = jnp.where(incl, per_head(ak, 0), 0.0)
    lk = jnp.where(strict, per_head(ak, 1), 0.0)
    arb = jnp.where(incl, per_head(ab, 0), 0.0)
    lb = jnp.where(strict, per_head(ab, 1), 0.0)

    x = eye - jnp.where(level_masks[0], lb, 0.0)
    for mask in level_masks[1:]:
        xb = x.astype(BF16)
        off = jnp.where(mask, lb, 0.0).astype(BF16)
        x = x - _bmm(xb, _bmm(off, xb).astype(BF16))
    tb = x.astype(BF16)

    kkm = jnp.stack([kk_p[g] * head_a for g in range(gn)] + [kk_p[g] * head_b for g in range(gn)])
    vm = jnp.stack([v_p[g] * head_a for g in range(gn)] + [v_p[g] * head_b for g in range(gn)])
    wk = _bmm(tb, kkm)
    u0 = _bmm(tb, _bmm(lk.astype(BF16), vm).astype(BF16))
    arbb = arb.astype(BF16)
    qw = _bmm(arbb, wk.astype(BF16))
    y0 = _bmm(jnp.concatenate([ark.astype(BF16), -arbb], axis=2),
              jnp.concatenate([vm, u0.astype(BF16)], axis=1))
    pair = lambda t: t[:gn] + t[gn:]
    wk_p = pair(wk).astype(BF16)
    u0_p = pair(u0).astype(BF16)
    qhat = (jnp.stack(rt_p).astype(F32) - pair(qw)).astype(BF16)
    y0_p = pair(y0)
    vu = jnp.concatenate([jnp.stack(v_p), u0_p], axis=1)

    h = h_scr[...]
    chunks = range(n // CHUNK)
    for j in (reversed(chunks) if reverse else chunks):
        in_chunk = ((col // CHUNK) == j).astype(BF16)
        kj = kbt * in_chunk
        bj = bbt * in_chunk
        hloc = jnp.where(same, _bmm(jnp.concatenate([kj, -bj], axis=2), vu), 0.0)
        wb = jnp.where(same, _bmm(bj, wk_p), 0.0).astype(BF16)
        gamma = jnp.stack([jnp.broadcast_to(gam_ref[0, j:j + 1, lanes(g)], (n, n)).T for g in range(gn)])
        hb = h.astype(BF16)
        rows = slice(j * CHUNK, (j + 1) * CHUNK)
        yj = _bmm(qhat[:, rows], hb) + y0_p[:, rows]
        for g in range(gn):
            y_ref[rows, lanes(g)] = yj[g]
        h = gamma * (h - _bmm(wb, hb) + hloc)
    h_scr[...] = h

    @pl.when(c == nc - 1)
    def _():
        hfin_ref[...] = h


def rw_scan(rt, kkt, v, kbar_t, bbar_t, gam, h0, d):
    t = v.shape[0]
    nsc = t // SC_T
    reverse = d == 1
    gw = SCAN_G * LANES
    tix = (lambda c: nsc - 1 - c) if reverse else (lambda c: c)
    tm_spec = pl.BlockSpec((1, SC_T, gw), lambda p, c: (d, tix(c), p))
    cm_spec = pl.BlockSpec((1, gw, SC_T), lambda p, c: (d, p, tix(c)))
    st_spec = pl.BlockSpec((SCAN_G, LANES, LANES), lambda p, c: (p, 0, 0))
    return pl.pallas_call(
        functools.partial(_rw_scan_body, reverse=reverse),
        grid=(RW_PAIRS // SCAN_G, nsc),
        in_specs=[tm_spec, tm_spec, pl.BlockSpec((SC_T, gw), lambda p, c: (tix(c), p)), cm_spec, cm_spec,
                  pl.BlockSpec((1, SUBLANES, gw), lambda p, c: (d, tix(c), p)), st_spec],
        out_specs=[pl.BlockSpec((SC_T, gw), lambda p, c: (tix(c), p)), st_spec],
        out_shape=[jax.ShapeDtypeStruct((t, RW_W), F32),
                   jax.ShapeDtypeStruct((RW_PAIRS, LANES, LANES), F32)],
        scratch_shapes=[pltpu.VMEM((SCAN_G, LANES, LANES), F32)],
        compiler_params=_cparams(2),
        name="rw_scan_rev" if reverse else "rw_scan_fwd",
    )(rt, kkt, v, kbar_t, bbar_t, gam, h0)


def _rw_post_body(yf_ref, yb_ref, bv_ref, g_ref, lg_ref, lb_ref, e_ref, et_ref, o_ref):
    y = yf_ref[...] + yb_ref[...]
    mu = _seg_sum_bcast(y, e_ref, et_ref) * (1.0 / RW_HEAD)
    dlt = y - mu
    var = _seg_sum_bcast(dlt * dlt, e_ref, et_ref) * (1.0 / RW_HEAD)
    yn = dlt * lax.rsqrt(var + GN_EPS)
    o_ref[...] = ((yn * lg_ref[...] + lb_ref[...] + bv_ref[...]) * g_ref[...]).astype(o_ref.dtype)


def rw_post(yf, yb, bv, g, lnx_g, lnx_b):
    t = yf.shape[0]
    tm = 256
    _, _, e, et = _rw_consts()
    tokspec = pl.BlockSpec((tm, RW_W), lambda i: (i, 0))
    row = pl.BlockSpec((1, RW_W), lambda i: (0, 0))
    return pl.pallas_call(
        _rw_post_body,
        grid=(t // tm,),
        in_specs=[tokspec, tokspec, tokspec, tokspec, row, row,
                  pl.BlockSpec((RW_W, LANES), lambda i: (0, 0)), pl.BlockSpec((LANES, RW_W), lambda i: (0, 0))],
        out_specs=tokspec,
        out_shape=jax.ShapeDtypeStruct((t, RW_W), BF16),
        compiler_params=_cparams(1),
        name="rw_post",
    )(yf, yb, bv, g, lnx_g, lnx_b, e, et)


def rwkv_group(u, prm, h0_f, h0_b, with_output):
    v, g, bv, rt, kkt, kbar_t, bbar_t, gam = rw_prep(u, prm)
    outs = []
    finals = []
    for d, h0 in ((0, h0_f), (1, h0_b)):
        y, hfin = rw_scan(rt, kkt, v, kbar_t, bbar_t, gam, h0, d)
        outs.append(y)
        finals.append(hfin)
    if not with_output:
        return None, finals[0], finals[1]
    out = rw_post(outs[0], outs[1], bv, g, prm["lnx_g"], prm["lnx_b"])
    return out, finals[0], finals[1]


NA_RQ = 8


def _na_body(q_ref, k_ref, v_ref, kc_ref, vc_ref, bt_ref, o_ref, *, n_rows):
    rb = pl.program_id(1)
    scale = NA_HEAD ** -0.5
    kws, vws, biases = [], [], []
    for qi in range(NA_RQ):
        r = rb * NA_RQ + qi
        start = jnp.clip(r - NA_KH // 2, 0, n_rows - NA_KH)
        dr0 = start - r + NA_KH - 1
        tok0 = pl.multiple_of(start * GRID_W, GRID_W)
        kws.append(k_ref[pl.ds(tok0, NA_KH * GRID_W), :])
        vws.append(v_ref[pl.ds(tok0, NA_KH * GRID_W), :])
        biases.append(jnp.concatenate([bt_ref[0, dr0 + 2 * m] for m in range(NA_KH // 2)], axis=1))
    q_all = q_ref[...]
    q3 = q_all.reshape(NA_RQ, GRID_W, NA_HEAD)
    s = lax.dot_general(q3, jnp.stack(kws), (((2,), (2,)), ((0,), (0,))),
                        preferred_element_type=F32) * scale + jnp.stack(biases)
    sc = (_dot_nt(q_all, kc_ref[...]) * scale).reshape(NA_RQ, GRID_W, CTX_LEN)
    mx = jnp.maximum(jnp.max(s, axis=-1, keepdims=True), jnp.max(sc, axis=-1, keepdims=True))
    p = jnp.exp(s - mx)
    pc = jnp.exp(sc - mx)
    den = jnp.sum(p, axis=-1, keepdims=True) + jnp.sum(pc, axis=-1, keepdims=True)
    o_ctx = _dot(pc.reshape(NA_RQ * GRID_W, CTX_LEN).astype(BF16), vc_ref[...]).reshape(NA_RQ, GRID_W, NA_HEAD)
    o = (_bmm(p.astype(BF16), jnp.stack(vws)) + o_ctx) / den
    o_ref[...] = o.reshape(NA_RQ * GRID_W, NA_HEAD).astype(o_ref.dtype)


def _na_bias_table(rpb):
    cols = jnp.arange(GRID_W)
    col_start = jnp.clip(cols - NA_KW // 2, 0, GRID_W - NA_KW)
    col_ok = (cols[None, :] >= col_start[:, None]) & (cols[None, :] < col_start[:, None] + NA_KW)
    dc = jnp.clip(cols[None, :] - cols[:, None] + NA_KW - 1, 0, 2 * NA_KW - 2)
    bt = rpb.astype(F32)[:, :, dc]
    bt = jnp.where(col_ok[None, None], bt, NEG_INF)
    return jnp.concatenate([bt[:, :-1], bt[:, 1:]], axis=-1)


def na_latent(att_l, att_c, rpb):
    s_len = att_l.shape[0]
    n_rows = s_len // GRID_W
    qb = NA_W // NA_HEAD
    bt = _na_bias_table(rpb)
    return pl.pallas_call(
        functools.partial(_na_body, n_rows=n_rows),
        grid=(NA_HEADS, n_rows // NA_RQ),
        in_specs=[
            pl.BlockSpec((NA_RQ * GRID_W, NA_HEAD), lambda h, r: (r, h)),
            pl.BlockSpec((s_len, NA_HEAD), lambda h, r: (0, qb + h)),
            pl.BlockSpec((s_len, NA_HEAD), lambda h, r: (0, 2 * qb + h)),
            pl.BlockSpec((CTX_LEN, NA_HEAD), lambda h, r: (0, qb + h)),
            pl.BlockSpec((CTX_LEN, NA_HEAD), lambda h, r: (0, 2 * qb + h)),
            pl.BlockSpec((1, 2 * NA_KH - 2, GRID_W, 2 * GRID_W), lambda h, r: (h, 0, 0, 0)),
        ],
        out_specs=pl.BlockSpec((NA_RQ * GRID_W, NA_HEAD), lambda h, r: (r, h)),
        out_shape=jax.ShapeDtypeStruct((s_len, NA_W), BF16),
        compiler_params=_cparams(2),
        name="na_latent",
    )(att_l, att_l, att_l, att_c, att_c, bt)


def _na_ctx_body(q_ref, k_ref, v_ref, o_ref):
    s = _dot_nt(q_ref[...], k_ref[...]) * (NA_HEAD ** -0.5)
    mx = jnp.max(s, axis=-1, keepdims=True)
    p = jnp.exp(s - mx)
    den = jnp.sum(p, axis=-1, keepdims=True)
    o_ref[...] = (_dot(p.astype(BF16), v_ref[...]) / den).astype(o_ref.dtype)


def na_ctx(att_c):
    qb = NA_W // NA_HEAD
    blk = lambda off: pl.BlockSpec((CTX_LEN, NA_HEAD), lambda h: (0, off + h))
    return pl.pallas_call(
        _na_ctx_body,
        grid=(NA_HEADS,),
        in_specs=[blk(0), blk(qb), blk(2 * qb)],
        out_specs=blk(0),
        out_shape=jax.ShapeDtypeStruct((CTX_LEN, NA_W), BF16),
        compiler_params=_cparams(1),
        name="na_ctx",
    )(att_c, att_c, att_c)


def _rope_body(x_ref, c_ref, s1_ref, s2_ref, o_ref):
    x = x_ref[...].astype(F32)
    quarter = SW_HEAD // 4
    x_up = pltpu.roll(x, LANES - quarter, 1)
    x_dn = pltpu.roll(x, quarter, 1)
    o_ref[...] = (x * c_ref[...] + x_up * s1_ref[...] + x_dn * s2_ref[...]).astype(o_ref.dtype)


def _rope_tables(s_len):
    t = jnp.arange(s_len)
    row = (t // GRID_W).astype(F32)
    col = (t % GRID_W).astype(F32)
    half = SW_HEAD // 2
    inv = ROPE_BASE ** (-jnp.arange(0, half, 2, dtype=F32) / half)
    ang_r = row[:, None] * inv[None, :]
    ang_c = col[:, None] * inv[None, :]
    cos = jnp.concatenate([jnp.cos(ang_r)] * 2 + [jnp.cos(ang_c)] * 2, axis=-1)
    sin_r, sin_c = jnp.sin(ang_r), jnp.sin(ang_c)
    zero = jnp.zeros_like(sin_r)
    s_up = jnp.concatenate([-sin_r, zero, -sin_c, zero], axis=-1)
    s_dn = jnp.concatenate([zero, sin_r, zero, sin_c], axis=-1)
    tile2 = lambda a: jnp.concatenate([a, a], axis=-1)
    return tile2(cos), tile2(s_up), tile2(s_dn)


def rope_qk(att_l):
    s_len = att_l.shape[0]
    tm = min(2048, s_len)
    nblk = (SW_W + SW_KV * SW_HEAD) // LANES
    off = NA_COLS // LANES
    cos, s_up, s_dn = _rope_tables(s_len)
    tab = pl.BlockSpec((tm, LANES), lambda i, j: (i, 0))
    return pl.pallas_call(
        _rope_body,
        grid=(s_len // tm, nblk),
        in_specs=[pl.BlockSpec((tm, LANES), lambda i, j: (i, off + j)), tab, tab, tab],
        out_specs=pl.BlockSpec((tm, LANES), lambda i, j: (i, j)),
        out_shape=jax.ShapeDtypeStruct((s_len, nblk * LANES), BF16),
        compiler_params=_cparams(2),
        name="rope_qk",
    )(att_l, cos, s_up, s_dn)


def _sw_softmax_out(q8, s_parts, v_parts, sink_col):
    mx = sink_col
    for s in s_parts:
        mx = jnp.maximum(mx, jnp.max(s, axis=-1, keepdims=True))
    den = jnp.exp(sink_col - mx)
    o = None
    for s, v in zip(s_parts, v_parts):
        p = jnp.exp(s - mx)
        den = den + jnp.sum(p, axis=-1, keepdims=True)
        pv = _dot(p.astype(BF16), v)
        o = pv if o is None else o + pv
    return o / den


def _sink_column(sink_ref, kh, rows_per_head):
    n = SW_GROUP * rows_per_head
    grp = lax.broadcasted_iota(jnp.int32, (n, 1), 0) // rows_per_head
    col = jnp.zeros((n, 1), F32)
    for g in range(SW_GROUP):
        col = jnp.where(grp == g, sink_ref[kh * SW_GROUP + g], col)
    return col


def _sw_body(sink_ref, q_ref, k_ref, v_ref, kc_ref, vc_ref, o_ref, *, s_len):
    n = pl.program_id(0)
    scale = SW_HEAD ** -0.5
    win = 3 * SW_BLOCK
    start = pl.multiple_of(jnp.clip((n - 1) * SW_BLOCK, 0, s_len - win), SW_BLOCK)
    k2 = k_ref[pl.ds(start, win), :]
    v2 = v_ref[pl.ds(start, win), :]
    qpos = n * SW_BLOCK + lax.broadcasted_iota(jnp.int32, (SW_BLOCK, win), 0)
    kpos = start + lax.broadcasted_iota(jnp.int32, (SW_BLOCK, win), 1)
    ok = jnp.abs(kpos - qpos) <= SW_WIN
    for kh in range(SW_KV):
        lanes = slice(kh * SW_HEAD, (kh + 1) * SW_HEAD)
        q8 = jnp.concatenate(
            [q_ref[:, (kh * SW_GROUP + g) * SW_HEAD:(kh * SW_GROUP + g + 1) * SW_HEAD] for g in range(SW_GROUP)],
            axis=0)
        s_loc = _dot_nt(q8, k2[:, lanes]) * scale
        s_loc = jnp.where(ok[None], s_loc.reshape(SW_GROUP, SW_BLOCK, win), NEG_INF).reshape(
            SW_GROUP * SW_BLOCK, win)
        s_ctx = _dot_nt(q8, kc_ref[:, lanes]) * scale
        o = _sw_softmax_out(q8, [s_loc, s_ctx], [v2[:, lanes], vc_ref[:, lanes]],
                            _sink_column(sink_ref, kh, SW_BLOCK))
        for g in range(SW_GROUP):
            c0 = (kh * SW_GROUP + g) * SW_HEAD
            o_ref[:, c0:c0 + SW_HEAD] = o[g * SW_BLOCK:(g + 1) * SW_BLOCK].astype(o_ref.dtype)


def swa_latent(qk_rot, att_l, att_c, sink):
    s_len = qk_rot.shape[0]
    kcol = SW_W // LANES
    ck = (NA_COLS + SW_W) // LANES
    return pl.pallas_call(
        functools.partial(_sw_body, s_len=s_len),
        grid=(s_len // SW_BLOCK,),
        in_specs=[
            pl.BlockSpec(memory_space=pltpu.SMEM),
            pl.BlockSpec((SW_BLOCK, SW_W), lambda n: (n, 0)),
            pl.BlockSpec((s_len, LANES), lambda n: (0, kcol)),
            pl.BlockSpec((s_len, LANES), lambda n: (0, ck + 1)),
            pl.BlockSpec((CTX_LEN, LANES), lambda n: (0, ck)),
            pl.BlockSpec((CTX_LEN, LANES), lambda n: (0, ck + 1)),
        ],
        out_specs=pl.BlockSpec((SW_BLOCK, SW_W), lambda n: (n, 0)),
        out_shape=jax.ShapeDtypeStruct((s_len, SW_W), BF16),
        compiler_params=_cparams(1),
        name="swa_latent",
    )(sink, qk_rot, qk_rot, att_l, att_c, att_c)


def _sw_ctx_body(sink_ref, q_ref, k_ref, v_ref, o_ref):
    scale = SW_HEAD ** -0.5
    for kh in range(SW_KV):
        lanes = slice(kh * SW_HEAD, (kh + 1) * SW_HEAD)
        q8 = jnp.concatenate(
            [q_ref[:, (kh * SW_GROUP + g) * SW_HEAD:(kh * SW_GROUP + g + 1) * SW_HEAD] for g in range(SW_GROUP)],
            axis=0)
        s = _dot_nt(q8, k_ref[:, lanes]) * scale
        o = _sw_softmax_out(q8, [s], [v_ref[:, lanes]], _sink_column(sink_ref, kh, CTX_LEN))
        for g in range(SW_GROUP):
            c0 = (kh * SW_GROUP + g) * SW_HEAD
            o_ref[:, c0:c0 + SW_HEAD] = o[g * CTX_LEN:(g + 1) * CTX_LEN].astype(o_ref.dtype)


def swa_ctx(att_c, sink):
    qo = NA_COLS // SW_W
    ck = (NA_COLS + SW_W) // LANES
    return pl.pallas_call(
        _sw_ctx_body,
        grid=(1,),
        in_specs=[
            pl.BlockSpec(memory_space=pltpu.SMEM),
            pl.BlockSpec((CTX_LEN, SW_W), lambda n: (0, qo)),
            pl.BlockSpec((CTX_LEN, LANES), lambda n: (0, ck)),
            pl.BlockSpec((CTX_LEN, LANES), lambda n: (0, ck + 1)),
        ],
        out_specs=pl.BlockSpec((CTX_LEN, SW_W), lambda n: (0, 0)),
        out_shape=jax.ShapeDtypeStruct((CTX_LEN, SW_W), BF16),
        compiler_params=_cparams(1),
        name="swa_ctx",
    )(sink, att_c, att_c, att_c)


def _router_body(x_ref, g_ref, sc_ref, sh_ref, wr_ref, rb_ref, h_ref, pick_ref, hp_ref):
    x = x_ref[...]
    tm = x.shape[0]
    ms = jnp.mean(x * x, axis=-1, keepdims=True)
    h = (x * lax.rsqrt(ms + NORM_EPS) * g_ref[...]) * (1.0 + sc_ref[...]) + sh_ref[...]
    h_ref[...] = h.astype(h_ref.dtype)

    hh, hl = _split2(h)
    wh, wl = _split2(wr_ref[...])
    logits = _dot(hh, wh) + _dot(hl, wh) + _dot(hh, wl)
    lane = lax.broadcasted_iota(jnp.int32, (tm, LANES), 1)
    lane_f = lane.astype(F32)
    valid = lane < N_EXPERTS
    neg = -jnp.inf
    big = float(2 * LANES)
    scores = _sigmoid(logits)
    biased = jnp.where(valid, scores + rb_ref[...], neg)
    grp = lane // EXPERTS_PER_GROUP

    def first_argmax(vals):
        m = jnp.max(vals, axis=-1, keepdims=True)
        idx = jnp.min(jnp.where(vals == m, lane_f, big), axis=-1, keepdims=True)
        return m, idx

    gscore = jnp.full((tm, LANES), neg, F32)
    for gi in range(N_GROUPS):
        vg = jnp.where(grp == gi, biased, neg)
        m1, i1 = first_argmax(vg)
        m2 = jnp.max(jnp.where(lane_f == i1, neg, vg), axis=-1, keepdims=True)
        gscore = jnp.where(lane == gi * EXPERTS_PER_GROUP, m1 + m2, gscore)
    keep = jnp.zeros((tm, LANES), jnp.bool_)
    for _ in range(TOPK_GROUPS):
        _, gi = first_argmax(gscore)
        keep = keep | (grp.astype(F32) == jnp.floor(gi * (1.0 / EXPERTS_PER_GROUP)))
        gscore = jnp.where(lane_f == gi, neg, gscore)
    cur = jnp.where(valid, jnp.where(keep, biased, NEG_INF), neg)
    sel = jnp.zeros((tm, LANES), jnp.bool_)
    picks = jnp.zeros((tm, LANES), F32)
    for k in range(TOP_K):
        _, ei = first_argmax(cur)
        hit = lane_f == ei
        sel = sel | hit
        cur = jnp.where(hit, neg, cur)
        picks = jnp.where(lane == k, ei, picks)
        picks = jnp.where(lane == TOP_K + k, jnp.sum(jnp.where(hit, scores, 0.0), axis=-1, keepdims=True), picks)
    w = jnp.where(sel, scores, 0.0)
    norm = ROUTE_SCALE / jnp.sum(w, axis=-1, keepdims=True)
    pick_ref[...] = jnp.where((lane >= TOP_K) & (lane < 2 * TOP_K), picks * norm, picks)

    bits = pltpu.bitcast(h.astype(BF16).astype(F32), jnp.uint32)
    half = D_MODEL // 2
    words = (bits[:, half:] & jnp.uint32(0xFFFF0000)) | (bits[:, :half] >> 16)
    for a in range(TOK_ROWS):
        hp_ref[pl.ds(a, tm, stride=TOK_ROWS), :] = words[:, a * LANES:(a + 1) * LANES]


def router(x, g, sc, sh, w_router_pad, r_bias_pad):
    t = x.shape[0]
    tm = 256
    row = pl.BlockSpec((1, D_MODEL), lambda i: (0, 0))
    return pl.pallas_call(
        _router_body,
        grid=(t // tm,),
        in_specs=[pl.BlockSpec((tm, D_MODEL), lambda i: (i, 0)), row, row, row,
                  pl.BlockSpec((D_MODEL, LANES), lambda i: (0, 0)), pl.BlockSpec((1, LANES), lambda i: (0, 0))],
        out_specs=[pl.BlockSpec((tm, D_MODEL), lambda i: (i, 0)), pl.BlockSpec((tm, LANES), lambda i: (i, 0)),
                   pl.BlockSpec((tm * TOK_ROWS, LANES), lambda i: (i, 0))],
        out_shape=[jax.ShapeDtypeStruct((t, D_MODEL), BF16), jax.ShapeDtypeStruct((t, LANES), F32),
                   jax.ShapeDtypeStruct((t * TOK_ROWS, LANES), jnp.uint32)],
        compiler_params=_cparams(1),
        name="router",
    )(x, g, sc, sh, w_router_pad, r_bias_pad)


MOE_TM = 512
MOE_TM_SMALL = 64
PACK_W = D_MODEL // 2
TOK_ROWS = PACK_W // LANES
TOK_PITCH = TOK_ROWS + 4
MOE_DOWN_COLS = 2 * LANES


def _unpack_bf16_pairs(words):
    lo = pltpu.bitcast(words << 16, F32)
    hi = pltpu.bitcast(words & jnp.uint32(0xFFFF0000), F32)
    return lo, hi


def _pack_bf16_pairs(lo, hi):
    lo_bits = pltpu.bitcast(lo.astype(BF16).astype(F32), jnp.uint32) >> 16
    hi_bits = pltpu.bitcast(hi.astype(BF16).astype(F32), jnp.uint32) & jnp.uint32(0xFFFF0000)
    return hi_bits | lo_bits


def _routed_body(te_ref, nv_ref, src_ref, dst_ref, hp_hbm, wg_ref, wu_ref, wd_ref, yk_hbm, xbuf, ybuf, gsem, ssem,
                 *, tm, n_real_rows):
    s = pl.program_id(0)
    nv = nv_ref[0]

    tile_rows = tm * TOK_ROWS

    def gather_wait(slot):
        pltpu.make_async_copy(hp_hbm.at[pl.ds(0, tile_rows)], xbuf.at[slot, pl.ds(0, tile_rows)],
                              gsem.at[slot]).wait()

    def scatter_wait(slot):
        pltpu.make_async_copy(ybuf.at[slot, pl.ds(0, tile_rows)], yk_hbm.at[pl.ds(0, tile_rows)],
                              ssem.at[slot]).wait()

    def start_gathers(slot, rows):
        for r in rows:
            pltpu.make_async_copy(hp_hbm.at[pl.ds(src_ref[0, 0, r], TOK_ROWS)],
                                  xbuf.at[slot, pl.ds(r * TOK_PITCH, TOK_ROWS)], gsem.at[slot]).start(priority=r % 2)

    @pl.when(s == 0)
    def _():
        ybuf[1, pl.ds(0, tile_rows), :] = jnp.zeros((tile_rows, LANES), jnp.uint32)
        fill = pltpu.make_async_copy(ybuf.at[1, pl.ds(0, tile_rows)],
                                     yk_hbm.at[pl.ds(n_real_rows * TOK_ROWS, tile_rows)], ssem.at[1])
        fill.start()
        fill.wait()
        start_gathers(0, range(tm))

    @pl.when((s >= 1) & (s <= nv))
    def _():
        t = s - 1
        slot = t % 2
        gather_wait(slot)
        hg = jnp.zeros((tm, D_EXPERT), F32)
        hu = jnp.zeros((tm, D_EXPERT), F32)
        rows_per_chunk = tm // TOK_ROWS
        for a in range(TOK_ROWS):
            x_lo, x_hi = _unpack_bf16_pairs(xbuf[slot, pl.ds(a, tm, stride=TOK_PITCH), :])
            x_a = jnp.concatenate([x_lo.astype(BF16), x_hi.astype(BF16)], axis=1)
            lo_rows = slice(a * LANES, (a + 1) * LANES)
            hi_rows = slice(PACK_W + a * LANES, PACK_W + (a + 1) * LANES)
            hg = hg + _dot(x_a, jnp.concatenate([wg_ref[0, 0, lo_rows], wg_ref[0, 0, hi_rows]], axis=0).astype(BF16))
            hu = hu + _dot(x_a, jnp.concatenate([wu_ref[0, 0, lo_rows], wu_ref[0, 0, hi_rows]], axis=0).astype(BF16))
            start_gathers(1 - slot, range(a * rows_per_chunk, (a + 1) * rows_per_chunk))
        act = ((hg * _sigmoid(hg)) * hu).astype(BF16)
        for c in range(PACK_W // MOE_DOWN_COLS):
            c0 = c * MOE_DOWN_COLS
            y_lo = _dot(act, wd_ref[0, 0, :, c0:c0 + MOE_DOWN_COLS].astype(BF16))
            y_hi = _dot(act, wd_ref[0, 0, :, PACK_W + c0:PACK_W + c0 + MOE_DOWN_COLS].astype(BF16))
            words = _pack_bf16_pairs(y_lo, y_hi)
            for j in range(MOE_DOWN_COLS // LANES):
                a = c * (MOE_DOWN_COLS // LANES) + j
                ybuf[slot, pl.ds(a, tm, stride=TOK_PITCH), :] = words[:, j * LANES:(j + 1) * LANES]

        @pl.when(t >= 1)
        def _():
            scatter_wait(1 - slot)

        for r in range(tm):
            pltpu.make_async_copy(ybuf.at[slot, pl.ds(r * TOK_PITCH, TOK_ROWS)],
                                  yk_hbm.at[pl.ds(dst_ref[0, 0, r], TOK_ROWS)], ssem.at[slot]).start(priority=r % 2)

        @pl.when(s == nv)
        def _():
            scatter_wait(slot)
            gather_wait(1 - slot)


def _dispatch_plan(picks, t, tm):
    n_pairs = TOP_K * t
    n_tiles = n_pairs // tm + N_EXPERTS
    e_flat = picks[:, :TOP_K].astype(jnp.int32).reshape(-1)
    pair = jnp.arange(n_pairs, dtype=jnp.int32)
    dst = (pair % TOP_K) * t + pair // TOP_K
    experts = jnp.arange(N_EXPERTS, dtype=jnp.int32)
    counts = jnp.sum((e_flat[:, None] == experts[None, :]).astype(jnp.int32), axis=0)
    fill = (-counts) % tm
    filler_e = jnp.repeat(experts, tm)
    filler_j = jnp.tile(jnp.arange(tm, dtype=jnp.int32), N_EXPERTS)
    filler_key = jnp.where(filler_j < jnp.repeat(fill, tm), 2 * filler_e + 1, 2 * N_EXPERTS)
    dst_bits = (n_pairs + tm - 1).bit_length()
    assert 2 * N_EXPERTS < (1 << (31 - dst_bits))
    keys = jnp.concatenate([2 * e_flat, filler_key]) * (1 << dst_bits) + jnp.concatenate([dst, n_pairs + filler_j])
    dst_sorted = lax.sort(keys) & ((1 << dst_bits) - 1)
    src_sorted = jnp.where(dst_sorted < n_pairs, dst_sorted % t, 0)
    tile_end = jnp.cumsum((counts + fill) // tm)
    n_valid = tile_end[-1].astype(jnp.int32)
    ti = jnp.minimum(jnp.arange(n_tiles, dtype=jnp.int32), n_valid - 1)
    tile_e = jnp.sum((tile_end[None, :] <= ti[:, None]).astype(jnp.int32), axis=1)
    return (tile_e.astype(jnp.int32), n_valid.reshape(1), (src_sorted * TOK_ROWS).reshape(n_tiles, 1, tm),
            (dst_sorted * TOK_ROWS).reshape(n_tiles, 1, tm))


def moe_routed(hp, picks, layer, we_gate, we_up, we_down):
    t = hp.shape[0] // TOK_ROWS
    tm = MOE_TM if TOP_K * t >= N_EXPERTS * MOE_TM else MOE_TM_SMALL
    tile_e, n_valid, src, dst = _dispatch_plan(picks, t, tm)
    n_tiles = tile_e.shape[0]
    last = n_tiles - 1
    cur = lambda s: jnp.minimum(s, last)
    prev = lambda s: jnp.clip(s - 1, 0, last)
    smem = lambda f: pl.BlockSpec((1, 1, tm), lambda s, te, nv: (f(s), 0, 0), memory_space=pltpu.SMEM)
    wspec = lambda shp: pl.BlockSpec((1, 1) + shp, lambda s, te, nv: (layer, te[prev(s)], 0, 0))
    grid_spec = pltpu.PrefetchScalarGridSpec(
        num_scalar_prefetch=2,
        grid=(n_tiles + 1,),
        in_specs=[
            smem(cur), smem(prev),
            pl.BlockSpec(memory_space=pl.ANY),
            wspec((D_MODEL, D_EXPERT)), wspec((D_MODEL, D_EXPERT)), wspec((D_EXPERT, D_MODEL)),
        ],
        out_specs=pl.BlockSpec(memory_space=pl.ANY),
        scratch_shapes=[pltpu.VMEM((2, tm * TOK_PITCH, LANES), jnp.uint32),
                        pltpu.VMEM((2, tm * TOK_PITCH, LANES), jnp.uint32),
                        pltpu.SemaphoreType.DMA((2,)), pltpu.SemaphoreType.DMA((2,))],
    )
    return pl.pallas_call(
        functools.partial(_routed_body, tm=tm, n_real_rows=TOP_K * t),
        grid_spec=grid_spec,
        out_shape=jax.ShapeDtypeStruct(((TOP_K * t + tm) * TOK_ROWS, LANES), jnp.uint32),
        compiler_params=_cparams(1),
        name="moe_routed",
    )(tile_e, n_valid, src, dst, hp, we_gate, we_up, we_down)


COMBINE_TM = 128


def _combine_body(h_ref, pick_ref, *rest):
    yk_refs = rest[:TOP_K]
    wg_ref, wu_ref, wd_ref, x_ref, ga_ref, o_ref = rest[TOP_K:]
    h = h_ref[...]
    hg = _dot(h, wg_ref[...])
    hu = _dot(h, wu_ref[...])
    act = ((hg * _sigmoid(hg)) * hu).astype(BF16)
    picks = pick_ref[...]
    tm = h.shape[0]
    shared = _dot(act, wd_ref[...])
    for a in range(TOK_ROWS):
        acc_lo = shared[:, a * LANES:(a + 1) * LANES]
        acc_hi = shared[:, PACK_W + a * LANES:PACK_W + (a + 1) * LANES]
        for k in range(TOP_K):
            lo, hi = _unpack_bf16_pairs(yk_refs[k][pl.ds(a, tm, stride=TOK_ROWS), :])
            w_k = picks[:, TOP_K + k:TOP_K + k + 1]
            acc_lo = acc_lo + w_k * lo
            acc_hi = acc_hi + w_k * hi
        for base, acc in ((0, acc_lo), (PACK_W, acc_hi)):
            cols = slice(base + a * LANES, base + (a + 1) * LANES)
            o_ref[:, cols] = x_ref[:, cols] + ga_ref[:, cols] * acc


def moe_combine(h, picks, yk, ws_gate, ws_up, ws_down, x, ga):
    t = h.shape[0]
    tm = COMBINE_TM
    nb = t // tm
    tok = lambda cols: pl.BlockSpec((tm, cols), lambda i: (i, 0))
    full = lambda a: pl.BlockSpec(a.shape, lambda i: (0,) * a.ndim)
    slot = lambda k: pl.BlockSpec((tm * TOK_ROWS, LANES), lambda i: (k * nb + i, 0))
    return pl.pallas_call(
        _combine_body,
        grid=(nb,),
        in_specs=[tok(D_MODEL), tok(LANES)] + [slot(k) for k in range(TOP_K)] + [
            full(ws_gate), full(ws_up), full(ws_down), tok(D_MODEL), pl.BlockSpec((1, D_MODEL), lambda i: (0, 0))],
        out_specs=tok(D_MODEL),
        out_shape=jax.ShapeDtypeStruct((t, D_MODEL), F32),
        compiler_params=_cparams(1),
        name="moe_combine",
    )(h, picks, *([yk] * TOP_K), ws_gate, ws_up, ws_down, x, ga)


def _pad_cols(a, n):
    return jnp.pad(a, [(0, 0)] * (a.ndim - 1) + [(0, n - a.shape[-1])])


def _rw_params(i, rw_conv, rw_w0, rw_w2, rw_a0, rw_a2, rw_g2, rw_kk, rw_ka, rw_rk, rw_lnx_g, rw_lnx_b):
    w2 = jnp.zeros((2, LORA_ALL, RW_W), F32)
    a2 = jnp.zeros((2, LORA_ALL, RW_W), F32)
    for d in range(2):
        w2 = w2.at[d, d * LORA_W:(d + 1) * LORA_W].set(rw_w2[i, d])
        a2 = a2.at[d, 2 * LORA_W + d * LORA_A:2 * LORA_W + (d + 1) * LORA_A].set(rw_a2[i, d])
    taps = jnp.pad(rw_conv[i], ((0, SUBLANES - 3), (0, RW_COLS_PAD - RW_COLS)))
    return dict(
        taps=taps, w0=rw_w0[i], a0=rw_a0[i], w2=w2.astype(BF16), a2=a2.astype(BF16), g2=rw_g2[i].astype(BF16),
        kks=rw_kk[i][None], ka=rw_ka[i][None], rk=rw_rk[i].reshape(1, RW_W),
        lnx_g=rw_lnx_g[i][None], lnx_b=rw_lnx_b[i][None])


def kernel(x, c, ctx, c_ctx, w_ada, b_ada, norm1_g, norm2_g, w_in, rw_conv, rw_w0, rw_w2, rw_a0, rw_a2, rw_g2,
           rw_kk, rw_ka, rw_rk, rw_lnx_g, rw_lnx_b, na_rpb, sw_sink, w_out, w_router, router_bias, we_gate,
           we_up, we_down, ws_gate, ws_up, ws_down, final_g):
    assert x.shape[0] == 1 and x.shape[2] == D_MODEL and ctx.shape[1] == CTX_LEN
    xl = x[0]
    xc = ctx[0]
    cond8 = jnp.zeros((SUBLANES, D_MODEL), F32).at[0].set(c[0]).at[1].set(c_ctx)
    mods = ada_mod(cond8, w_ada, b_ada)
    w_in_bf16 = w_in.astype(BF16)

    for i in range(DEPTH):
        ctx_needed = i < DEPTH - 1
        mod_l = [m[None] for m in mods[i, 0].reshape(6, D_MODEL)]
        mod_c = [m[None] for m in mods[i, 1].reshape(6, D_MODEL)]
        w_rw = [(w_in_bf16, i, 0)]
        w_att = [(w_in_bf16[i][:, RW_COLS:][None], 0, 0)]
        w_o_parts = [(w_out, i, 0), (w_out, i, RW_W // NA_W), (w_out, i, (RW_W + NA_W) // SW_W)]
        rw_prm = _rw_params(i, rw_conv, rw_w0, rw_w2, rw_a0, rw_a2, rw_g2, rw_kk, rw_ka, rw_rk, rw_lnx_g,
                            rw_lnx_b)
        n1 = norm1_g[i][None]
        n2 = norm2_g[i][None]

        hl = norm_mod(xl, n1, mod_l[1], mod_l[0])
        hc = norm_mod(xc, n1, mod_c[1], mod_c[0])
        url = matmul([hl], w_rw, RW_COLS_PAD, 768, F32, name="in_proj_rw")
        urc = matmul([hc], w_rw, RW_COLS_PAD, 768, F32, name="in_proj_rw_ctx")
        ual = matmul([hl], w_att, ATT_COLS, 256, BF16, name="in_proj_att")
        uac = matmul([hc], w_att, ATT_COLS, 256, BF16, name="in_proj_att_ctx")

        z = jnp.zeros((RW_PAIRS, LANES, LANES), F32)
        rwc, s_f, s_b = rwkv_group(urc, rw_prm, z, z, ctx_needed)
        rwl, _, _ = rwkv_group(url, rw_prm, s_f, s_b, True)
        nal = na_latent(ual, uac, na_rpb[i])
        swl = swa_latent(rope_qk(ual), ual, uac, sw_sink[i])
        xl = matmul([rwl, nal, swl], w_o_parts, D_MODEL, 512, F32, residual=(xl, mod_l[2]), name="out_proj")

        wr = _pad_cols(w_router[i], LANES)
        rb = _pad_cols(router_bias[i][None], LANES)
        shared = (ws_gate[i].astype(BF16), ws_up[i].astype(BF16), ws_down[i].astype(BF16))

        def moe(xt, mod):
            h2, picks, hp = router(xt, n2, mod[4], mod[3], wr, rb)
            yk = moe_routed(hp, picks, i, we_gate, we_up, we_down)
            return moe_combine(h2, picks, yk, *shared, xt, mod[5])

        xl = moe(xl, mod_l)
        if ctx_needed:
            nac = na_ctx(uac)
            swc = swa_ctx(uac, sw_sink[i])
            xc = matmul([rwc, nac, swc], w_o_parts, D_MODEL, 512, F32, residual=(xc, mod_c[2]), name="out_proj_ctx")
            xc = moe(xc, mod_c)
    return final_norm(xl, final_g[None])[None]
```
